```python
import math
import jax, jax.numpy as jnp
from jax import lax
import numpy as np

D_MODEL = 1024
BATCH = 2
SEQ = 8192
DEPTH = 2

HEAD_DIM = 64
MIX_W = D_MODEL
MIX_HALF = MIX_W // 2
FOX_HEADS = MIX_HALF // HEAD_DIM
FOX_W = FOX_HEADS * HEAD_DIM
SSD_HD = HEAD_DIM
SSD_HEADS = MIX_HALF // SSD_HD
SSD_W = SSD_HEADS * SSD_HD
SSD_GROUPS = 2
SSD_STATE = 64
SSD_CONV = 4
SSD_CONV_DIM = SSD_W + 2 * SSD_GROUPS * SSD_STATE
SSD_CHUNK = 128
SB_HEADS = MIX_HALF // HEAD_DIM
SB_W = SB_HEADS * HEAD_DIM
GLA_HEADS = 4
GLA_DV = MIX_HALF // GLA_HEADS
GLA_DK = GLA_DV // 2
GLA_RANK = 16
GLA_CHUNK = 16
GLA_GATE_NORM = 16.0
D_FF = 4 * D_MODEL
Q_BLOCK = 128
EPS = 1e-5
N_EVEN = (DEPTH + 1) // 2
N_ODD = DEPTH // 2

AB_SPLITS = (FOX_W, FOX_W, FOX_W, FOX_HEADS, SSD_W, SSD_CONV_DIM, SSD_HEADS)
AB_IN = sum(AB_SPLITS)
CD_SPLITS = (SB_W, SB_W, SB_W, GLA_HEADS * GLA_DK, GLA_HEADS * GLA_DK,
             GLA_HEADS * GLA_DV, GLA_RANK, GLA_HEADS * GLA_DV)
CD_IN = sum(CD_SPLITS)

kernel_name = 'hybrid_fox_ssd_stickbreak_gla_trunk'


def _split(h, sizes):
    idx = [int(i) for i in np.cumsum(sizes)[:-1]]
    return jnp.split(h, idx, axis=-1)


def rms_norm(x, w):
    xf = x.astype(jnp.float32)
    y = xf * lax.rsqrt(jnp.mean(xf * xf, axis=-1, keepdims=True) + EPS)
    return (y * w.astype(jnp.float32)).astype(x.dtype)


def _to_heads(t, n_heads):
    b, l, _ = t.shape
    return t.reshape(b, l, n_heads, -1).transpose(0, 2, 1, 3)


def _from_heads(t):
    b, h, l, d = t.shape
    return t.transpose(0, 2, 1, 3).reshape(b, l, h * d)


def _gather_blocks(out):
    nb, b, h, q, d = out.shape
    return out.transpose(1, 2, 0, 3, 4).reshape(b, h, nb * q, d)


def forgetting_attention(q, k, v, log_f):
    L = q.shape[2]
    scale = HEAD_DIM ** -0.5
    F = jnp.cumsum(log_f, axis=-1)
    key_pos = jnp.arange(L)

    def block(i):
        start = i * Q_BLOCK
        qb = lax.dynamic_slice_in_dim(q, start, Q_BLOCK, axis=2)
        Fb = lax.dynamic_slice_in_dim(F, start, Q_BLOCK, axis=2)
        s = jnp.einsum('bhqd,bhkd->bhqk', qb, k).astype(jnp.float32) * scale
        s = s + Fb[..., :, None] - F[..., None, :]
        qpos = start + jnp.arange(Q_BLOCK)
        mask = qpos[:, None] >= key_pos[None, :]
        p = jax.nn.softmax(jnp.where(mask, s, -jnp.inf), axis=-1)
        return jnp.einsum('bhqk,bhkd->bhqd', p.astype(v.dtype), v)

    return _gather_blocks(lax.map(block, jnp.arange(L // Q_BLOCK)))


def stick_breaking_attention(q, k, v):
    L = q.shape[2]
    scale = HEAD_DIM ** -0.5
    key_pos = jnp.arange(L)

    def block(i):
        start = i * Q_BLOCK
        qb = lax.dynamic_slice_in_dim(q, start, Q_BLOCK, axis=2)
        z = jnp.einsum('bhqd,bhkd->bhqk', qb, k).astype(jnp.float32) * scale
        qpos = start + jnp.arange(Q_BLOCK)
        mask = key_pos[None, :] < qpos[:, None]
        log_beta = jax.nn.log_sigmoid(z)
        log_rest = jnp.where(mask, jax.nn.log_sigmoid(-z), 0.0)
        suffix = lax.cumsum(log_rest, axis=3, reverse=True) - log_rest
        a = jnp.where(mask, jnp.exp(log_beta + suffix), 0.0)
        return jnp.einsum('bhqk,bhkd->bhqd', a.astype(v.dtype), v)

    return _gather_blocks(lax.map(block, jnp.arange(L // Q_BLOCK)))


def causal_depthwise_conv(x, w, b):
    c = x.shape[-1]
    y = lax.conv_general_dilated(
        x, w[:, None, :].astype(x.dtype), window_strides=(1,),
        padding=[(w.shape[0] - 1, 0)], dimension_numbers=('NWC', 'WIO', 'NWC'),
        feature_group_count=c)
    return y + b.astype(x.dtype)


def _chunk_recurrence(decay, local):
    def step(s, inp):
        d, loc = inp
        return s * d + loc, s
    init = jnp.zeros_like(local[:, 0])
    _, prev = lax.scan(step, init, (jnp.moveaxis(decay, 1, 0), jnp.moveaxis(local, 1, 0)))
    return jnp.moveaxis(prev, 0, 1)


def ssd_chunked(x, dt, A, Bm, Cm, d_skip):
    b, L, H, P = x.shape
    rep = H // SSD_GROUPS
    Q = SSD_CHUNK
    nc = L // Q
    Bh = jnp.repeat(Bm, rep, axis=2).reshape(b, nc, Q, H, SSD_STATE)
    Ch = jnp.repeat(Cm, rep, axis=2).reshape(b, nc, Q, H, SSD_STATE)
    xc = x.reshape(b, nc, Q, H, P)
    dtc = dt.reshape(b, nc, Q, H)
    xdt = xc * dtc[..., None]
    a_cum = jnp.cumsum(dtc * A, axis=2)
    causal = (jnp.arange(Q)[:, None] >= jnp.arange(Q)[None, :])[:, :, None]
    seg = a_cum[:, :, :, None, :] - a_cum[:, :, None, :, :]
    lmat = jnp.exp(jnp.where(causal, seg, -jnp.inf))
    scores = jnp.einsum('bclhn,bcshn->bclsh', Ch, Bh) * lmat
    y_diag = jnp.einsum('bclsh,bcshp->bclhp', scores, xdt)
    decay_to_end = jnp.exp(a_cum[:, :, -1:, :] - a_cum)
    local = jnp.einsum('bcshn,bcsh,bcshp->bchpn', Bh, decay_to_end, xdt)
    chunk_decay = jnp.exp(a_cum[:, :, -1, :])[..., None, None]
    prev = _chunk_recurrence(chunk_decay, local)
    y_off = jnp.einsum('bclhn,bchpn,bclh->bclhp', Ch, prev, jnp.exp(a_cum))
    y = y_diag + y_off + d_skip[None, None, None, :, None] * xc
    return y.reshape(b, L, H, P)


def gla_chunked(q, k, v, log_a):
    b, L, H, K = q.shape
    V = v.shape[-1]
    Q = GLA_CHUNK
    nc = L // Q
    qc = q.reshape(b, nc, Q, H, K)
    kc = k.reshape(b, nc, Q, H, K)
    vc = v.reshape(b, nc, Q, H, V)
    g = jnp.cumsum(log_a.reshape(b, nc, Q, H, K), axis=2)
    causal = (jnp.arange(Q)[:, None] >= jnp.arange(Q)[None, :])[:, :, None, None]
    rel = g[:, :, :, None] - g[:, :, None, :]
    decay = jnp.exp(jnp.where(causal, rel, -jnp.inf))
    attn = jnp.einsum('bcthk,bcshk,bctshk->bctsh', qc, kc, decay)
    o_intra = jnp.einsum('bctsh,bcshv->bcthv', attn, vc)
    g_last = g[:, :, -1]
    k_dec = kc * jnp.exp(g_last[:, :, None] - g)
    local = jnp.einsum('bcshk,bcshv->bchkv', k_dec, vc)
    prev = _chunk_recurrence(jnp.exp(g_last)[..., None], local)
    o_inter = jnp.einsum('bcthk,bchkv->bcthv', qc * jnp.exp(g), prev)
    return (o_intra + o_inter).reshape(b, L, H, V)


def mixer_fox_ssd(h, w_in, f_bias, conv_w, conv_b, dt_bias, a_log, d_skip, ssd_norm_w, w_out):
    b, L, _ = h.shape
    proj = h @ w_in
    fq, fk, fv, f_logit, z, xbc, dt_raw = _split(proj, AB_SPLITS)
    log_f = jax.nn.log_sigmoid((f_logit + f_bias).astype(jnp.float32)).transpose(0, 2, 1)
    y_fox = forgetting_attention(_to_heads(fq, FOX_HEADS), _to_heads(fk, FOX_HEADS),
                                 _to_heads(fv, FOX_HEADS), log_f)
    y_fox = _from_heads(y_fox)
    xbc = jax.nn.silu(causal_depthwise_conv(xbc, conv_w, conv_b))
    xs, bm, cm = _split(xbc, (SSD_W, SSD_GROUPS * SSD_STATE, SSD_GROUPS * SSD_STATE))
    dt = jax.nn.softplus((dt_raw + dt_bias).astype(jnp.float32))
    A = -jnp.exp(a_log.astype(jnp.float32))
    y = ssd_chunked(xs.reshape(b, L, SSD_HEADS, SSD_HD), dt, A,
                    bm.reshape(b, L, SSD_GROUPS, SSD_STATE),
                    cm.reshape(b, L, SSD_GROUPS, SSD_STATE), d_skip.astype(jnp.float32))
    y = y.reshape(b, L, SSD_W) * jax.nn.silu(z.astype(jnp.float32))
    y = rms_norm(y.reshape(b, L, SSD_GROUPS, SSD_W // SSD_GROUPS),
                 ssd_norm_w.reshape(SSD_GROUPS, SSD_W // SSD_GROUPS)).reshape(b, L, SSD_W)
    cat = jnp.concatenate([y_fox.astype(h.dtype), y.astype(h.dtype)], axis=-1)
    return cat @ w_out


def mixer_sb_gla(h, w_in, gate_w2, gate_b, gla_norm_w, w_out):
    b, L, _ = h.shape
    proj = h @ w_in
    sq, sk, sv, gq, gk, gv, g_low, gr = _split(proj, CD_SPLITS)
    y_sb = _from_heads(stick_breaking_attention(_to_heads(sq, SB_HEADS), _to_heads(sk, SB_HEADS),
                                                _to_heads(sv, SB_HEADS)))
    gate_logits = (g_low @ gate_w2 + gate_b).astype(jnp.float32).reshape(b, L, GLA_HEADS, GLA_DK)
    log_a = jax.nn.log_sigmoid(gate_logits) / GLA_GATE_NORM
    o = gla_chunked(gq.reshape(b, L, GLA_HEADS, GLA_DK) * (GLA_DK ** -0.5),
                    gk.reshape(b, L, GLA_HEADS, GLA_DK),
                    gv.reshape(b, L, GLA_HEADS, GLA_DV), log_a)
    o = rms_norm(o, gla_norm_w).reshape(b, L, GLA_HEADS * GLA_DV) * jax.nn.silu(gr)
    cat = jnp.concatenate([y_sb.astype(h.dtype), o.astype(h.dtype)], axis=-1)
    return cat @ w_out


def squared_relu_mlp(h, w_up, w_down):
    u = jax.nn.relu(h @ w_up)
    return (u * u) @ w_down


def setup_inputs(seed: int = 0) -> dict:
    key = jax.random.key(seed)
    ks = jax.random.split(key, 20)
    f32 = jnp.float32

    def nrm(k, shape, fan):
        return jax.random.normal(k, shape, f32) * (fan ** -0.5)

    def gain(k, shape):
        return 1.0 + 0.02 * jax.random.normal(k, shape, f32)

    x = jax.random.normal(ks[0], (BATCH, SEQ, D_MODEL), f32)
    norm_mix = gain(ks[1], (DEPTH, D_MODEL))
    norm_mlp = gain(ks[2], (DEPTH, D_MODEL))
    norm_final = gain(ks[3], (D_MODEL,))
    w_in_ab = nrm(ks[4], (N_EVEN, D_MODEL, AB_IN), D_MODEL)
    fox_f_bias = jax.random.uniform(ks[5], (N_EVEN, FOX_HEADS), f32, 1.0, 4.0)
    ssd_conv_w = nrm(ks[6], (N_EVEN, SSD_CONV, SSD_CONV_DIM), SSD_CONV)
    ssd_conv_b = 0.02 * jax.random.normal(ks[7], (N_EVEN, SSD_CONV_DIM), f32)
    dt0 = jnp.exp(jax.random.uniform(ks[8], (N_EVEN, SSD_HEADS), f32,
                                     math.log(1e-3), math.log(1e-1)))
    ssd_dt_bias = dt0 + jnp.log(-jnp.expm1(-dt0))
    ssd_a_log = jnp.log(jax.random.uniform(ks[9], (N_EVEN, SSD_HEADS), f32, 1.0, 16.0))
    ssd_d = gain(ks[10], (N_EVEN, SSD_HEADS))
    ssd_norm = gain(ks[11], (N_EVEN, SSD_W))
    w_out_ab = nrm(ks[12], (N_EVEN, MIX_W, D_MODEL), MIX_W)
    w_in_cd = nrm(ks[13], (N_ODD, D_MODEL, CD_IN), D_MODEL)
    gla_gate_w2 = nrm(ks[14], (N_ODD, GLA_RANK, GLA_HEADS * GLA_DK), GLA_RANK)
    gla_gate_b = 0.02 * jax.random.normal(ks[15], (N_ODD, GLA_HEADS * GLA_DK), f32)
    gla_norm = gain(ks[16], (N_ODD, GLA_DV))
    w_out_cd = nrm(ks[17], (N_ODD, MIX_W, D_MODEL), MIX_W)
    w_mlp_up = nrm(ks[18], (DEPTH, D_MODEL, D_FF), D_MODEL)
    w_mlp_down = nrm(ks[19], (DEPTH, D_FF, D_MODEL), D_FF)
    return {'x': x, 'norm_mix': norm_mix, 'norm_mlp': norm_mlp, 'norm_final': norm_final,
            'w_in_ab': w_in_ab, 'fox_f_bias': fox_f_bias, 'ssd_conv_w': ssd_conv_w,
            'ssd_conv_b': ssd_conv_b, 'ssd_dt_bias': ssd_dt_bias, 'ssd_a_log': ssd_a_log,
            'ssd_d': ssd_d, 'ssd_norm': ssd_norm, 'w_out_ab': w_out_ab,
            'w_in_cd': w_in_cd, 'gla_gate_w2': gla_gate_w2, 'gla_gate_b': gla_gate_b,
            'gla_norm': gla_norm, 'w_out_cd': w_out_cd,
            'w_mlp_up': w_mlp_up, 'w_mlp_down': w_mlp_down}


def reference(x, norm_mix, norm_mlp, norm_final, w_in_ab, fox_f_bias, ssd_conv_w, ssd_conv_b,
              ssd_dt_bias, ssd_a_log, ssd_d, ssd_norm, w_out_ab, w_in_cd, gla_gate_w2,
              gla_gate_b, gla_norm, w_out_cd, w_mlp_up, w_mlp_down):
    h = x
    for i in range(DEPTH):
        j = i // 2
        u = rms_norm(h, norm_mix[i])
        if i % 2 == 0:
            mix = mixer_fox_ssd(u, w_in_ab[j], fox_f_bias[j], ssd_conv_w[j], ssd_conv_b[j],
                                ssd_dt_bias[j], ssd_a_log[j], ssd_d[j], ssd_norm[j], w_out_ab[j])
        else:
            mix = mixer_sb_gla(u, w_in_cd[j], gla_gate_w2[j], gla_gate_b[j], gla_norm[j],
                               w_out_cd[j])
        h = h + mix.astype(h.dtype)
        u = rms_norm(h, norm_mlp[i])
        h = h + squared_relu_mlp(u, w_mlp_up[i], w_mlp_down[i]).astype(h.dtype)
    return rms_norm(h, norm_final)
```

```python
import functools
import math

import numpy as np
import jax
import jax.numpy as jnp
from jax import lax
from jax.experimental import pallas as pl
from jax.experimental.pallas import tpu as pltpu

F32 = jnp.float32
BF16 = jnp.bfloat16

LANES = 128
SUBLANES = 8
VMEM_LIMIT_BYTES = 56 * 1024 * 1024

HEAD_DIM = 64
N_ATT_HEADS = 8
ATT_W = N_ATT_HEADS * HEAD_DIM
SSD_HEADS = 8
SSD_W = 512
SSD_GROUPS = 2
SSD_STATE = 64
SSD_CONV = 4
SSD_CONV_DIM = SSD_W + 2 * SSD_GROUPS * SSD_STATE
GLA_HEADS = 4
GLA_DK = 64
GLA_DV = 128
GLA_RANK = 16
GLA_GATE_NORM = 16.0
EPS = 1e-5
CHUNK = 128
GLA_LEVELS = 7
DT_LANE0 = 8


def _params(sem):
    return pltpu.CompilerParams(dimension_semantics=sem, vmem_limit_bytes=VMEM_LIMIT_BYTES)


def _split3(x):
    hi = x.astype(BF16)
    r = x - hi.astype(F32)
    mid = r.astype(BF16)
    lo = (r - mid.astype(F32)).astype(BF16)
    return hi, mid, lo


def _split2(x):
    hi = x.astype(BF16)
    lo = (x - hi.astype(F32)).astype(BF16)
    return hi, lo


def _softplus(x):
    return jnp.maximum(x, 0.0) + jnp.log1p(jnp.exp(-jnp.abs(x)))


def _log_sigmoid(x):
    return jnp.minimum(x, 0.0) - jnp.log1p(jnp.exp(-jnp.abs(x)))


def _silu(x):
    return x / (1.0 + jnp.exp(-x))


def _rms(x, w):
    ms = jnp.mean(x * x, axis=-1, keepdims=True)
    return x * lax.rsqrt(ms + EPS) * w


def _dot(a, b):
    return jnp.dot(a, b, preferred_element_type=F32)


def _dot_nt(a, b):
    return lax.dot_general(a, b, (((1,), (1,)), ((), ())), preferred_element_type=F32)


AB_COLS = 3 * ATT_W + SSD_W + SSD_CONV_DIM + LANES


def _inproj_ab_kernel(x_ref, nw_ref, w_ref, sb_ref, tri3_ref, sel_ref, cst_ref,
                      qd_ref, kd_ref, vd_ref, qa_ref, ka_ref, z_ref, xbc_ref, dt_ref,
                      carry_ref):
    @pl.when(pl.program_id(1) == 0)
    def _():
        carry_ref[...] = jnp.zeros_like(carry_ref)

    u = _rms(x_ref[0], nw_ref[...]).astype(BF16)

    def mm(a, b):
        return _dot(u, w_ref[:, a:b])

    o = 0
    qd_ref[0] = (mm(o, o + ATT_W) * (HEAD_DIM ** -0.5)).astype(BF16); o += ATT_W
    kd_ref[0] = mm(o, o + ATT_W).astype(BF16); o += ATT_W
    vd_ref[0] = mm(o, o + ATT_W).astype(BF16); o += ATT_W
    z_ref[0] = mm(o, o + SSD_W); o += SSD_W
    xbc_ref[0] = mm(o, o + SSD_CONV_DIM); o += SSD_CONV_DIM
    small = mm(o, o + LANES) + sb_ref[...]

    lane = lax.broadcasted_iota(jnp.int32, small.shape, 1)
    is_f = lane < N_ATT_HEADS
    is_dt = (lane >= DT_LANE0) & (lane < DT_LANE0 + SSD_HEADS)
    log_f = jnp.where(is_f, _log_sigmoid(small), 0.0)
    dt_ref[0] = jnp.where(is_dt, _softplus(small), 0.0)

    hi, mid, lo = _split3(log_f)
    cum = _dot(tri3_ref[...], jnp.concatenate([hi, mid, lo], axis=0)) + carry_ref[...]
    tm = cum.shape[0]
    carry_ref[...] = cum[tm - 1:tm, :]

    fh, fm, fl = _split3(cum)
    fcat = (fh.astype(F32) + pltpu.roll(fm.astype(F32), N_ATT_HEADS, 1)
            + pltpu.roll(fl.astype(F32), 2 * N_ATT_HEADS, 1)).astype(BF16)
    aug = _dot(fcat, sel_ref[...]) + cst_ref[...]
    qa_ref[0] = aug[:, :ATT_W].astype(BF16)
    ka_ref[0] = aug[:, ATT_W:].astype(BF16)


def _aug_tables():
    sel = np.zeros((LANES, 2 * ATT_W), np.float32)
    cst = np.zeros((1, 2 * ATT_W), np.float32)
    for h in range(N_ATT_HEADS):
        p, e = divmod(h, 2)
        for part in range(3):
            src = part * N_ATT_HEADS + h
            sel[src, p * LANES + 6 * e + part] = 1.0
            sel[src, ATT_W + p * LANES + 6 * e + 3 + part] = -1.0
            cst[0, p * LANES + 6 * e + 3 + part] = 1.0
            cst[0, ATT_W + p * LANES + 6 * e + part] = 1.0
    return jnp.asarray(sel, BF16), jnp.asarray(cst, F32)


def _tri_incl(n):
    r = np.arange(n)
    return (r[None, :] <= r[:, None]).astype(np.float32)


def _inproj_ab(h, nw, w_in, f_bias, dt_bias, tm):
    B, L, D = h.shape
    fq, fk, fv, fl, wz, wxbc, wdt = jnp.split(w_in, np.cumsum(
        [ATT_W, ATT_W, ATT_W, N_ATT_HEADS, SSD_W, SSD_CONV_DIM])[:].tolist(), axis=1)
    pad = jnp.zeros((D, LANES - N_ATT_HEADS - SSD_HEADS), w_in.dtype)
    w = jnp.concatenate([fq, fk, fv, wz, wxbc, fl, wdt, pad], axis=1).astype(BF16)
    sb = jnp.concatenate([f_bias, dt_bias, jnp.zeros((LANES - 16,), F32)]).reshape(1, LANES)
    tri = _tri_incl(tm)
    tri3 = jnp.asarray(np.concatenate([tri, tri, tri], axis=1), BF16)
    sel, cst = _aug_tables()

    row = lambda width: pl.BlockSpec((1, tm, width), lambda b, i: (b, i, 0))
    const = lambda shape: pl.BlockSpec(shape, lambda b, i: (0,) * len(shape))
    outs = [
        jax.ShapeDtypeStruct((B, L, ATT_W), BF16),
        jax.ShapeDtypeStruct((B, L, ATT_W), BF16),
        jax.ShapeDtypeStruct((B, L, ATT_W), BF16),
        jax.ShapeDtypeStruct((B, L, ATT_W), BF16),
        jax.ShapeDtypeStruct((B, L, ATT_W), BF16),
        jax.ShapeDtypeStruct((B, L, SSD_W), F32),
        jax.ShapeDtypeStruct((B, L, SSD_CONV_DIM), F32),
        jax.ShapeDtypeStruct((B, L, LANES), F32),
    ]
    return pl.pallas_call(
        _inproj_ab_kernel,
        grid=(B, L // tm),
        in_specs=[row(D), const((1, D)), const((D, AB_COLS)), const((1, LANES)),
                  const((tm, 3 * tm)), const((LANES, 2 * ATT_W)), const((1, 2 * ATT_W))],
        out_specs=[row(ATT_W)] * 5 + [row(SSD_W), row(SSD_CONV_DIM), row(LANES)],
        out_shape=outs,
        scratch_shapes=[pltpu.VMEM((1, LANES), F32)],
        compiler_params=_params(("arbitrary", "arbitrary")),
        name="inproj_ab",
    )(h, nw.reshape(1, D), w, sb, tri3, sel, cst)


def _head_lane_mask(e, shape):
    lane = lax.broadcasted_iota(jnp.int32, shape, 1)
    return (lane < HEAD_DIM) if e == 0 else (lane >= HEAD_DIM)


def _fox_kernel(qd_ref, qa_ref, kd_ref, ka_ref, v_ref, o_ref, *, tq):
    i = pl.program_id(2)
    qd = qd_ref[0].astype(F32)
    qa = qa_ref[0].astype(F32)
    lane = lax.broadcasted_iota(jnp.int32, (tq, LANES), 1)
    rows = lax.broadcasted_iota(jnp.int32, (tq, tq), 0)
    cols = lax.broadcasted_iota(jnp.int32, (tq, tq), 1)
    causal = rows >= cols
    outs = []
    for e in range(2):
        dmask = _head_lane_mask(e, (tq, LANES))
        amask = (lane >= 6 * e) & (lane < 6 * e + 6)
        q = jnp.concatenate([jnp.where(dmask, qd, 0.0), jnp.where(amask, qa, 0.0)],
                            axis=1).astype(BF16)

        def step(j, carry, masked):
            m, l, acc = carry
            ks = pl.multiple_of(j * tq, tq)
            k = jnp.concatenate([kd_ref[0, pl.ds(ks, tq), :], ka_ref[0, pl.ds(ks, tq), :]], axis=1)
            s = _dot_nt(q, k)
            if masked:
                s = jnp.where(causal, s, -jnp.inf)
            m_new = jnp.maximum(m, jnp.max(s, axis=1, keepdims=True))
            alpha = jnp.exp(m - m_new)
            p = jnp.exp(s - m_new)
            l = alpha * l + jnp.sum(p, axis=1, keepdims=True)
            acc = alpha * acc + _dot(p.astype(BF16), v_ref[0, pl.ds(ks, tq), :])
            return m_new, l, acc

        init = (jnp.full((tq, 1), -jnp.inf, F32), jnp.zeros((tq, 1), F32),
                jnp.zeros((tq, LANES), F32))
        carry = lax.fori_loop(0, i, functools.partial(step, masked=False), init)
        _, l, acc = step(i, carry, True)
        outs.append(acc / l)
    o_ref[0] = jnp.where(_head_lane_mask(0, (tq, LANES)), outs[0], outs[1]).astype(BF16)


def _fox_attention(qd, qa, kd, ka, v, tq):
    B, L, _ = qd.shape
    n_pairs = N_ATT_HEADS // 2
    qspec = pl.BlockSpec((1, tq, LANES), lambda b, p, i: (b, i, p))
    kspec = pl.BlockSpec((1, L, LANES), lambda b, p, i: (b, 0, p))
    return pl.pallas_call(
        functools.partial(_fox_kernel, tq=tq),
        grid=(B, n_pairs, L // tq),
        in_specs=[qspec, qspec, kspec, kspec, kspec],
        out_specs=qspec,
        out_shape=jax.ShapeDtypeStruct((B, L, ATT_W), BF16),
        compiler_params=_params(("arbitrary", "arbitrary", "arbitrary")),
        name="fox_attention",
    )(qd, qa, kd, ka, v)


def _ssd_kernel(xbc_ref, z_ref, dt_ref, cw_ref, cb_ref, alog_ref, dsk_ref, nw_ref, tri3_ref,
                y_ref, xext_ref, st_ref):
    Q = CHUNK

    @pl.when(pl.program_id(1) == 0)
    def _():
        xext_ref[0:SUBLANES, :] = jnp.zeros((SUBLANES, SSD_CONV_DIM), F32)
        st_ref[...] = jnp.zeros_like(st_ref)

    xext_ref[SUBLANES:SUBLANES + Q, :] = xbc_ref[0]
    conv = cb_ref[...]
    for k in range(SSD_CONV):
        conv = conv + cw_ref[k:k + 1, :] * xext_ref[pl.ds(SUBLANES - (SSD_CONV - 1) + k, Q), :]
    xext_ref[0:SUBLANES, :] = xext_ref[Q:Q + SUBLANES, :]
    xc = _silu(conv)
    xs = xc[:, :SSD_W]
    bm = xc[:, SSD_W:SSD_W + LANES]
    cm = xc[:, SSD_W + LANES:]

    dt = dt_ref[0]
    a = dt * (-jnp.exp(alog_ref[...]))
    hi, mid, lo = _split3(a)
    a_cum = _dot(tri3_ref[...], jnp.concatenate([hi, mid, lo], axis=0))
    a_row = a_cum.T
    a_last_col = a_cum[Q - 1:Q, :]
    bm_t = bm.T

    rows = lax.broadcasted_iota(jnp.int32, (Q, Q), 0)
    cols = lax.broadcasted_iota(jnp.int32, (Q, Q), 1)
    causal = rows >= cols
    lane = lax.broadcasted_iota(jnp.int32, (Q, LANES), 1)
    first_half = lane < HEAD_DIM

    scores = []
    cmask = []
    for g in range(SSD_GROUPS):
        cg = jnp.where(_head_lane_mask(g, (Q, LANES)), cm, 0.0).astype(BF16)
        cmask.append(cg)
        scores.append(_dot_nt(cg, bm.astype(BF16)))

    y_pairs = []
    heads_per_group = SSD_HEADS // SSD_GROUPS
    for p in range(SSD_HEADS // 2):
        xs_pair = xs[:, p * LANES:(p + 1) * LANES]
        dt_pair = jnp.where(first_half, dt[:, DT_LANE0 + 2 * p:DT_LANE0 + 2 * p + 1],
                            dt[:, DT_LANE0 + 2 * p + 1:DT_LANE0 + 2 * p + 2])
        xdt = (xs_pair * dt_pair).astype(BF16)
        y_head = []
        for e in range(2):
            h = 2 * p + e
            g = h // heads_per_group
            hl = DT_LANE0 + h
            acol = a_cum[:, hl:hl + 1]
            arow = a_row[hl:hl + 1, :]
            alast = a_last_col[:, hl:hl + 1]
            lmat = jnp.exp(jnp.where(causal, acol - arow, -jnp.inf))
            y_diag = _dot((scores[g] * lmat).astype(BF16), xdt)
            prev = st_ref[h]
            y_off = _dot(cmask[g], prev.astype(BF16)) * jnp.exp(acol)
            y_head.append(y_diag + y_off)
            decay_row = jnp.exp(alast - arow)
            local = _dot((bm_t * decay_row).astype(BF16), xdt)
            st_ref[h] = prev * jnp.exp(alast) + local
        y_pairs.append(jnp.where(first_half, y_head[0], y_head[1]))
    y = jnp.concatenate(y_pairs, axis=1) + dsk_ref[...] * xs
    y = y * _silu(z_ref[0])
    gw = SSD_W // SSD_GROUPS
    y = jnp.concatenate([_rms(y[:, g * gw:(g + 1) * gw], nw_ref[:, g * gw:(g + 1) * gw])
                         for g in range(SSD_GROUPS)], axis=1)
    y_ref[0] = y.astype(BF16)


def _ssd(xbc, z, dt, conv_w, conv_b, a_log, d_skip, norm_w):
    B, L, _ = xbc.shape
    Q = CHUNK
    alog = jnp.zeros((1, LANES), F32).at[0, DT_LANE0:DT_LANE0 + SSD_HEADS].set(a_log)
    dsk = jnp.repeat(d_skip, SSD_W // SSD_HEADS).reshape(1, SSD_W)
    tri = _tri_incl(Q)
    tri3 = jnp.asarray(np.concatenate([tri, tri, tri], axis=1), BF16)
    row = lambda width: pl.BlockSpec((1, Q, width), lambda b, c: (b, c, 0))
    const = lambda shape: pl.BlockSpec(shape, lambda b, c: (0,) * len(shape))
    return pl.pallas_call(
        _ssd_kernel,
        grid=(B, L // Q),
        in_specs=[row(SSD_CONV_DIM), row(SSD_W), row(LANES), const((SSD_CONV, SSD_CONV_DIM)),
                  const((1, SSD_CONV_DIM)), const((1, LANES)), const((1, SSD_W)),
                  const((1, SSD_W)), const((Q, 3 * Q))],
        out_specs=row(SSD_W),
        out_shape=jax.ShapeDtypeStruct((B, L, SSD_W), BF16),
        scratch_shapes=[pltpu.VMEM((Q + SUBLANES, SSD_CONV_DIM), F32),
                        pltpu.VMEM((SSD_HEADS, LANES, LANES), F32)],
        compiler_params=_params(("arbitrary", "arbitrary")),
        name="ssd_scan",
    )(xbc, z, dt, conv_w, conv_b.reshape(1, -1), alog, dsk, norm_w.reshape(1, -1), tri3)


def _mix_mlp_kernel(h_ref, y1_ref, y2_ref, wo_ref, nw_ref, wup_ref, wdn_ref, nf_ref, o_ref,
                    *, ff_chunk, final):
    half = y1_ref.shape[-1]
    mix = _dot(y1_ref[...], wo_ref[0:half, :]) + _dot(y2_ref[...], wo_ref[half:2 * half, :])
    h1 = h_ref[...] + mix
    u = _rms(h1, nw_ref[...]).astype(BF16)
    d_ff = wup_ref.shape[1]
    acc = jnp.zeros_like(h1)
    for c in range(d_ff // ff_chunk):
        act = jnp.maximum(_dot(u, wup_ref[:, c * ff_chunk:(c + 1) * ff_chunk]), 0.0)
        acc = acc + _dot((act * act).astype(BF16), wdn_ref[c * ff_chunk:(c + 1) * ff_chunk, :])
    h2 = h1 + acc
    if final:
        h2 = _rms(h2, nf_ref[...])
    o_ref[...] = h2


def _mix_mlp(h, y1, y2, w_out, nw, w_up, w_down, nf, tm, final):
    B, L, D = h.shape
    T = B * L
    d_ff = w_up.shape[1]
    half = y1.shape[-1]
    row = lambda width: pl.BlockSpec((tm, width), lambda i: (i, 0))
    const = lambda shape: pl.BlockSpec(shape, lambda i: (0,) * len(shape))
    out = pl.pallas_call(
        functools.partial(_mix_mlp_kernel, ff_chunk=min(1024, d_ff), final=final),
        grid=(T // tm,),
        in_specs=[row(D), row(half), row(half), const((2 * half, D)), const((1, D)),
                  const((D, d_ff)), const((d_ff, D)), const((1, D))],
        out_specs=row(D),
        out_shape=jax.ShapeDtypeStruct((T, D), F32),
        compiler_params=_params(("arbitrary",)),
        name="mix_mlp",
    )(h.reshape(T, D), y1.reshape(T, half), y2.reshape(T, half), w_out.astype(BF16),
      nw.reshape(1, D), w_up.astype(BF16), w_down.astype(BF16), nf.reshape(1, D))
    return out.reshape(B, L, D)


GLA_QK = GLA_HEADS * GLA_DK
GLA_V = GLA_HEADS * GLA_DV
CD_COLS = 3 * ATT_W + 2 * GLA_QK + 2 * GLA_V + LANES


def _inproj_cd_kernel(x_ref, nw_ref, w_ref, sq_ref, sk_ref, sv_ref, gq_ref, gk_ref, gv_ref,
                      gr_ref, glow_ref):
    u = _rms(x_ref[...], nw_ref[...]).astype(BF16)

    def mm(a, b):
        return _dot(u, w_ref[:, a:b])

    o = 0
    sq_ref[...] = (mm(o, o + ATT_W) * (HEAD_DIM ** -0.5)).astype(BF16); o += ATT_W
    sk_ref[...] = mm(o, o + ATT_W).astype(BF16); o += ATT_W
    sv_ref[...] = mm(o, o + ATT_W).astype(BF16); o += ATT_W
    gq_ref[...] = mm(o, o + GLA_QK) * (GLA_DK ** -0.5); o += GLA_QK
    gk_ref[...] = mm(o, o + GLA_QK); o += GLA_QK
    gv_ref[...] = mm(o, o + GLA_V); o += GLA_V
    gr_ref[...] = mm(o, o + GLA_V); o += GLA_V
    glow_ref[...] = mm(o, o + LANES)


def _inproj_cd(h, nw, w_in, tm):
    B, L, D = h.shape
    T = B * L
    sq, sk, sv, gq, gk, gv, glow, gr = jnp.split(w_in, np.cumsum(
        [ATT_W, ATT_W, ATT_W, GLA_QK, GLA_QK, GLA_V, GLA_RANK]).tolist(), axis=1)
    pad = jnp.zeros((D, LANES - GLA_RANK), w_in.dtype)
    w = jnp.concatenate([sq, sk, sv, gq, gk, gv, gr, glow, pad], axis=1).astype(BF16)
    row = lambda width: pl.BlockSpec((tm, width), lambda i: (i, 0))
    const = lambda shape: pl.BlockSpec(shape, lambda i: (0,) * len(shape))
    widths = [ATT_W, ATT_W, ATT_W, GLA_QK, GLA_QK, GLA_V, GLA_V, LANES]
    dtypes = [BF16, BF16, BF16, F32, F32, F32, F32, F32]
    outs = pl.pallas_call(
        _inproj_cd_kernel,
        grid=(T // tm,),
        in_specs=[row(D), const((1, D)), const((D, CD_COLS))],
        out_specs=[row(wd) for wd in widths],
        out_shape=[jax.ShapeDtypeStruct((T, wd), dt) for wd, dt in zip(widths, dtypes)],
        compiler_params=_params(("arbitrary",)),
        name="inproj_cd",
    )(h.reshape(T, D), nw.reshape(1, D), w)
    return [o.reshape(B, L, -1) for o in outs]


def _sb_kernel(q_ref, k_ref, v_ref, u2_ref, o_ref, *, tq):
    i = pl.program_id(2)
    qf = q_ref[0].astype(F32)
    rows = lax.broadcasted_iota(jnp.int32, (tq, tq), 0)
    cols = lax.broadcasted_iota(jnp.int32, (tq, tq), 1)
    strict = cols < rows
    outs = []
    for e in range(2):
        q = jnp.where(_head_lane_mask(e, (tq, LANES)), qf, 0.0).astype(BF16)

        def step(j, carry, masked):
            c, acc = carry
            ks = pl.multiple_of(j * tq, tq)
            z = _dot_nt(q, k_ref[0, pl.ds(ks, tq), :])
            lr = -_softplus(z)
            if masked:
                lr = jnp.where(strict, lr, 0.0)
            hi, lo = _split2(lr)
            incl = _dot(jnp.concatenate([hi, lo], axis=1), u2_ref[...])
            a = jnp.exp(z + c + incl)
            if masked:
                a = jnp.where(strict, a, 0.0)
            acc = acc + _dot(a.astype(BF16), v_ref[0, pl.ds(ks, tq), :])
            c = c + jnp.sum(lr, axis=1, keepdims=True)
            return c, acc

        init = (jnp.zeros((tq, 1), F32), jnp.zeros((tq, LANES), F32))
        carry = step(i, init, True)
        _, acc = lax.fori_loop(0, i, lambda t, cr: step(i - 1 - t, cr, False), carry)
        outs.append(acc)
    o_ref[0] = jnp.where(_head_lane_mask(0, (tq, LANES)), outs[0], outs[1]).astype(BF16)


def _sb_attention(q, k, v, tq):
    B, L, _ = q.shape
    n_pairs = N_ATT_HEADS // 2
    r = np.arange(tq)
    u = (r[:, None] >= r[None, :]).astype(np.float32)
    u2 = jnp.asarray(np.concatenate([u, u], axis=0), BF16)
    qspec = pl.BlockSpec((1, tq, LANES), lambda b, p, i: (b, i, p))
    kspec = pl.BlockSpec((1, L, LANES), lambda b, p, i: (b, 0, p))
    return pl.pallas_call(
        functools.partial(_sb_kernel, tq=tq),
        grid=(B, n_pairs, L // tq),
        in_specs=[qspec, kspec, kspec, pl.BlockSpec((2 * tq, tq), lambda b, p, i: (0, 0))],
        out_specs=qspec,
        out_shape=jax.ShapeDtypeStruct((B, L, ATT_W), BF16),
        compiler_params=_params(("arbitrary", "arbitrary", "arbitrary")),
        name="sb_attention",
    )(q, k, v, u2)


def _gla_tables():
    Q = CHUNK
    r = np.arange(Q)
    j = np.arange(Q)
    coef = [(j[None, :] <= r[:, None]), (j[None, :] > r[:, None])]
    masks = [np.eye(Q, dtype=bool)]
    for lvl in range(GLA_LEVELS):
        m = 1 << lvl
        c0 = (r // (2 * m)) * (2 * m)
        mid = c0 + m - 1
        upper = (r - c0) >= m
        up = (j[None, :] > mid[:, None]) & (j[None, :] <= r[:, None])
        lowr = (j[None, :] > r[:, None]) & (j[None, :] <= mid[:, None])
        coef.append(np.where(upper[:, None], up, lowr))
        masks.append((c0[:, None] == c0[None, :]) & upper[:, None] & (~upper)[None, :])
    coef = np.concatenate(coef, axis=0).astype(np.float32)
    coef2 = np.concatenate([coef, coef], axis=1)
    masks = np.stack(masks).astype(np.float32)
    hv = np.arange(GLA_V) // GLA_DV
    hk = np.arange(GLA_QK) // GLA_DK
    bdiag = (hv[:, None] == hk[None, :]).astype(np.float32)
    return jnp.asarray(coef2, BF16), jnp.asarray(masks, F32), jnp.asarray(bdiag, F32)


def _gla_kernel(gq_ref, gk_ref, gv_ref, glow_ref, gr_ref, w2_ref, gb_ref, coef_ref, mask_ref,
                bdiag_ref, nw_ref, o_ref, st_ref):
    Q = CHUNK

    @pl.when(pl.program_id(1) == 0)
    def _():
        st_ref[...] = jnp.zeros_like(st_ref)

    logits = _dot(glow_ref[0].astype(BF16), w2_ref[...]) + gb_ref[...]
    la = _log_sigmoid(logits) * (1.0 / GLA_GATE_NORM)
    hi, lo = _split2(la)
    expo = _dot(coef_ref[...], jnp.concatenate([hi, lo], axis=0))
    q = gq_ref[0]
    k = gk_ref[0]
    v = gv_ref[0]
    lane_head = lax.broadcasted_iota(jnp.int32, (Q, GLA_QK), 1) // GLA_DK
    row = lax.broadcasted_iota(jnp.int32, (Q, GLA_QK), 0)
    hmask = [lane_head == h for h in range(GLA_HEADS)]

    att = [None] * GLA_HEADS
    for lvl in range(-1, GLA_LEVELS):
        if lvl < 0:
            xq, xk = q, k.astype(BF16)
        else:
            m = 1 << lvl
            upper = (row & (2 * m - 1)) >= m
            xq = jnp.where(upper, q, k) * jnp.exp(expo[(2 + lvl) * Q:(3 + lvl) * Q, :])
            xk = xq.astype(BF16)
        msk = mask_ref[lvl + 1]
        for h in range(GLA_HEADS):
            part = _dot_nt(jnp.where(hmask[h], xq, 0.0).astype(BF16), xk) * msk
            att[h] = part if att[h] is None else att[h] + part

    st = st_ref[...]
    q_in = (q * jnp.exp(expo[0:Q, :])).astype(BF16)
    o = _dot_nt(q_in, st.astype(BF16))
    o_intra = [_dot(att[h].astype(BF16), v[:, h * GLA_DV:(h + 1) * GLA_DV].astype(BF16))
               for h in range(GLA_HEADS)]
    o = o + jnp.concatenate(o_intra, axis=1)

    k_dec = (k * jnp.exp(expo[Q:2 * Q, :])).astype(BF16)
    upd = _dot(v.T.astype(BF16), k_dec)
    g_last = expo[Q - 1:Q, :]
    st_ref[...] = st * jnp.exp(g_last) + upd * bdiag_ref[...]

    gr = gr_ref[0]
    o = jnp.concatenate([_rms(o[:, h * GLA_DV:(h + 1) * GLA_DV], nw_ref[...])
                         for h in range(GLA_HEADS)], axis=1)
    o_ref[0] = (o * _silu(gr)).astype(BF16)


def _gla(gq, gk, gv, glow, gr, gate_w2, gate_b, norm_w):
    B, L, _ = gq.shape
    Q = CHUNK
    coef2, masks, bdiag = _gla_tables()
    w2 = jnp.zeros((LANES, GLA_QK), F32).at[:GLA_RANK].set(gate_w2).astype(BF16)
    row = lambda width: pl.BlockSpec((1, Q, width), lambda b, c: (b, c, 0))
    const = lambda shape: pl.BlockSpec(shape, lambda b, c: (0,) * len(shape))
    return pl.pallas_call(
        _gla_kernel,
        grid=(B, L // Q),
        in_specs=[row(GLA_QK), row(GLA_QK), row(GLA_V), row(LANES), row(GLA_V),
                  const((LANES, GLA_QK)), const((1, GLA_QK)), const(coef2.shape),
                  const(masks.shape), const(bdiag.shape), const((1, GLA_DV))],
        out_specs=row(GLA_V),
        out_shape=jax.ShapeDtypeStruct((B, L, GLA_V), BF16),
        scratch_shapes=[pltpu.VMEM((GLA_V, GLA_QK), F32)],
        compiler_params=_params(("arbitrary", "arbitrary")),
        name="gla_scan",
    )(gq, gk, gv, glow, gr, w2, gate_b.reshape(1, -1), coef2, masks, bdiag,
      norm_w.reshape(1, -1))


def _block(n, want):
    return want if n % want == 0 else n


def kernel(x, norm_mix, norm_mlp, norm_final, w_in_ab, fox_f_bias, ssd_conv_w, ssd_conv_b,
           ssd_dt_bias, ssd_a_log, ssd_d, ssd_norm, w_out_ab, w_in_cd, gla_gate_w2,
           gla_gate_b, gla_norm, w_out_cd, w_mlp_up, w_mlp_down):
    B, L, D = x.shape
    assert L % CHUNK == 0
    tm = _block(L, 512)
    tq = _block(L, 256)

    qd, kd, vd, qa, ka, z, xbc, dt = _inproj_ab(x, norm_mix[0], w_in_ab[0], fox_f_bias[0],
                                                ssd_dt_bias[0], tm)
    y_fox = _fox_attention(qd, qa, kd, ka, vd, tq)
    y_ssd = _ssd(xbc, z, dt, ssd_conv_w[0], ssd_conv_b[0], ssd_a_log[0], ssd_d[0], ssd_norm[0])
    h = _mix_mlp(x, y_fox, y_ssd, w_out_ab[0], norm_mlp[0], w_mlp_up[0], w_mlp_down[0],
                 norm_final, tm, final=False)

    sq, sk, sv, gq, gk, gv, gr, glow = _inproj_cd(h, norm_mix[1], w_in_cd[0], tm)
    y_sb = _sb_attention(sq, sk, sv, tq)
    y_gla = _gla(gq, gk, gv, glow, gr, gla_gate_w2[0], gla_gate_b[0], gla_norm[0])
    return _mix_mlp(h, y_sb, y_gla, w_out_cd[0], norm_mlp[1], w_mlp_up[1], w_mlp_down[1],
                    norm_final, tm, final=True)
```

```python
import functools
import math

import numpy as np
import jax
import jax.numpy as jnp
from jax import lax
from jax.experimental import pallas as pl
from jax.experimental.pallas import tpu as pltpu

F32 = jnp.float32
BF16 = jnp.bfloat16

LANES = 128
SUBLANES = 8
VMEM_LIMIT_BYTES = 56 * 1024 * 1024

HEAD_DIM = 64
N_ATT_HEADS = 8
ATT_W = N_ATT_HEADS * HEAD_DIM
SSD_HEADS = 8
SSD_W = 512
SSD_GROUPS = 2
SSD_STATE = 64
SSD_CONV = 4
SSD_CONV_DIM = SSD_W + 2 * SSD_GROUPS * SSD_STATE
GLA_HEADS = 4
GLA_DK = 64
GLA_DV = 128
GLA_RANK = 16
GLA_GATE_NORM = 16.0
EPS = 1e-5
CHUNK = 128
GLA_LEVELS = 7
DT_LANE0 = 8


def _params(sem):
    return pltpu.CompilerParams(dimension_semantics=sem, vmem_limit_bytes=VMEM_LIMIT_BYTES)


def _split3(x):
    hi = x.astype(BF16)
    r = x - hi.astype(F32)
    mid = r.astype(BF16)
    lo = (r - mid.astype(F32)).astype(BF16)
    return hi, mid, lo


def _split2(x):
    hi = x.astype(BF16)
    lo = (x - hi.astype(F32)).astype(BF16)
    return hi, lo


def _softplus(x):
    return jnp.maximum(x, 0.0) + jnp.log1p(jnp.exp(-jnp.abs(x)))


def _log_sigmoid(x):
    return jnp.minimum(x, 0.0) - jnp.log1p(jnp.exp(-jnp.abs(x)))


def _silu(x):
    return x / (1.0 + jnp.exp(-x))


def _rms(x, w):
    ms = jnp.mean(x * x, axis=-1, keepdims=True)
    return x * lax.rsqrt(ms + EPS) * w


def _dot(a, b):
    return jnp.dot(a, b, preferred_element_type=F32)


def _dot_nt(a, b):
    return lax.dot_general(a, b, (((1,), (1,)), ((), ())), preferred_element_type=F32)


AB_COLS = 3 * ATT_W + SSD_W + SSD_CONV_DIM + LANES


def _inproj_ab_kernel(x_ref, nw_ref, w_ref, sb_ref, tri3_ref, sel_ref, cst_ref,
                      qd_ref, kd_ref, vd_ref, qa_ref, ka_ref, z_ref, xbc_ref, dt_ref,
                      carry_ref):
    @pl.when(pl.program_id(1) == 0)
    def _():
        carry_ref[...] = jnp.zeros_like(carry_ref)

    u = _rms(x_ref[0], nw_ref[...]).astype(BF16)

    def mm(a, b):
        return _dot(u, w_ref[:, a:b])

    o = 0
    qd_ref[0] = (mm(o, o + ATT_W) * (HEAD_DIM ** -0.5)).astype(BF16); o += ATT_W
    kd_ref[0] = mm(o, o + ATT_W).astype(BF16); o += ATT_W
    vd_ref[0] = mm(o, o + ATT_W).astype(BF16); o += ATT_W
    z_ref[0] = mm(o, o + SSD_W); o += SSD_W
    xbc_ref[0] = mm(o, o + SSD_CONV_DIM); o += SSD_CONV_DIM
    small = mm(o, o + LANES) + sb_ref[...]

    lane = lax.broadcasted_iota(jnp.int32, small.shape, 1)
    is_f = lane < N_ATT_HEADS
    is_dt = (lane >= DT_LANE0) & (lane < DT_LANE0 + SSD_HEADS)
    log_f = jnp.where(is_f, _log_sigmoid(small), 0.0)
    dt_ref[0] = jnp.where(is_dt, _softplus(small), 0.0)

    hi, mid, lo = _split3(log_f)
    cum = _dot(tri3_ref[...], jnp.concatenate([hi, mid, lo], axis=0)) + carry_ref[...]
    tm = cum.shape[0]
    carry_ref[...] = cum[tm - 1:tm, :]

    fh, fm, fl = _split3(cum)
    fcat = (fh.astype(F32) + pltpu.roll(fm.astype(F32), N_ATT_HEADS, 1)
            + pltpu.roll(fl.astype(F32), 2 * N_ATT_HEADS, 1)).astype(BF16)
    aug = _dot(fcat, sel_ref[...]) + cst_ref[...]
    qa_ref[0] = aug[:, :ATT_W].astype(BF16)
    ka_ref[0] = aug[:, ATT_W:].astype(BF16)


def _aug_tables():
    sel = np.zeros((LANES, 2 * ATT_W), np.float32)
    cst = np.zeros((1, 2 * ATT_W), np.float32)
    for h in range(N_ATT_HEADS):
        p, e = divmod(h, 2)
        for part in range(3):
            src = part * N_ATT_HEADS + h
            sel[src, p * LANES + 6 * e + part] = 1.0
            sel[src, ATT_W + p * LANES + 6 * e + 3 + part] = -1.0
            cst[0, p * LANES + 6 * e + 3 + part] = 1.0
            cst[0, ATT_W + p * LANES + 6 * e + part] = 1.0
    return jnp.asarray(sel, BF16), jnp.asarray(cst, F32)


def _tri_incl(n):
    r = np.arange(n)
    return (r[None, :] <= r[:, None]).astype(np.float32)


def _inproj_ab(h, nw, w_in, f_bias, dt_bias, tm):
    B, L, D = h.shape
    fq, fk, fv, fl, wz, wxbc, wdt = jnp.split(w_in, np.cumsum(
        [ATT_W, ATT_W, ATT_W, N_ATT_HEADS, SSD_W, SSD_CONV_DIM])[:].tolist(), axis=1)
    pad = jnp.zeros((D, LANES - N_ATT_HEADS - SSD_HEADS), w_in.dtype)
    w = jnp.concatenate([fq, fk, fv, wz, wxbc, fl, wdt, pad], axis=1).astype(BF16)
    sb = jnp.concatenate([f_bias, dt_bias, jnp.zeros((LANES - 16,), F32)]).reshape(1, LANES)
    tri = _tri_incl(tm)
    tri3 = jnp.asarray(np.concatenate([tri, tri, tri], axis=1), BF16)
    sel, cst = _aug_tables()

    row = lambda width: pl.BlockSpec((1, tm, width), lambda b, i: (b, i, 0))
    const = lambda shape: pl.BlockSpec(shape, lambda b, i: (0,) * len(shape))
    outs = [
        jax.ShapeDtypeStruct((B, L, ATT_W), BF16),
        jax.ShapeDtypeStruct((B, L, ATT_W), BF16),
        jax.ShapeDtypeStruct((B, L, ATT_W), BF16),
        jax.ShapeDtypeStruct((B, L, ATT_W), BF16),
        jax.ShapeDtypeStruct((B, L, ATT_W), BF16),
        jax.ShapeDtypeStruct((B, L, SSD_W), F32),
        jax.ShapeDtypeStruct((B, L, SSD_CONV_DIM), F32),
        jax.ShapeDtypeStruct((B, L, LANES), F32),
    ]
    return pl.pallas_call(
        _inproj_ab_kernel,
        grid=(B, L // tm),
        in_specs=[row(D), const((1, D)), const((D, AB_COLS)), const((1, LANES)),
                  const((tm, 3 * tm)), const((LANES, 2 * ATT_W)), const((1, 2 * ATT_W))],
        out_specs=[row(ATT_W)] * 5 + [row(SSD_W), row(SSD_CONV_DIM), row(LANES)],
        out_shape=outs,
        scratch_shapes=[pltpu.VMEM((1, LANES), F32)],
        compiler_params=_params(("arbitrary", "arbitrary")),
        name="inproj_ab",
    )(h, nw.reshape(1, D), w, sb, tri3, sel, cst)


def _head_lane_mask(e, shape):
    lane = lax.broadcasted_iota(jnp.int32, shape, 1)
    return (lane < HEAD_DIM) if e == 0 else (lane >= HEAD_DIM)


def _fox_kernel(qd_ref, qa_ref, kd_ref, ka_ref, v_ref, o_ref, *, tq):
    i = pl.program_id(2)
    qd = qd_ref[0].astype(F32)
    qa = qa_ref[0].astype(F32)
    lane = lax.broadcasted_iota(jnp.int32, (tq, LANES), 1)
    qs = []
    for e in range(2):
        amask = (lane >= 6 * e) & (lane < 6 * e + 6)
        qs.append(jnp.concatenate([jnp.where(_head_lane_mask(e, (tq, LANES)), qd, 0.0),
                                   jnp.where(amask, qa, 0.0)], axis=1))
    q = jnp.concatenate(qs, axis=0).astype(BF16)
    tk = tq // 2
    rows = lax.broadcasted_iota(jnp.int32, (2 * tq, tk), 0) & (tq - 1)
    cols = lax.broadcasted_iota(jnp.int32, (2 * tq, tk), 1)
    nf = 2 * i

    def scores(j):
        ks = pl.multiple_of(j * tk, tk)
        k = jnp.concatenate([kd_ref[0, pl.ds(ks, tk), :], ka_ref[0, pl.ds(ks, tk), :]], axis=1)
        return _dot_nt(q, k)

    def update(j, s, m, l, acc, diag):
        if diag is not None:
            s = jnp.where(rows >= cols + diag * tk, s, -jnp.inf)
        m_new = jnp.maximum(m, jnp.max(s, axis=1, keepdims=True))
        alpha = jnp.exp(m - m_new)
        p = jnp.exp(s - m_new)
        l = alpha * l + jnp.sum(p, axis=1, keepdims=True)
        ks = pl.multiple_of(j * tk, tk)
        acc = alpha * acc + _dot(p.astype(BF16), v_ref[0, pl.ds(ks, tk), :])
        return m_new, l, acc

    state = (jnp.full((2 * tq, 1), -jnp.inf, F32), jnp.zeros((2 * tq, 1), F32),
             jnp.zeros((2 * tq, LANES), F32))
    s_next = scores(jnp.where(nf > 0, 0, nf + 1))
    state = update(nf, scores(nf), *state, diag=0)

    def body(t, carry):
        s, m, l, acc = carry
        s_next = scores(t + jnp.where(t < nf, 0, 1))
        return (s_next,) + update(t - 1, s, m, l, acc, None)

    s, m, l, acc = lax.fori_loop(1, nf + 1, body, (s_next,) + state)
    _, l, acc = update(nf + 1, s, m, l, acc, diag=1)
    out = acc / l
    o_ref[0] = jnp.where(_head_lane_mask(0, (tq, LANES)), out[:tq], out[tq:]).astype(BF16)


def _fox_attention(qd, qa, kd, ka, v, tq):
    B, L, _ = qd.shape
    n_pairs = N_ATT_HEADS // 2
    qspec = pl.BlockSpec((1, tq, LANES), lambda b, p, i: (b, i, p))
    kspec = pl.BlockSpec((1, L, LANES), lambda b, p, i: (b, 0, p))
    return pl.pallas_call(
        functools.partial(_fox_kernel, tq=tq),
        grid=(B, n_pairs, L // tq),
        in_specs=[qspec, qspec, kspec, kspec, kspec],
        out_specs=qspec,
        out_shape=jax.ShapeDtypeStruct((B, L, ATT_W), BF16),
        compiler_params=_params(("arbitrary", "arbitrary", "arbitrary")),
        name="fox_attention",
    )(qd, qa, kd, ka, v)


def _ssd_kernel(xbc_ref, z_ref, dt_ref, cw_ref, cb_ref, alog_ref, dsk_ref, nw_ref, tri3_ref,
                y_ref, xext_ref, st_ref):
    Q = CHUNK

    @pl.when(pl.program_id(1) == 0)
    def _():
        xext_ref[0:SUBLANES, :] = jnp.zeros((SUBLANES, SSD_CONV_DIM), F32)
        st_ref[...] = jnp.zeros_like(st_ref)

    xext_ref[SUBLANES:SUBLANES + Q, :] = xbc_ref[0]
    conv = cb_ref[...]
    for k in range(SSD_CONV):
        conv = conv + cw_ref[k:k + 1, :] * xext_ref[pl.ds(SUBLANES - (SSD_CONV - 1) + k, Q), :]
    xext_ref[0:SUBLANES, :] = xext_ref[Q:Q + SUBLANES, :]
    xc = _silu(conv)
    xs = xc[:, :SSD_W]
    bm = xc[:, SSD_W:SSD_W + LANES]
    cm = xc[:, SSD_W + LANES:]

    dt = dt_ref[0]
    a = dt * (-jnp.exp(alog_ref[...]))
    hi, mid, lo = _split3(a)
    a_cum = _dot(tri3_ref[...], jnp.concatenate([hi, mid, lo], axis=0))
    a_row = a_cum.T
    a_last_col = a_cum[Q - 1:Q, :]
    bm_t = bm.T

    rows = lax.broadcasted_iota(jnp.int32, (Q, Q), 0)
    cols = lax.broadcasted_iota(jnp.int32, (Q, Q), 1)
    causal = rows >= cols
    lane = lax.broadcasted_iota(jnp.int32, (Q, LANES), 1)
    first_half = lane < HEAD_DIM

    scores = []
    cmask = []
    for g in range(SSD_GROUPS):
        cg = jnp.where(_head_lane_mask(g, (Q, LANES)), cm, 0.0).astype(BF16)
        cmask.append(cg)
        scores.append(_dot_nt(cg, bm.astype(BF16)))

    y_pairs = []
    heads_per_group = SSD_HEADS // SSD_GROUPS
    for p in range(SSD_HEADS // 2):
        xs_pair = xs[:, p * LANES:(p + 1) * LANES]
        dt_pair = jnp.where(first_half, dt[:, DT_LANE0 + 2 * p:DT_LANE0 + 2 * p + 1],
                            dt[:, DT_LANE0 + 2 * p + 1:DT_LANE0 + 2 * p + 2])
        xdt = (xs_pair * dt_pair).astype(BF16)
        y_head = []
        for e in range(2):
            h = 2 * p + e
            g = h // heads_per_group
            hl = DT_LANE0 + h
            acol = a_cum[:, hl:hl + 1]
            arow = a_row[hl:hl + 1, :]
            alast = a_last_col[:, hl:hl + 1]
            lmat = jnp.exp(jnp.where(causal, acol - arow, -jnp.inf))
            y_diag = _dot((scores[g] * lmat).astype(BF16), xdt)
            prev = st_ref[h]
            y_off = _dot(cmask[g], prev.astype(BF16)) * jnp.exp(acol)
            y_head.append(y_diag + y_off)
            decay_row = jnp.exp(alast - arow)
            local = _dot((bm_t * decay_row).astype(BF16), xdt)
            st_ref[h] = prev * jnp.exp(alast) + local
        y_pairs.append(jnp.where(first_half, y_head[0], y_head[1]))
    y = jnp.concatenate(y_pairs, axis=1) + dsk_ref[...] * xs
    y = y * _silu(z_ref[0])
    gw = SSD_W // SSD_GROUPS
    y = jnp.concatenate([_rms(y[:, g * gw:(g + 1) * gw], nw_ref[:, g * gw:(g + 1) * gw])
                         for g in range(SSD_GROUPS)], axis=1)
    y_ref[0] = y.astype(BF16)


def _ssd(xbc, z, dt, conv_w, conv_b, a_log, d_skip, norm_w):
    B, L, _ = xbc.shape
    Q = CHUNK
    alog = jnp.zeros((1, LANES), F32).at[0, DT_LANE0:DT_LANE0 + SSD_HEADS].set(a_log)
    dsk = jnp.repeat(d_skip, SSD_W // SSD_HEADS).reshape(1, SSD_W)
    tri = _tri_incl(Q)
    tri3 = jnp.asarray(np.concatenate([tri, tri, tri], axis=1), BF16)
    row = lambda width: pl.BlockSpec((1, Q, width), lambda b, c: (b, c, 0))
    const = lambda shape: pl.BlockSpec(shape, lambda b, c: (0,) * len(shape))
    return pl.pallas_call(
        _ssd_kernel,
        grid=(B, L // Q),
        in_specs=[row(SSD_CONV_DIM), row(SSD_W), row(LANES), const((SSD_CONV, SSD_CONV_DIM)),
                  const((1, SSD_CONV_DIM)), const((1, LANES)), const((1, SSD_W)),
                  const((1, SSD_W)), const((Q, 3 * Q))],
        out_specs=row(SSD_W),
        out_shape=jax.ShapeDtypeStruct((B, L, SSD_W), BF16),
        scratch_shapes=[pltpu.VMEM((Q + SUBLANES, SSD_CONV_DIM), F32),
                        pltpu.VMEM((SSD_HEADS, LANES, LANES), F32)],
        compiler_params=_params(("arbitrary", "arbitrary")),
        name="ssd_scan",
    )(xbc, z, dt, conv_w, conv_b.reshape(1, -1), alog, dsk, norm_w.reshape(1, -1), tri3)


def _mix_mlp_kernel(h_ref, y1_ref, y2_ref, wo_ref, nw_ref, wup_ref, wdn_ref, nf_ref, o_ref,
                    *, ff_chunk, final):
    half = y1_ref.shape[-1]
    mix = _dot(y1_ref[...], wo_ref[0:half, :]) + _dot(y2_ref[...], wo_ref[half:2 * half, :])
    h1 = h_ref[...] + mix
    u = _rms(h1, nw_ref[...]).astype(BF16)
    d_ff = wup_ref.shape[1]
    acc = jnp.zeros_like(h1)
    for c in range(d_ff // ff_chunk):
        act = jnp.maximum(_dot(u, wup_ref[:, c * ff_chunk:(c + 1) * ff_chunk]), 0.0)
        acc = acc + _dot((act * act).astype(BF16), wdn_ref[c * ff_chunk:(c + 1) * ff_chunk, :])
    h2 = h1 + acc
    if final:
        h2 = _rms(h2, nf_ref[...])
    o_ref[...] = h2


def _mix_mlp(h, y1, y2, w_out, nw, w_up, w_down, nf, tm, final):
    B, L, D = h.shape
    T = B * L
    d_ff = w_up.shape[1]
    half = y1.shape[-1]
    row = lambda width: pl.BlockSpec((tm, width), lambda i: (i, 0))
    const = lambda shape: pl.BlockSpec(shape, lambda i: (0,) * len(shape))
    out = pl.pallas_call(
        functools.partial(_mix_mlp_kernel, ff_chunk=min(1024, d_ff), final=final),
        grid=(T // tm,),
        in_specs=[row(D), row(half), row(half), const((2 * half, D)), const((1, D)),
                  const((D, d_ff)), const((d_ff, D)), const((1, D))],
        out_specs=row(D),
        out_shape=jax.ShapeDtypeStruct((T, D), F32),
        compiler_params=_params(("arbitrary",)),
        name="mix_mlp",
    )(h.reshape(T, D), y1.reshape(T, half), y2.reshape(T, half), w_out.astype(BF16),
      nw.reshape(1, D), w_up.astype(BF16), w_down.astype(BF16), nf.reshape(1, D))
    return out.reshape(B, L, D)


GLA_QK = GLA_HEADS * GLA_DK
GLA_V = GLA_HEADS * GLA_DV
CD_COLS = 3 * ATT_W + 2 * GLA_QK + 2 * GLA_V + LANES


def _inproj_cd_kernel(x_ref, nw_ref, w_ref, sq_ref, sk_ref, sv_ref, gq_ref, gk_ref, gv_ref,
                      gr_ref, glow_ref):
    u = _rms(x_ref[...], nw_ref[...]).astype(BF16)

    def mm(a, b):
        return _dot(u, w_ref[:, a:b])

    o = 0
    sq_ref[...] = (mm(o, o + ATT_W) * (HEAD_DIM ** -0.5)).astype(BF16); o += ATT_W
    sk_ref[...] = mm(o, o + ATT_W).astype(BF16); o += ATT_W
    sv_ref[...] = mm(o, o + ATT_W).astype(BF16); o += ATT_W
    gq_ref[...] = mm(o, o + GLA_QK) * (GLA_DK ** -0.5); o += GLA_QK
    gk_ref[...] = mm(o, o + GLA_QK); o += GLA_QK
    gv_ref[...] = mm(o, o + GLA_V); o += GLA_V
    gr_ref[...] = mm(o, o + GLA_V); o += GLA_V
    glow_ref[...] = mm(o, o + LANES)


def _inproj_cd(h, nw, w_in, tm):
    B, L, D = h.shape
    T = B * L
    sq, sk, sv, gq, gk, gv, glow, gr = jnp.split(w_in, np.cumsum(
        [ATT_W, ATT_W, ATT_W, GLA_QK, GLA_QK, GLA_V, GLA_RANK]).tolist(), axis=1)
    pad = jnp.zeros((D, LANES - GLA_RANK), w_in.dtype)
    w = jnp.concatenate([sq, sk, sv, gq, gk, gv, gr, glow, pad], axis=1).astype(BF16)
    row = lambda width: pl.BlockSpec((tm, width), lambda i: (i, 0))
    const = lambda shape: pl.BlockSpec(shape, lambda i: (0,) * len(shape))
    widths = [ATT_W, ATT_W, ATT_W, GLA_QK, GLA_QK, GLA_V, GLA_V, LANES]
    dtypes = [BF16, BF16, BF16, F32, F32, F32, F32, F32]
    outs = pl.pallas_call(
        _inproj_cd_kernel,
        grid=(T // tm,),
        in_specs=[row(D), const((1, D)), const((D, CD_COLS))],
        out_specs=[row(wd) for wd in widths],
        out_shape=[jax.ShapeDtypeStruct((T, wd), dt) for wd, dt in zip(widths, dtypes)],
        compiler_params=_params(("arbitrary",)),
        name="inproj_cd",
    )(h.reshape(T, D), nw.reshape(1, D), w)
    return [o.reshape(B, L, -1) for o in outs]


SB_DEAD = -104.0


def _sb_kernel(q_ref, k_ref, v_ref, u2_ref, o_ref, *, tq):
    i = pl.program_id(2)
    qf = q_ref[0].astype(F32)
    q = jnp.concatenate([jnp.where(_head_lane_mask(e, (tq, LANES)), qf, 0.0) for e in range(2)],
                        axis=0).astype(BF16)
    rows = lax.broadcasted_iota(jnp.int32, (2 * tq, tq), 0)
    cols = lax.broadcasted_iota(jnp.int32, (2 * tq, tq), 1)
    strict = cols < (rows & (tq - 1))

    def step(j, c, acc, masked):
        ks = pl.multiple_of(j * tq, tq)
        z = _dot_nt(q, k_ref[0, pl.ds(ks, tq), :])
        lr = -_softplus(z)
        if masked:
            lr = jnp.where(strict, lr, 0.0)
        hi, lo = _split2(lr)
        incl = _dot(jnp.concatenate([hi, lo], axis=1), u2_ref[...])
        a = jnp.exp(z + c + incl)
        if masked:
            a = jnp.where(strict, a, 0.0)
        acc = acc + _dot(a.astype(BF16), v_ref[0, pl.ds(ks, tq), :])
        c = c + jnp.sum(lr, axis=1, keepdims=True)
        return c, acc

    c, acc = step(i, jnp.zeros((2 * tq, 1), F32), jnp.zeros((2 * tq, LANES), F32), True)

    def cond(carry):
        j, cmax, _, _ = carry
        return (j >= 0) & (cmax > SB_DEAD)

    def body(carry):
        j, _, c, acc = carry
        c, acc = step(j, c, acc, False)
        return j - 1, jnp.max(c), c, acc

    _, _, _, acc = lax.while_loop(cond, body, (i - 1, jnp.max(c), c, acc))
    o_ref[0] = jnp.where(_head_lane_mask(0, (tq, LANES)), acc[:tq], acc[tq:]).astype(BF16)


def _sb_attention(q, k, v, tq):
    B, L, _ = q.shape
    n_pairs = N_ATT_HEADS // 2
    r = np.arange(tq)
    u = (r[:, None] >= r[None, :]).astype(np.float32)
    u2 = jnp.asarray(np.concatenate([u, u], axis=0), BF16)
    qspec = pl.BlockSpec((1, tq, LANES), lambda b, p, i: (b, i, p))
    kspec = pl.BlockSpec((1, L, LANES), lambda b, p, i: (b, 0, p))
    return pl.pallas_call(
        functools.partial(_sb_kernel, tq=tq),
        grid=(B, n_pairs, L // tq),
        in_specs=[qspec, kspec, kspec, pl.BlockSpec((2 * tq, tq), lambda b, p, i: (0, 0))],
        out_specs=qspec,
        out_shape=jax.ShapeDtypeStruct((B, L, ATT_W), BF16),
        compiler_params=_params(("arbitrary", "arbitrary", "arbitrary")),
        name="sb_attention",
    )(q, k, v, u2)


def _gla_tables():
    Q = CHUNK
    r = np.arange(Q)
    j = np.arange(Q)
    coef = [(j[None, :] <= r[:, None]), (j[None, :] > r[:, None])]
    masks = [np.eye(Q, dtype=bool)]
    for lvl in range(GLA_LEVELS):
        m = 1 << lvl
        c0 = (r // (2 * m)) * (2 * m)
        mid = c0 + m - 1
        upper = (r - c0) >= m
        up = (j[None, :] > mid[:, None]) & (j[None, :] <= r[:, None])
        lowr = (j[None, :] > r[:, None]) & (j[None, :] <= mid[:, None])
        coef.append(np.where(upper[:, None], up, lowr))
        masks.append((c0[:, None] == c0[None, :]) & upper[:, None] & (~upper)[None, :])
    coef = np.concatenate(coef, axis=0).astype(np.float32)
    coef2 = np.concatenate([coef, coef], axis=1)
    masks = np.stack(masks).astype(np.float32)
    hv = np.arange(GLA_V) // GLA_DV
    hk = np.arange(GLA_QK) // GLA_DK
    bdiag = (hv[:, None] == hk[None, :]).astype(np.float32)
    return jnp.asarray(coef2, BF16), jnp.asarray(masks, F32), jnp.asarray(bdiag, F32)


def _gla_kernel(gq_ref, gk_ref, gv_ref, glow_ref, gr_ref, w2_ref, gb_ref, coef_ref, mask_ref,
                bdiag_ref, nw_ref, o_ref, st_ref):
    Q = CHUNK

    @pl.when(pl.program_id(1) == 0)
    def _():
        st_ref[...] = jnp.zeros_like(st_ref)

    logits = _dot(glow_ref[0].astype(BF16), w2_ref[...]) + gb_ref[...]
    la = _log_sigmoid(logits) * (1.0 / GLA_GATE_NORM)
    hi, lo = _split2(la)
    expo = _dot(coef_ref[...], jnp.concatenate([hi, lo], axis=0))
    q = gq_ref[0]
    k = gk_ref[0]
    v = gv_ref[0]
    lane_head = lax.broadcasted_iota(jnp.int32, (Q, GLA_QK), 1) // GLA_DK
    row = lax.broadcasted_iota(jnp.int32, (Q, GLA_QK), 0)
    hmask = [lane_head == h for h in range(GLA_HEADS)]

    att = [None] * GLA_HEADS
    for lvl in range(-1, GLA_LEVELS):
        if lvl < 0:
            xq, xk = q, k.astype(BF16)
        else:
            m = 1 << lvl
            upper = (row & (2 * m - 1)) >= m
            xq = jnp.where(upper, q, k) * jnp.exp(expo[(2 + lvl) * Q:(3 + lvl) * Q, :])
            xk = xq.astype(BF16)
        msk = mask_ref[lvl + 1]
        for h in range(GLA_HEADS):
            part = _dot_nt(jnp.where(hmask[h], xq, 0.0).astype(BF16), xk) * msk
            att[h] = part if att[h] is None else att[h] + part

    st = st_ref[...]
    q_in = (q * jnp.exp(expo[0:Q, :])).astype(BF16)
    o = _dot_nt(q_in, st.astype(BF16))
    o_intra = [_dot(att[h].astype(BF16), v[:, h * GLA_DV:(h + 1) * GLA_DV].astype(BF16))
               for h in range(GLA_HEADS)]
    o = o + jnp.concatenate(o_intra, axis=1)

    k_dec = (k * jnp.exp(expo[Q:2 * Q, :])).astype(BF16)
    upd = _dot(v.T.astype(BF16), k_dec)
    g_last = expo[Q - 1:Q, :]
    st_ref[...] = st * jnp.exp(g_last) + upd * bdiag_ref[...]

    gr = gr_ref[0]
    o = jnp.concatenate([_rms(o[:, h * GLA_DV:(h + 1) * GLA_DV], nw_ref[...])
                         for h in range(GLA_HEADS)], axis=1)
    o_ref[0] = (o * _silu(gr)).astype(BF16)


def _gla(gq, gk, gv, glow, gr, gate_w2, gate_b, norm_w):
    B, L, _ = gq.shape
    Q = CHUNK
    coef2, masks, bdiag = _gla_tables()
    w2 = jnp.zeros((LANES, GLA_QK), F32).at[:GLA_RANK].set(gate_w2).astype(BF16)
    row = lambda width: pl.BlockSpec((1, Q, width), lambda b, c: (b, c, 0))
    const = lambda shape: pl.BlockSpec(shape, lambda b, c: (0,) * len(shape))
    return pl.pallas_call(
        _gla_kernel,
        grid=(B, L // Q),
        in_specs=[row(GLA_QK), row(GLA_QK), row(GLA_V), row(LANES), row(GLA_V),
                  const((LANES, GLA_QK)), const((1, GLA_QK)), const(coef2.shape),
                  const(masks.shape), const(bdiag.shape), const((1, GLA_DV))],
        out_specs=row(GLA_V),
        out_shape=jax.ShapeDtypeStruct((B, L, GLA_V), BF16),
        scratch_shapes=[pltpu.VMEM((GLA_V, GLA_QK), F32)],
        compiler_params=_params(("arbitrary", "arbitrary")),
        name="gla_scan",
    )(gq, gk, gv, glow, gr, w2, gate_b.reshape(1, -1), coef2, masks, bdiag,
      norm_w.reshape(1, -1))


def _block(n, want):
    return want if n % want == 0 else n


def kernel(x, norm_mix, norm_mlp, norm_final, w_in_ab, fox_f_bias, ssd_conv_w, ssd_conv_b,
           ssd_dt_bias, ssd_a_log, ssd_d, ssd_norm, w_out_ab, w_in_cd, gla_gate_w2,
           gla_gate_b, gla_norm, w_out_cd, w_mlp_up, w_mlp_down):
    B, L, D = x.shape
    assert L % CHUNK == 0
    tm = _block(L, 512)
    tq = _block(L, 256)
    tq_fox = _block(L, 512)

    qd, kd, vd, qa, ka, z, xbc, dt = _inproj_ab(x, norm_mix[0], w_in_ab[0], fox_f_bias[0],
                                                ssd_dt_bias[0], tm)
    y_fox = _fox_attention(qd, qa, kd, ka, vd, tq_fox)
    y_ssd = _ssd(xbc, z, dt, ssd_conv_w[0], ssd_conv_b[0], ssd_a_log[0], ssd_d[0], ssd_norm[0])
    h = _mix_mlp(x, y_fox, y_ssd, w_out_ab[0], norm_mlp[0], w_mlp_up[0], w_mlp_down[0],
                 norm_final, tm, final=False)

    sq, sk, sv, gq, gk, gv, gr, glow = _inproj_cd(h, norm_mix[1], w_in_cd[0], tm)
    y_sb = _sb_attention(sq, sk, sv, tq)
    y_gla = _gla(gq, gk, gv, glow, gr, gla_gate_w2[0], gla_gate_b[0], gla_norm[0])
    return _mix_mlp(h, y_sb, y_gla, w_out_cd[0], norm_mlp[1], w_mlp_up[1], w_mlp_down[1],
                    norm_final, tm, final=True)
```

```python
import functools
import math

import numpy as np
import jax
import jax.numpy as jnp
from jax import lax
from jax.experimental import pallas as pl
from jax.experimental.pallas import tpu as pltpu

F32 = jnp.float32
BF16 = jnp.bfloat16

LANES = 128
SUBLANES = 8
VMEM_LIMIT_BYTES = 56 * 1024 * 1024

HEAD_DIM = 64
N_ATT_HEADS = 8
ATT_W = N_ATT_HEADS * HEAD_DIM
SSD_HEADS = 8
SSD_W = 512
SSD_GROUPS = 2
SSD_STATE = 64
SSD_CONV = 4
SSD_CONV_DIM = SSD_W + 2 * SSD_GROUPS * SSD_STATE
GLA_HEADS = 4
GLA_DK = 64
GLA_DV = 128
GLA_RANK = 16
GLA_GATE_NORM = 16.0
EPS = 1e-5
CHUNK = 128
GLA_LEVELS = 7
LOG2E = math.log2(math.e)
DT_LANE0 = 8


def _params(sem):
    return pltpu.CompilerParams(dimension_semantics=sem, vmem_limit_bytes=VMEM_LIMIT_BYTES)


def _split3(x):
    hi = x.astype(BF16)
    r = x - hi.astype(F32)
    mid = r.astype(BF16)
    lo = (r - mid.astype(F32)).astype(BF16)
    return hi, mid, lo


def _split2(x):
    hi = x.astype(BF16)
    lo = (x - hi.astype(F32)).astype(BF16)
    return hi, lo


def _softplus(x):
    return jnp.maximum(x, 0.0) + jnp.log1p(jnp.exp(-jnp.abs(x)))


def _log_sigmoid(x):
    return jnp.minimum(x, 0.0) - jnp.log1p(jnp.exp(-jnp.abs(x)))


def _silu(x):
    return x / (1.0 + jnp.exp(-x))


def _rms(x, w):
    ms = jnp.mean(x * x, axis=-1, keepdims=True)
    return x * lax.rsqrt(ms + EPS) * w


def _dot(a, b):
    return jnp.dot(a, b, preferred_element_type=F32)


def _dot_nt(a, b):
    return lax.dot_general(a, b, (((1,), (1,)), ((), ())), preferred_element_type=F32)


AB_COLS = 3 * ATT_W + SSD_W + SSD_CONV_DIM + LANES


def _inproj_ab_kernel(x_ref, nw_ref, w_ref, sb_ref, tri3_ref, sel_ref, cst_ref,
                      qd_ref, kd_ref, vd_ref, qa_ref, ka_ref, z_ref, xbc_ref, dt_ref,
                      carry_ref):
    @pl.when(pl.program_id(1) == 0)
    def _():
        carry_ref[...] = jnp.zeros_like(carry_ref)

    u = _rms(x_ref[0], nw_ref[...]).astype(BF16)

    def mm(a, b):
        return _dot(u, w_ref[:, a:b])

    o = 0
    qd_ref[0] = (mm(o, o + ATT_W) * (LOG2E * HEAD_DIM ** -0.5)).astype(BF16); o += ATT_W
    kd_ref[0] = mm(o, o + ATT_W).astype(BF16); o += ATT_W
    vd_ref[0] = mm(o, o + ATT_W).astype(BF16); o += ATT_W
    z_ref[0] = mm(o, o + SSD_W); o += SSD_W
    xbc_ref[0] = mm(o, o + SSD_CONV_DIM); o += SSD_CONV_DIM
    small = mm(o, o + LANES) + sb_ref[...]

    lane = lax.broadcasted_iota(jnp.int32, small.shape, 1)
    is_f = lane < N_ATT_HEADS
    is_dt = (lane >= DT_LANE0) & (lane < DT_LANE0 + SSD_HEADS)
    log_f = jnp.where(is_f, _log_sigmoid(small), 0.0)
    dt_ref[0] = jnp.where(is_dt, _softplus(small), 0.0)

    hi, mid, lo = _split3(log_f)
    cum = _dot(tri3_ref[...], jnp.concatenate([hi, mid, lo], axis=0)) + carry_ref[...]
    tm = cum.shape[0]
    carry_ref[...] = cum[tm - 1:tm, :]

    fh, fm, fl = _split3(cum * LOG2E)
    fcat = (fh.astype(F32) + pltpu.roll(fm.astype(F32), N_ATT_HEADS, 1)
            + pltpu.roll(fl.astype(F32), 2 * N_ATT_HEADS, 1)).astype(BF16)
    aug = _dot(fcat, sel_ref[...]) + cst_ref[...]
    qa_ref[0] = aug[:, :ATT_W].astype(BF16)
    ka_ref[0] = aug[:, ATT_W:].astype(BF16)


def _aug_tables():
    sel = np.zeros((LANES, 2 * ATT_W), np.float32)
    cst = np.zeros((1, 2 * ATT_W), np.float32)
    for h in range(N_ATT_HEADS):
        p, e = divmod(h, 2)
        for part in range(3):
            src = part * N_ATT_HEADS + h
            sel[src, p * LANES + 6 * e + part] = 1.0
            sel[src, ATT_W + p * LANES + 6 * e + 3 + part] = -1.0
            cst[0, p * LANES + 6 * e + 3 + part] = 1.0
            cst[0, ATT_W + p * LANES + 6 * e + part] = 1.0
    return jnp.asarray(sel, BF16), jnp.asarray(cst, F32)


def _tri_incl(n):
    r = np.arange(n)
    return (r[None, :] <= r[:, None]).astype(np.float32)


def _inproj_ab(h, nw, w_in, f_bias, dt_bias, tm):
    B, L, D = h.shape
    fq, fk, fv, fl, wz, wxbc, wdt = jnp.split(w_in, np.cumsum(
        [ATT_W, ATT_W, ATT_W, N_ATT_HEADS, SSD_W, SSD_CONV_DIM])[:].tolist(), axis=1)
    pad = jnp.zeros((D, LANES - N_ATT_HEADS - SSD_HEADS), w_in.dtype)
    w = jnp.concatenate([fq, fk, fv, wz, wxbc, fl, wdt, pad], axis=1).astype(BF16)
    sb = jnp.concatenate([f_bias, dt_bias, jnp.zeros((LANES - 16,), F32)]).reshape(1, LANES)
    tri = _tri_incl(tm)
    tri3 = jnp.asarray(np.concatenate([tri, tri, tri], axis=1), BF16)
    sel, cst = _aug_tables()

    row = lambda width: pl.BlockSpec((1, tm, width), lambda b, i: (b, i, 0))
    const = lambda shape: pl.BlockSpec(shape, lambda b, i: (0,) * len(shape))
    outs = [
        jax.ShapeDtypeStruct((B, L, ATT_W), BF16),
        jax.ShapeDtypeStruct((B, L, ATT_W), BF16),
        jax.ShapeDtypeStruct((B, L, ATT_W), BF16),
        jax.ShapeDtypeStruct((B, L, ATT_W), BF16),
        jax.ShapeDtypeStruct((B, L, ATT_W), BF16),
        jax.ShapeDtypeStruct((B, L, SSD_W), F32),
        jax.ShapeDtypeStruct((B, L, SSD_CONV_DIM), F32),
        jax.ShapeDtypeStruct((B, L, LANES), F32),
    ]
    return pl.pallas_call(
        _inproj_ab_kernel,
        grid=(B, L // tm),
        in_specs=[row(D), const((1, D)), const((D, AB_COLS)), const((1, LANES)),
                  const((tm, 3 * tm)), const((LANES, 2 * ATT_W)), const((1, 2 * ATT_W))],
        out_specs=[row(ATT_W)] * 5 + [row(SSD_W), row(SSD_CONV_DIM), row(LANES)],
        out_shape=outs,
        scratch_shapes=[pltpu.VMEM((1, LANES), F32)],
        compiler_params=_params(("arbitrary", "arbitrary")),
        name="inproj_ab",
    )(h, nw.reshape(1, D), w, sb, tri3, sel, cst)


def _head_lane_mask(e, shape):
    lane = lax.broadcasted_iota(jnp.int32, shape, 1)
    return (lane < HEAD_DIM) if e == 0 else (lane >= HEAD_DIM)


FOX_STRIP = 64


def _fox_kernel(qd_ref, qa_ref, kd_ref, ka_ref, v_ref, o_ref,
                q_ref, s0_ref, s1_ref, p0_ref, p1_ref, m_ref, al_ref, acc_ref, *, tq):
    i = pl.program_id(2)
    tk = tq
    R = 2 * tq
    qd = qd_ref[0].astype(F32)
    qa = qa_ref[0].astype(F32)
    lane = lax.broadcasted_iota(jnp.int32, (tq, LANES), 1)
    for e in range(2):
        amask = (lane >= 6 * e) & (lane < 6 * e + 6)
        q_ref[e * tq:(e + 1) * tq, :] = jnp.concatenate(
            [jnp.where(_head_lane_mask(e, (tq, LANES)), qd, 0.0), jnp.where(amask, qa, 0.0)],
            axis=1).astype(BF16)
    m_ref[...] = jnp.full(m_ref.shape, -jnp.inf, F32)
    acc_ref[...] = jnp.zeros(acc_ref.shape, F32)
    ones = jnp.ones((tk, LANES), BF16)

    def scores(j, s_ref):
        ks = pl.multiple_of(j * tk, tk)
        k = jnp.concatenate([kd_ref[0, pl.ds(ks, tk), :], ka_ref[0, pl.ds(ks, tk), :]], axis=1)
        s_ref[...] = _dot_nt(q_ref[...], k)

    def update(j, s_ref, p_ref, masked):
        for r0 in range(0, R, FOX_STRIP):
            rs = slice(r0, r0 + FOX_STRIP)
            s = s_ref[rs, :]
            if masked:
                rows = lax.broadcasted_iota(jnp.int32, s.shape, 0) + (r0 % tq)
                cols = lax.broadcasted_iota(jnp.int32, s.shape, 1)
                s = jnp.where(rows >= cols, s, -jnp.inf)
            m_old = m_ref[rs, :]
            m_new = jnp.maximum(m_old, jnp.max(s, axis=1, keepdims=True))
            al_ref[rs, :] = jnp.exp2(m_old - m_new)
            m_ref[rs, :] = m_new
            p_ref[rs, :] = jnp.exp2(s - m_new).astype(BF16)
        ks = pl.multiple_of(j * tk, tk)
        v1 = jnp.concatenate([v_ref[0, pl.ds(ks, tk), :], ones], axis=1)
        acc_ref[...] = al_ref[...] * acc_ref[...] + _dot(p_ref[...], v1)

    scores(0, s0_ref)

    def pair(u, carry):
        scores(2 * u + 1, s1_ref)
        update(2 * u, s0_ref, p0_ref, False)
        scores(2 * u + 2, s0_ref)
        update(2 * u + 1, s1_ref, p1_ref, False)
        return carry

    lax.fori_loop(0, i // 2, pair, 0)

    @pl.when(i % 2 == 0)
    def _():
        update(i, s0_ref, p0_ref, True)

    @pl.when(i % 2 == 1)
    def _():
        scores(i, s1_ref)
        update(i - 1, s0_ref, p0_ref, False)
        update(i, s1_ref, p1_ref, True)

    acc = acc_ref[...]
    out = acc[:, :LANES] / acc[:, LANES:]
    o_ref[0] = jnp.where(_head_lane_mask(0, (tq, LANES)), out[:tq], out[tq:]).astype(BF16)


def _fox_attention(qd, qa, kd, ka, v, tq):
    B, L, _ = qd.shape
    n_pairs = N_ATT_HEADS // 2
    R = 2 * tq
    qspec = pl.BlockSpec((1, tq, LANES), lambda b, p, i: (b, i, p))
    kspec = pl.BlockSpec((1, L, LANES), lambda b, p, i: (b, 0, p))
    return pl.pallas_call(
        functools.partial(_fox_kernel, tq=tq),
        grid=(B, n_pairs, L // tq),
        in_specs=[qspec, qspec, kspec, kspec, kspec],
        out_specs=qspec,
        out_shape=jax.ShapeDtypeStruct((B, L, ATT_W), BF16),
        scratch_shapes=[pltpu.VMEM((R, 2 * LANES), BF16),
                        pltpu.VMEM((R, tq), F32), pltpu.VMEM((R, tq), F32),
                        pltpu.VMEM((R, tq), BF16), pltpu.VMEM((R, tq), BF16),
                        pltpu.VMEM((R, 1), F32), pltpu.VMEM((R, 1), F32),
                        pltpu.VMEM((R, 2 * LANES), F32)],
        compiler_params=_params(("arbitrary", "arbitrary", "arbitrary")),
        name="fox_attention",
    )(qd, qa, kd, ka, v)


def _ssd_kernel(xbc_ref, z_ref, dt_ref, cw_ref, cb_ref, alog_ref, dsk_ref, nw_ref, tri3_ref,
                y_ref, xext_ref, st_ref):
    @pl.when(pl.program_id(0) == 0)
    def _():
        xext_ref[:, 0:SUBLANES, :] = jnp.zeros((xext_ref.shape[0], SUBLANES, SSD_CONV_DIM), F32)
        st_ref[...] = jnp.zeros_like(st_ref)

    for b in range(xbc_ref.shape[0]):
        _ssd_chunk(xbc_ref.at[b], z_ref.at[b], dt_ref.at[b], cw_ref, cb_ref, alog_ref, dsk_ref,
                   nw_ref, tri3_ref, y_ref.at[b], xext_ref.at[b], st_ref.at[b])


def _ssd_chunk(xbc_ref, z_ref, dt_ref, cw_ref, cb_ref, alog_ref, dsk_ref, nw_ref, tri3_ref,
               y_ref, xext_ref, st_ref):
    Q = CHUNK
    xext_ref[SUBLANES:SUBLANES + Q, :] = xbc_ref[...]
    conv = cb_ref[...]
    for k in range(SSD_CONV):
        conv = conv + cw_ref[k:k + 1, :] * xext_ref[pl.ds(SUBLANES - (SSD_CONV - 1) + k, Q), :]
    xext_ref[0:SUBLANES, :] = xext_ref[Q:Q + SUBLANES, :]
    xc = _silu(conv)
    xs = xc[:, :SSD_W]
    bm = xc[:, SSD_W:SSD_W + LANES]
    cm = xc[:, SSD_W + LANES:]

    dt = dt_ref[...]
    a = dt * (-jnp.exp(alog_ref[...]))
    hi, mid, lo = _split3(a)
    a_cum = _dot(tri3_ref[...], jnp.concatenate([hi, mid, lo], axis=0))
    a_row = a_cum.T
    a_last_col = a_cum[Q - 1:Q, :]
    bm_t = bm.T

    rows = lax.broadcasted_iota(jnp.int32, (Q, Q), 0)
    cols = lax.broadcasted_iota(jnp.int32, (Q, Q), 1)
    causal = rows >= cols
    lane = lax.broadcasted_iota(jnp.int32, (Q, LANES), 1)
    first_half = lane < HEAD_DIM

    scores = []
    cmask = []
    for g in range(SSD_GROUPS):
        cg = jnp.where(_head_lane_mask(g, (Q, LANES)), cm, 0.0).astype(BF16)
        cmask.append(cg)
        scores.append(_dot_nt(cg, bm.astype(BF16)))

    y_pairs = []
    heads_per_group = SSD_HEADS // SSD_GROUPS
    for p in range(SSD_HEADS // 2):
        xs_pair = xs[:, p * LANES:(p + 1) * LANES]
        dt_pair = jnp.where(first_half, dt[:, DT_LANE0 + 2 * p:DT_LANE0 + 2 * p + 1],
                            dt[:, DT_LANE0 + 2 * p + 1:DT_LANE0 + 2 * p + 2])
        xdt = (xs_pair * dt_pair).astype(BF16)
        y_head = []
        for e in range(2):
            h = 2 * p + e
            g = h // heads_per_group
            hl = DT_LANE0 + h
            acol = a_cum[:, hl:hl + 1]
            arow = a_row[hl:hl + 1, :]
            alast = a_last_col[:, hl:hl + 1]
            lmat = jnp.exp(jnp.where(causal, acol - arow, -jnp.inf))
            y_diag = _dot((scores[g] * lmat).astype(BF16), xdt)
            prev = st_ref[h]
            y_off = _dot(cmask[g], prev.astype(BF16)) * jnp.exp(acol)
            y_head.append(y_diag + y_off)
            decay_row = jnp.exp(alast - arow)
            local = _dot((bm_t * decay_row).astype(BF16), xdt)
            st_ref[h] = prev * jnp.exp(alast) + local
        y_pairs.append(jnp.where(first_half, y_head[0], y_head[1]))
    y = jnp.concatenate(y_pairs, axis=1) + dsk_ref[...] * xs
    y = y * _silu(z_ref[...])
    gw = SSD_W // SSD_GROUPS
    y = jnp.concatenate([_rms(y[:, g * gw:(g + 1) * gw], nw_ref[:, g * gw:(g + 1) * gw])
                         for g in range(SSD_GROUPS)], axis=1)
    y_ref[...] = y.astype(BF16)


def _ssd(xbc, z, dt, conv_w, conv_b, a_log, d_skip, norm_w):
    B, L, _ = xbc.shape
    Q = CHUNK
    alog = jnp.zeros((1, LANES), F32).at[0, DT_LANE0:DT_LANE0 + SSD_HEADS].set(a_log)
    dsk = jnp.repeat(d_skip, SSD_W // SSD_HEADS).reshape(1, SSD_W)
    tri = _tri_incl(Q)
    tri3 = jnp.asarray(np.concatenate([tri, tri, tri], axis=1), BF16)
    row = lambda width: pl.BlockSpec((B, Q, width), lambda c: (0, c, 0))
    const = lambda shape: pl.BlockSpec(shape, lambda c: (0,) * len(shape))
    return pl.pallas_call(
        _ssd_kernel,
        grid=(L // Q,),
        in_specs=[row(SSD_CONV_DIM), row(SSD_W), row(LANES), const((SSD_CONV, SSD_CONV_DIM)),
                  const((1, SSD_CONV_DIM)), const((1, LANES)), const((1, SSD_W)),
                  const((1, SSD_W)), const((Q, 3 * Q))],
        out_specs=row(SSD_W),
        out_shape=jax.ShapeDtypeStruct((B, L, SSD_W), BF16),
        scratch_shapes=[pltpu.VMEM((B, Q + SUBLANES, SSD_CONV_DIM), F32),
                        pltpu.VMEM((B, SSD_HEADS, LANES, LANES), F32)],
        compiler_params=_params(("arbitrary",)),
        name="ssd_scan",
    )(xbc, z, dt, conv_w, conv_b.reshape(1, -1), alog, dsk, norm_w.reshape(1, -1), tri3)


def _mix_mlp_kernel(h_ref, y1_ref, y2_ref, wo_ref, nw_ref, wup_ref, wdn_ref, nf_ref, o_ref,
                    *, ff_chunk, final):
    half = y1_ref.shape[-1]
    mix = _dot(y1_ref[...], wo_ref[0:half, :]) + _dot(y2_ref[...], wo_ref[half:2 * half, :])
    h1 = h_ref[...] + mix
    u = _rms(h1, nw_ref[...]).astype(BF16)
    d_ff = wup_ref.shape[1]
    acc = jnp.zeros_like(h1)
    for c in range(d_ff // ff_chunk):
        act = jnp.maximum(_dot(u, wup_ref[:, c * ff_chunk:(c + 1) * ff_chunk]), 0.0)
        acc = acc + _dot((act * act).astype(BF16), wdn_ref[c * ff_chunk:(c + 1) * ff_chunk, :])
    h2 = h1 + acc
    if final:
        h2 = _rms(h2, nf_ref[...])
    o_ref[...] = h2


def _mix_mlp(h, y1, y2, w_out, nw, w_up, w_down, nf, tm, final):
    B, L, D = h.shape
    T = B * L
    d_ff = w_up.shape[1]
    half = y1.shape[-1]
    row = lambda width: pl.BlockSpec((tm, width), lambda i: (i, 0))
    const = lambda shape: pl.BlockSpec(shape, lambda i: (0,) * len(shape))
    out = pl.pallas_call(
        functools.partial(_mix_mlp_kernel, ff_chunk=min(1024, d_ff), final=final),
        grid=(T // tm,),
        in_specs=[row(D), row(half), row(half), const((2 * half, D)), const((1, D)),
                  const((D, d_ff)), const((d_ff, D)), const((1, D))],
        out_specs=row(D),
        out_shape=jax.ShapeDtypeStruct((T, D), F32),
        compiler_params=_params(("arbitrary",)),
        name="mix_mlp",
    )(h.reshape(T, D), y1.reshape(T, half), y2.reshape(T, half), w_out.astype(BF16),
      nw.reshape(1, D), w_up.astype(BF16), w_down.astype(BF16), nf.reshape(1, D))
    return out.reshape(B, L, D)


GLA_QK = GLA_HEADS * GLA_DK
GLA_V = GLA_HEADS * GLA_DV
CD_COLS = 3 * ATT_W + 2 * GLA_QK + 2 * GLA_V + LANES


def _inproj_cd_kernel(x_ref, nw_ref, w_ref, sq_ref, sk_ref, sv_ref, gq_ref, gk_ref, gv_ref,
                      gr_ref, glow_ref):
    u = _rms(x_ref[...], nw_ref[...]).astype(BF16)

    def mm(a, b):
        return _dot(u, w_ref[:, a:b])

    o = 0
    sq_ref[...] = (mm(o, o + ATT_W) * (LOG2E * HEAD_DIM ** -0.5)).astype(BF16); o += ATT_W
    sk_ref[...] = mm(o, o + ATT_W).astype(BF16); o += ATT_W
    sv_ref[...] = mm(o, o + ATT_W).astype(BF16); o += ATT_W
    gq_ref[...] = mm(o, o + GLA_QK) * (GLA_DK ** -0.5); o += GLA_QK
    gk_ref[...] = mm(o, o + GLA_QK); o += GLA_QK
    gv_ref[...] = mm(o, o + GLA_V); o += GLA_V
    gr_ref[...] = mm(o, o + GLA_V); o += GLA_V
    glow_ref[...] = mm(o, o + LANES)


def _inproj_cd(h, nw, w_in, tm):
    B, L, D = h.shape
    T = B * L
    sq, sk, sv, gq, gk, gv, glow, gr = jnp.split(w_in, np.cumsum(
        [ATT_W, ATT_W, ATT_W, GLA_QK, GLA_QK, GLA_V, GLA_RANK]).tolist(), axis=1)
    pad = jnp.zeros((D, LANES - GLA_RANK), w_in.dtype)
    w = jnp.concatenate([sq, sk, sv, gq, gk, gv, gr, glow, pad], axis=1).astype(BF16)
    row = lambda width: pl.BlockSpec((tm, width), lambda i: (i, 0))
    const = lambda shape: pl.BlockSpec(shape, lambda i: (0,) * len(shape))
    widths = [ATT_W, ATT_W, ATT_W, GLA_QK, GLA_QK, GLA_V, GLA_V, LANES]
    dtypes = [BF16, BF16, BF16, F32, F32, F32, F32, F32]
    outs = pl.pallas_call(
        _inproj_cd_kernel,
        grid=(T // tm,),
        in_specs=[row(D), const((1, D)), const((D, CD_COLS))],
        out_specs=[row(wd) for wd in widths],
        out_shape=[jax.ShapeDtypeStruct((T, wd), dt) for wd, dt in zip(widths, dtypes)],
        compiler_params=_params(("arbitrary",)),
        name="inproj_cd",
    )(h.reshape(T, D), nw.reshape(1, D), w)
    return [o.reshape(B, L, -1) for o in outs]


SB_DEAD = -151.0
SB_STRIP = 64
SB_CHAINS = 2


def _sb_kernel(q_ref, k_ref, v_ref, u2_ref, o_ref,
               qs_ref, z_ref, lr_ref, in_ref, a_ref, c_ref, cn_ref, acc_ref, *, tq):
    i = pl.program_id(2)
    tk = tq
    R = 2 * tq
    chains = range(SB_CHAINS)
    for ch in chains:
        qf = q_ref[0, :, ch * LANES:(ch + 1) * LANES].astype(F32)
        for e in range(2):
            qs_ref[ch, e * tq:(e + 1) * tq, :] = jnp.where(_head_lane_mask(e, (tq, LANES)), qf,
                                                           0.0).astype(BF16)
    c_ref[...] = jnp.zeros(c_ref.shape, F32)
    acc_ref[...] = jnp.zeros(acc_ref.shape, F32)

    def strict_mask(r0, shape):
        rows = lax.broadcasted_iota(jnp.int32, shape, 0) + (r0 % tq)
        return lax.broadcasted_iota(jnp.int32, shape, 1) < rows

    def step(j, masked):
        ks = pl.multiple_of(j * tk, tk)
        for ch in chains:
            z_ref[ch] = _dot_nt(qs_ref[ch], k_ref[0, pl.ds(ks, tk), ch * LANES:(ch + 1) * LANES])
        for ch in chains:
            for r0 in range(0, R, SB_STRIP):
                rs = slice(r0, r0 + SB_STRIP)
                z = z_ref[ch, rs, :]
                nz = -z
                lr = jnp.minimum(nz, 0.0) - jnp.log2(1.0 + jnp.exp2(jnp.minimum(z, nz)))
                if masked:
                    lr = jnp.where(strict_mask(r0, lr.shape), lr, 0.0)
                hi, lo = _split2(lr)
                lr_ref[ch, rs, :tk] = hi
                lr_ref[ch, rs, tk:] = lo
                cn_ref[ch, rs, :] = c_ref[ch, rs, :] + jnp.sum(lr, axis=1, keepdims=True)
        for ch in chains:
            in_ref[ch] = _dot(lr_ref[ch], u2_ref[...])
        for ch in chains:
            for r0 in range(0, R, SB_STRIP):
                rs = slice(r0, r0 + SB_STRIP)
                a = jnp.exp2(z_ref[ch, rs, :] + c_ref[ch, rs, :] + in_ref[ch, rs, :])
                if masked:
                    a = jnp.where(strict_mask(r0, a.shape), a, 0.0)
                a_ref[ch, rs, :] = a.astype(BF16)
        for ch in chains:
            acc_ref[ch] += _dot(a_ref[ch], v_ref[0, pl.ds(ks, tk), ch * LANES:(ch + 1) * LANES])
        cn = cn_ref[...]
        c_ref[...] = cn
        return jnp.max(cn)

    cmax = step(i, True)

    def cond(carry):
        j, cmax = carry
        return (j >= 0) & (cmax > SB_DEAD)

    def body(carry):
        j, _ = carry
        return j - 1, step(j, False)

    lax.while_loop(cond, body, (i - 1, cmax))
    for ch in chains:
        acc = acc_ref[ch]
        o_ref[0, :, ch * LANES:(ch + 1) * LANES] = jnp.where(
            _head_lane_mask(0, (tq, LANES)), acc[:tq], acc[tq:]).astype(BF16)


def _sb_attention(q, k, v, tq):
    B, L, _ = q.shape
    n_groups = N_ATT_HEADS // (2 * SB_CHAINS)
    R = 2 * tq
    W = SB_CHAINS * LANES
    r = np.arange(tq)
    u = (r[:, None] >= r[None, :]).astype(np.float32)
    u2 = jnp.asarray(np.concatenate([u, u], axis=0), BF16)
    qspec = pl.BlockSpec((1, tq, W), lambda b, p, i: (b, i, p))
    kspec = pl.BlockSpec((1, L, W), lambda b, p, i: (b, 0, p))
    return pl.pallas_call(
        functools.partial(_sb_kernel, tq=tq),
        grid=(B, n_groups, L // tq),
        in_specs=[qspec, kspec, kspec, pl.BlockSpec((2 * tq, tq), lambda b, p, i: (0, 0))],
        out_specs=qspec,
        out_shape=jax.ShapeDtypeStruct((B, L, ATT_W), BF16),
        scratch_shapes=[pltpu.VMEM((SB_CHAINS, R, LANES), BF16), pltpu.VMEM((SB_CHAINS, R, tq), F32),
                        pltpu.VMEM((SB_CHAINS, R, 2 * tq), BF16),
                        pltpu.VMEM((SB_CHAINS, R, tq), F32),
                        pltpu.VMEM((SB_CHAINS, R, tq), BF16), pltpu.VMEM((SB_CHAINS, R, 1), F32),
                        pltpu.VMEM((SB_CHAINS, R, 1), F32),
                        pltpu.VMEM((SB_CHAINS, R, LANES), F32)],
        compiler_params=_params(("arbitrary", "arbitrary", "arbitrary")),
        name="sb_attention",
    )(q, k, v, u2)


def _gla_tables():
    Q = CHUNK
    r = np.arange(Q)
    j = np.arange(Q)
    coef = [(j[None, :] <= r[:, None]), (j[None, :] > r[:, None])]
    masks = [np.eye(Q, dtype=bool)]
    for lvl in range(GLA_LEVELS):
        m = 1 << lvl
        c0 = (r // (2 * m)) * (2 * m)
        mid = c0 + m - 1
        upper = (r - c0) >= m
        up = (j[None, :] > mid[:, None]) & (j[None, :] <= r[:, None])
        lowr = (j[None, :] > r[:, None]) & (j[None, :] <= mid[:, None])
        coef.append(np.where(upper[:, None], up, lowr))
        masks.append((c0[:, None] == c0[None, :]) & upper[:, None] & (~upper)[None, :])
    coef = np.concatenate(coef, axis=0).astype(np.float32)
    coef2 = np.concatenate([coef, coef], axis=1)
    masks = np.stack(masks).astype(np.float32)
    hv = np.arange(GLA_V) // GLA_DV
    hk = np.arange(GLA_QK) // GLA_DK
    bdiag = (hv[:, None] == hk[None, :]).astype(np.float32)
    return jnp.asarray(coef2, BF16), jnp.asarray(masks, F32), jnp.asarray(bdiag, F32)


def _gla_kernel(gq_ref, gk_ref, gv_ref, glow_ref, gr_ref, w2_ref, gb_ref, coef_ref, mask_ref,
                bdiag_ref, nw_ref, o_ref, st_ref):
    @pl.when(pl.program_id(0) == 0)
    def _():
        st_ref[...] = jnp.zeros_like(st_ref)

    for b in range(gq_ref.shape[0]):
        _gla_chunk(gq_ref.at[b], gk_ref.at[b], gv_ref.at[b], glow_ref.at[b], gr_ref.at[b], w2_ref,
                   gb_ref, coef_ref, mask_ref, bdiag_ref, nw_ref, o_ref.at[b], st_ref.at[b])


def _gla_chunk(gq_ref, gk_ref, gv_ref, glow_ref, gr_ref, w2_ref, gb_ref, coef_ref, mask_ref,
               bdiag_ref, nw_ref, o_ref, st_ref):
    Q = CHUNK
    logits = _dot(glow_ref[...].astype(BF16), w2_ref[...]) + gb_ref[...]
    la = _log_sigmoid(logits) * (1.0 / GLA_GATE_NORM)
    hi, lo = _split2(la)
    expo = _dot(coef_ref[...], jnp.concatenate([hi, lo], axis=0))
    q = gq_ref[...]
    k = gk_ref[...]
    v = gv_ref[...]
    lane_head = lax.broadcasted_iota(jnp.int32, (Q, GLA_QK), 1) // GLA_DK
    row = lax.broadcasted_iota(jnp.int32, (Q, GLA_QK), 0)
    hmask = [lane_head == h for h in range(GLA_HEADS)]

    att = [None] * GLA_HEADS
    for lvl in range(-1, GLA_LEVELS):
        if lvl < 0:
            xq, xk = q, k.astype(BF16)
        else:
            m = 1 << lvl
            upper = (row & (2 * m - 1)) >= m
            xq = jnp.where(upper, q, k) * jnp.exp(expo[(2 + lvl) * Q:(3 + lvl) * Q, :])
            xk = xq.astype(BF16)
        msk = mask_ref[lvl + 1]
        for h in range(GLA_HEADS):
            part = _dot_nt(jnp.where(hmask[h], xq, 0.0).astype(BF16), xk) * msk
            att[h] = part if att[h] is None else att[h] + part

    st = st_ref[...]
    q_in = (q * jnp.exp(expo[0:Q, :])).astype(BF16)
    o = _dot_nt(q_in, st.astype(BF16))
    o_intra = [_dot(att[h].astype(BF16), v[:, h * GLA_DV:(h + 1) * GLA_DV].astype(BF16))
               for h in range(GLA_HEADS)]
    o = o + jnp.concatenate(o_intra, axis=1)

    k_dec = (k * jnp.exp(expo[Q:2 * Q, :])).astype(BF16)
    upd = _dot(v.T.astype(BF16), k_dec)
    g_last = expo[Q - 1:Q, :]
    st_ref[...] = st * jnp.exp(g_last) + upd * bdiag_ref[...]

    gr = gr_ref[...]
    o = jnp.concatenate([_rms(o[:, h * GLA_DV:(h + 1) * GLA_DV], nw_ref[...])
                         for h in range(GLA_HEADS)], axis=1)
    o_ref[...] = (o * _silu(gr)).astype(BF16)


def _gla(gq, gk, gv, glow, gr, gate_w2, gate_b, norm_w):
    B, L, _ = gq.shape
    Q = CHUNK
    coef2, masks, bdiag = _gla_tables()
    w2 = jnp.zeros((LANES, GLA_QK), F32).at[:GLA_RANK].set(gate_w2).astype(BF16)
    row = lambda width: pl.BlockSpec((B, Q, width), lambda c: (0, c, 0))
    const = lambda shape: pl.BlockSpec(shape, lambda c: (0,) * len(shape))
    return pl.pallas_call(
        _gla_kernel,
        grid=(L // Q,),
        in_specs=[row(GLA_QK), row(GLA_QK), row(GLA_V), row(LANES), row(GLA_V),
                  const((LANES, GLA_QK)), const((1, GLA_QK)), const(coef2.shape),
                  const(masks.shape), const(bdiag.shape), const((1, GLA_DV))],
        out_specs=row(GLA_V),
        out_shape=jax.ShapeDtypeStruct((B, L, GLA_V), BF16),
        scratch_shapes=[pltpu.VMEM((B, GLA_V, GLA_QK), F32)],
        compiler_params=_params(("arbitrary",)),
        name="gla_scan",
    )(gq, gk, gv, glow, gr, w2, gate_b.reshape(1, -1), coef2, masks, bdiag,
      norm_w.reshape(1, -1))


def _block(n, want):
    return want if n % want == 0 else n


def kernel(x, norm_mix, norm_mlp, norm_final, w_in_ab, fox_f_bias, ssd_conv_w, ssd_conv_b,
           ssd_dt_bias, ssd_a_log, ssd_d, ssd_norm, w_out_ab, w_in_cd, gla_gate_w2,
           gla_gate_b, gla_norm, w_out_cd, w_mlp_up, w_mlp_down):
    B, L, D = x.shape
    assert L % CHUNK == 0
    tm = _block(L, 512)
    tq = _block(L, 256)
    tq_fox = _block(L, 512)

    qd, kd, vd, qa, ka, z, xbc, dt = _inproj_ab(x, norm_mix[0], w_in_ab[0], fox_f_bias[0],
                                                ssd_dt_bias[0], tm)
    y_fox = _fox_attention(qd, qa, kd, ka, vd, tq_fox)
    y_ssd = _ssd(xbc, z, dt, ssd_conv_w[0], ssd_conv_b[0], ssd_a_log[0], ssd_d[0], ssd_norm[0])
    h = _mix_mlp(x, y_fox, y_ssd, w_out_ab[0], norm_mlp[0], w_mlp_up[0], w_mlp_down[0],
                 norm_final, tm, final=False)

    sq, sk, sv, gq, gk, gv, gr, glow = _inproj_cd(h, norm_mix[1], w_in_cd[0], tm)
    y_sb = _sb_attention(sq, sk, sv, tq)
    y_gla = _gla(gq, gk, gv, glow, gr, gla_gate_w2[0], gla_gate_b[0], gla_norm[0])
    return _mix_mlp(h, y_sb, y_gla, w_out_cd[0], norm_mlp[1], w_mlp_up[1], w_mlp_down[1],
                    norm_final, tm, final=True)
```

```python
import functools
import math

import numpy as np
import jax
import jax.numpy as jnp
from jax import lax
from jax.experimental import pallas as pl
from jax.experimental.pallas import tpu as pltpu

F32 = jnp.float32
BF16 = jnp.bfloat16

LANES = 128
SUBLANES = 8
VMEM_LIMIT_BYTES = 56 * 1024 * 1024

HEAD_DIM = 64
N_ATT_HEADS = 8
ATT_W = N_ATT_HEADS * HEAD_DIM
SSD_HEADS = 8
SSD_W = 512
SSD_GROUPS = 2
SSD_STATE = 64
SSD_CONV = 4
SSD_CONV_DIM = SSD_W + 2 * SSD_GROUPS * SSD_STATE
GLA_HEADS = 4
GLA_DK = 64
GLA_DV = 128
GLA_RANK = 16
GLA_GATE_NORM = 16.0
EPS = 1e-5
CHUNK = 128
GLA_LEVELS = 7
LOG2E = math.log2(math.e)
DT_LANE0 = 8


def _params(sem):
    return pltpu.CompilerParams(dimension_semantics=sem, vmem_limit_bytes=VMEM_LIMIT_BYTES)


def _split3(x):
    hi = x.astype(BF16)
    r = x - hi.astype(F32)
    mid = r.astype(BF16)
    lo = (r - mid.astype(F32)).astype(BF16)
    return hi, mid, lo


def _split2(x):
    hi = x.astype(BF16)
    lo = (x - hi.astype(F32)).astype(BF16)
    return hi, lo


def _softplus(x):
    return jnp.maximum(x, 0.0) + jnp.log1p(jnp.exp(-jnp.abs(x)))


def _log_sigmoid(x):
    return jnp.minimum(x, 0.0) - jnp.log1p(jnp.exp(-jnp.abs(x)))


def _silu(x):
    return x / (1.0 + jnp.exp(-x))


def _rms(x, w):
    ms = jnp.mean(x * x, axis=-1, keepdims=True)
    return x * lax.rsqrt(ms + EPS) * w


def _dot(a, b):
    return jnp.dot(a, b, preferred_element_type=F32)


def _dot_nt(a, b):
    return lax.dot_general(a, b, (((1,), (1,)), ((), ())), preferred_element_type=F32)


AB_COLS = 3 * ATT_W + SSD_W + SSD_CONV_DIM + LANES


def _inproj_ab_kernel(x_ref, nw_ref, w_ref, sb_ref, tri3_ref, sel_ref, cst_ref,
                      qd_ref, kd_ref, vd_ref, qa_ref, ka_ref, z_ref, xbc_ref, dt_ref, fend_ref,
                      carry_ref):
    @pl.when(pl.program_id(1) == 0)
    def _():
        carry_ref[...] = jnp.zeros_like(carry_ref)

    u = _rms(x_ref[0], nw_ref[...]).astype(BF16)

    def mm(a, b):
        return _dot(u, w_ref[:, a:b])

    o = 0
    qd_ref[0] = (mm(o, o + ATT_W) * (LOG2E * HEAD_DIM ** -0.5)).astype(BF16); o += ATT_W
    kd_ref[0] = mm(o, o + ATT_W).astype(BF16); o += ATT_W
    vd_ref[0] = mm(o, o + ATT_W).astype(BF16); o += ATT_W
    z_ref[0] = mm(o, o + SSD_W); o += SSD_W
    xbc_ref[0] = mm(o, o + SSD_CONV_DIM); o += SSD_CONV_DIM
    small = mm(o, o + LANES) + sb_ref[...]

    lane = lax.broadcasted_iota(jnp.int32, small.shape, 1)
    is_f = lane < N_ATT_HEADS
    is_dt = (lane >= DT_LANE0) & (lane < DT_LANE0 + SSD_HEADS)
    log_f = jnp.where(is_f, _log_sigmoid(small), 0.0)
    dt_ref[0] = jnp.where(is_dt, _softplus(small), 0.0)

    hi, mid, lo = _split3(log_f)
    cum = _dot(tri3_ref[...], jnp.concatenate([hi, mid, lo], axis=0)) + carry_ref[...]
    tm = cum.shape[0]
    carry_ref[...] = cum[tm - 1:tm, :]

    cum2 = cum * LOG2E
    fend_ref[0, 0] = cum2[tm - 1:tm, :]
    fh, fm, fl = _split3(cum2)
    fcat = (fh.astype(F32) + pltpu.roll(fm.astype(F32), N_ATT_HEADS, 1)
            + pltpu.roll(fl.astype(F32), 2 * N_ATT_HEADS, 1)).astype(BF16)
    aug = _dot(fcat, sel_ref[...]) + cst_ref[...]
    qa_ref[0] = aug[:, :ATT_W].astype(BF16)
    ka_ref[0] = aug[:, ATT_W:].astype(BF16)


def _aug_tables():
    sel = np.zeros((LANES, 2 * ATT_W), np.float32)
    cst = np.zeros((1, 2 * ATT_W), np.float32)
    for h in range(N_ATT_HEADS):
        p, e = divmod(h, 2)
        for part in range(3):
            src = part * N_ATT_HEADS + h
            sel[src, p * LANES + 6 * e + part] = 1.0
            sel[src, ATT_W + p * LANES + 6 * e + 3 + part] = -1.0
            cst[0, p * LANES + 6 * e + 3 + part] = 1.0
            cst[0, ATT_W + p * LANES + 6 * e + part] = 1.0
    return jnp.asarray(sel, BF16), jnp.asarray(cst, F32)


def _tri_incl(n):
    r = np.arange(n)
    return (r[None, :] <= r[:, None]).astype(np.float32)


def _inproj_ab(h, nw, w_in, f_bias, dt_bias, tm):
    B, L, D = h.shape
    fq, fk, fv, fl, wz, wxbc, wdt = jnp.split(w_in, np.cumsum(
        [ATT_W, ATT_W, ATT_W, N_ATT_HEADS, SSD_W, SSD_CONV_DIM])[:].tolist(), axis=1)
    pad = jnp.zeros((D, LANES - N_ATT_HEADS - SSD_HEADS), w_in.dtype)
    w = jnp.concatenate([fq, fk, fv, wz, wxbc, fl, wdt, pad], axis=1).astype(BF16)
    sb = jnp.concatenate([f_bias, dt_bias, jnp.zeros((LANES - 16,), F32)]).reshape(1, LANES)
    tri = _tri_incl(tm)
    tri3 = jnp.asarray(np.concatenate([tri, tri, tri], axis=1), BF16)
    sel, cst = _aug_tables()

    row = lambda width: pl.BlockSpec((1, tm, width), lambda b, i: (b, i, 0))
    const = lambda shape: pl.BlockSpec(shape, lambda b, i: (0,) * len(shape))
    outs = [
        jax.ShapeDtypeStruct((B, L, ATT_W), BF16),
        jax.ShapeDtypeStruct((B, L, ATT_W), BF16),
        jax.ShapeDtypeStruct((B, L, ATT_W), BF16),
        jax.ShapeDtypeStruct((B, L, ATT_W), BF16),
        jax.ShapeDtypeStruct((B, L, ATT_W), BF16),
        jax.ShapeDtypeStruct((B, L, SSD_W), F32),
        jax.ShapeDtypeStruct((B, L, SSD_CONV_DIM), F32),
        jax.ShapeDtypeStruct((B, L, LANES), F32),
        jax.ShapeDtypeStruct((B, L // tm, 1, LANES), F32),
    ]
    return pl.pallas_call(
        _inproj_ab_kernel,
        grid=(B, L // tm),
        in_specs=[row(D), const((1, D)), const((D, AB_COLS)), const((1, LANES)),
                  const((tm, 3 * tm)), const((LANES, 2 * ATT_W)), const((1, 2 * ATT_W))],
        out_specs=[row(ATT_W)] * 5 + [row(SSD_W), row(SSD_CONV_DIM), row(LANES),
                   pl.BlockSpec((1, 1, 1, LANES), lambda b, i: (b, i, 0, 0))],
        out_shape=outs,
        scratch_shapes=[pltpu.VMEM((1, LANES), F32)],
        compiler_params=_params(("arbitrary", "arbitrary")),
        name="inproj_ab",
    )(h, nw.reshape(1, D), w, sb, tri3, sel, cst)


def _head_lane_mask(e, shape):
    lane = lax.broadcasted_iota(jnp.int32, shape, 1)
    return (lane < HEAD_DIM) if e == 0 else (lane >= HEAD_DIM)


EXP2_DEAD = -151.0
FOX_STRIP = 64


def _fox_kernel(qd_ref, qa_ref, kd_ref, ka_ref, v_ref, fend_ref, o_ref,
                q_ref, s0_ref, s1_ref, p0_ref, p1_ref, m_ref, al_ref, acc_ref, kn_ref, *, tq):
    i = pl.program_id(2)
    tk = tq
    R = 2 * tq
    qd = qd_ref[0].astype(F32)
    qa = qa_ref[0].astype(F32)
    lane = lax.broadcasted_iota(jnp.int32, (tq, LANES), 1)
    first = lane < HEAD_DIM
    for e in range(2):
        amask = (lane >= 6 * e) & (lane < 6 * e + 6)
        q_ref[e * tq:(e + 1) * tq, :] = jnp.concatenate(
            [jnp.where(first if e == 0 else ~first, qd, 0.0), jnp.where(amask, qa, 0.0)],
            axis=1).astype(BF16)
    m_ref[...] = jnp.full(m_ref.shape, -jnp.inf, F32)
    acc_ref[...] = jnp.zeros(acc_ref.shape, F32)
    ones = jnp.ones((tk, LANES), BF16)

    def head_sq_norms(x):
        sq = x * x
        return (jnp.sum(jnp.where(first, sq, 0.0), axis=1, keepdims=True),
                jnp.sum(jnp.where(first, 0.0, sq), axis=1, keepdims=True))

    @pl.when(i == 0)
    def _():
        def kbody(c, carry):
            ks = pl.multiple_of(c * tk, tk)
            n0, n1 = head_sq_norms(kd_ref[0, pl.ds(ks, tk), :].astype(F32))
            return jnp.maximum(carry[0], n0), jnp.maximum(carry[1], n1)
        zero = jnp.zeros((tk, 1), F32)
        n0, n1 = lax.fori_loop(0, kd_ref.shape[1] // tk, kbody, (zero, zero))
        for e, n in enumerate((n0, n1)):
            kn_ref[e] = jnp.broadcast_to(jnp.sqrt(jnp.max(n, axis=0, keepdims=True)), kn_ref.shape[1:])

    def scores(j, s_ref):
        ks = pl.multiple_of(j * tk, tk)
        k = jnp.concatenate([kd_ref[0, pl.ds(ks, tk), :], ka_ref[0, pl.ds(ks, tk), :]], axis=1)
        s_ref[...] = _dot_nt(q_ref[...], k)

    def update(j, s_ref, p_ref, masked):
        for r0 in range(0, R, FOX_STRIP):
            rs = slice(r0, r0 + FOX_STRIP)
            s = s_ref[rs, :]
            if masked:
                rows = lax.broadcasted_iota(jnp.int32, s.shape, 0) + (r0 % tq)
                cols = lax.broadcasted_iota(jnp.int32, s.shape, 1)
                s = jnp.where(rows >= cols, s, -jnp.inf)
            m_old = m_ref[rs, :]
            m_new = jnp.maximum(m_old, jnp.max(s, axis=1, keepdims=True))
            al_ref[rs, :] = jnp.exp2(m_old - m_new)
            m_ref[rs, :] = m_new
            p_ref[rs, :] = jnp.exp2(s - m_new).astype(BF16)
        ks = pl.multiple_of(j * tk, tk)
        v1 = jnp.concatenate([v_ref[0, pl.ds(ks, tk), :], ones], axis=1)
        acc_ref[...] = al_ref[...] * acc_ref[...] + _dot(p_ref[...], v1)

    scores(i, s0_ref)
    update(i, s0_ref, p0_ref, True)

    qn = head_sq_norms(qd)
    fend = fend_ref[0]
    hl = lax.broadcasted_iota(jnp.int32, fend.shape, 1)
    jcol = lax.broadcasted_iota(jnp.int32, (fend.shape[0], 1), 0)
    jmin = i
    for e in range(2):
        head = 2 * pl.program_id(1) + e
        fe = jnp.sum(jnp.where(hl == head, fend, 0.0), axis=1, keepdims=True)
        f_top = jnp.sum(jnp.where(jcol == i - 1, fe, 0.0), axis=0, keepdims=True)
        qk = jnp.sqrt(jnp.max(qn[e], axis=0, keepdims=True)) * kn_ref[e, 0:1, 0:1]
        m_min = jnp.min(m_ref[e * tq:(e + 1) * tq, :], axis=0, keepdims=True)
        live = (jcol < i) & (qk + f_top - fe + 1.0 - m_min > EXP2_DEAD)
        jmin = jnp.minimum(jmin, jnp.min(jnp.where(live, jcol, i)))
    nb = i - jmin

    @pl.when(nb > 0)
    def _():
        scores(jmin, s0_ref)

    def pair(u, carry):
        j = jmin + 2 * u
        scores(j + 1, s1_ref)
        update(j, s0_ref, p0_ref, False)
        scores(j + 2, s0_ref)
        update(j + 1, s1_ref, p1_ref, False)
        return carry

    lax.fori_loop(0, nb // 2, pair, 0)

    @pl.when(nb % 2 == 1)
    def _():
        update(i - 1, s0_ref, p0_ref, False)

    acc = acc_ref[...]
    out = acc[:, :LANES] / acc[:, LANES:]
    o_ref[0] = jnp.where(first, out[:tq], out[tq:]).astype(BF16)


def _fox_attention(qd, qa, kd, ka, v, fend, tq):
    B, L, _ = qd.shape
    assert fend.shape == (B, L // tq, LANES)
    n_pairs = N_ATT_HEADS // 2
    R = 2 * tq
    qspec = pl.BlockSpec((1, tq, LANES), lambda b, p, i: (b, i, p))
    kspec = pl.BlockSpec((1, L, LANES), lambda b, p, i: (b, 0, p))
    return pl.pallas_call(
        functools.partial(_fox_kernel, tq=tq),
        grid=(B, n_pairs, L // tq),
        in_specs=[qspec, qspec, kspec, kspec, kspec,
                  pl.BlockSpec((1, L // tq, LANES), lambda b, p, i: (b, 0, 0))],
        out_specs=qspec,
        out_shape=jax.ShapeDtypeStruct((B, L, ATT_W), BF16),
        scratch_shapes=[pltpu.VMEM((R, 2 * LANES), BF16),
                        pltpu.VMEM((R, tq), F32), pltpu.VMEM((R, tq), F32),
                        pltpu.VMEM((R, tq), BF16), pltpu.VMEM((R, tq), BF16),
                        pltpu.VMEM((R, 1), F32), pltpu.VMEM((R, 1), F32),
                        pltpu.VMEM((R, 2 * LANES), F32),
                        pltpu.VMEM((2, SUBLANES, LANES), F32)],
        compiler_params=_params(("arbitrary", "arbitrary", "arbitrary")),
        name="fox_attention",
    )(qd, qa, kd, ka, v, fend)


def _ssd_kernel(xbc_ref, z_ref, dt_ref, cw_ref, cb_ref, alog_ref, dsk_ref, nw_ref, tri3_ref,
                y_ref, xext_ref, st_ref):
    @pl.when(pl.program_id(0) == 0)
    def _():
        xext_ref[:, 0:SUBLANES, :] = jnp.zeros((xext_ref.shape[0], SUBLANES, SSD_CONV_DIM), F32)
        st_ref[...] = jnp.zeros_like(st_ref)

    for b in range(xbc_ref.shape[0]):
        _ssd_chunk(xbc_ref.at[b], z_ref.at[b], dt_ref.at[b], cw_ref, cb_ref, alog_ref, dsk_ref,
                   nw_ref, tri3_ref, y_ref.at[b], xext_ref.at[b], st_ref.at[b])


def _ssd_chunk(xbc_ref, z_ref, dt_ref, cw_ref, cb_ref, alog_ref, dsk_ref, nw_ref, tri3_ref,
               y_ref, xext_ref, st_ref):
    Q = CHUNK
    xext_ref[SUBLANES:SUBLANES + Q, :] = xbc_ref[...]
    conv = cb_ref[...]
    for k in range(SSD_CONV):
        conv = conv + cw_ref[k:k + 1, :] * xext_ref[pl.ds(SUBLANES - (SSD_CONV - 1) + k, Q), :]
    xext_ref[0:SUBLANES, :] = xext_ref[Q:Q + SUBLANES, :]
    xc = _silu(conv)
    xs = xc[:, :SSD_W]
    bm = xc[:, SSD_W:SSD_W + LANES]
    cm = xc[:, SSD_W + LANES:]

    dt = dt_ref[...]
    a = dt * (-jnp.exp(alog_ref[...]))
    hi, mid, lo = _split3(a)
    a_cum = _dot(tri3_ref[...], jnp.concatenate([hi, mid, lo], axis=0))
    a_row = a_cum.T
    a_last_col = a_cum[Q - 1:Q, :]
    bm_t = bm.T

    rows = lax.broadcasted_iota(jnp.int32, (Q, Q), 0)
    cols = lax.broadcasted_iota(jnp.int32, (Q, Q), 1)
    causal = rows >= cols
    lane = lax.broadcasted_iota(jnp.int32, (Q, LANES), 1)
    first_half = lane < HEAD_DIM

    scores = []
    cmask = []
    for g in range(SSD_GROUPS):
        cg = jnp.where(_head_lane_mask(g, (Q, LANES)), cm, 0.0).astype(BF16)
        cmask.append(cg)
        scores.append(_dot_nt(cg, bm.astype(BF16)))

    y_pairs = []
    heads_per_group = SSD_HEADS // SSD_GROUPS
    for p in range(SSD_HEADS // 2):
        xs_pair = xs[:, p * LANES:(p + 1) * LANES]
        dt_pair = jnp.where(first_half, dt[:, DT_LANE0 + 2 * p:DT_LANE0 + 2 * p + 1],
                            dt[:, DT_LANE0 + 2 * p + 1:DT_LANE0 + 2 * p + 2])
        xdt = (xs_pair * dt_pair).astype(BF16)
        y_head = []
        for e in range(2):
            h = 2 * p + e
            g = h // heads_per_group
            hl = DT_LANE0 + h
            acol = a_cum[:, hl:hl + 1]
            arow = a_row[hl:hl + 1, :]
            alast = a_last_col[:, hl:hl + 1]
            lmat = jnp.exp(jnp.where(causal, acol - arow, -jnp.inf))
            y_diag = _dot((scores[g] * lmat).astype(BF16), xdt)
            prev = st_ref[h]
            y_off = _dot(cmask[g], prev.astype(BF16)) * jnp.exp(acol)
            y_head.append(y_diag + y_off)
            decay_row = jnp.exp(alast - arow)
            local = _dot((bm_t * decay_row).astype(BF16), xdt)
            st_ref[h] = prev * jnp.exp(alast) + local
        y_pairs.append(jnp.where(first_half, y_head[0], y_head[1]))
    y = jnp.concatenate(y_pairs, axis=1) + dsk_ref[...] * xs
    y = y * _silu(z_ref[...])
    gw = SSD_W // SSD_GROUPS
    y = jnp.concatenate([_rms(y[:, g * gw:(g + 1) * gw], nw_ref[:, g * gw:(g + 1) * gw])
                         for g in range(SSD_GROUPS)], axis=1)
    y_ref[...] = y.astype(BF16)


def _ssd(xbc, z, dt, conv_w, conv_b, a_log, d_skip, norm_w):
    B, L, _ = xbc.shape
    Q = CHUNK
    alog = jnp.zeros((1, LANES), F32).at[0, DT_LANE0:DT_LANE0 + SSD_HEADS].set(a_log)
    dsk = jnp.repeat(d_skip, SSD_W // SSD_HEADS).reshape(1, SSD_W)
    tri = _tri_incl(Q)
    tri3 = jnp.asarray(np.concatenate([tri, tri, tri], axis=1), BF16)
    row = lambda width: pl.BlockSpec((B, Q, width), lambda c: (0, c, 0))
    const = lambda shape: pl.BlockSpec(shape, lambda c: (0,) * len(shape))
    return pl.pallas_call(
        _ssd_kernel,
        grid=(L // Q,),
        in_specs=[row(SSD_CONV_DIM), row(SSD_W), row(LANES), const((SSD_CONV, SSD_CONV_DIM)),
                  const((1, SSD_CONV_DIM)), const((1, LANES)), const((1, SSD_W)),
                  const((1, SSD_W)), const((Q, 3 * Q))],
        out_specs=row(SSD_W),
        out_shape=jax.ShapeDtypeStruct((B, L, SSD_W), BF16),
        scratch_shapes=[pltpu.VMEM((B, Q + SUBLANES, SSD_CONV_DIM), F32),
                        pltpu.VMEM((B, SSD_HEADS, LANES, LANES), F32)],
        compiler_params=_params(("arbitrary",)),
        name="ssd_scan",
    )(xbc, z, dt, conv_w, conv_b.reshape(1, -1), alog, dsk, norm_w.reshape(1, -1), tri3)


def _mix_mlp_kernel(h_ref, y1_ref, y2_ref, wo_ref, nw_ref, wup_ref, wdn_ref, nf_ref, o_ref,
                    *, ff_chunk, final):
    half = y1_ref.shape[-1]
    mix = _dot(y1_ref[...], wo_ref[0:half, :]) + _dot(y2_ref[...], wo_ref[half:2 * half, :])
    h1 = h_ref[...] + mix
    u = _rms(h1, nw_ref[...]).astype(BF16)
    d_ff = wup_ref.shape[1]
    acc = jnp.zeros_like(h1)
    for c in range(d_ff // ff_chunk):
        act = jnp.maximum(_dot(u, wup_ref[:, c * ff_chunk:(c + 1) * ff_chunk]), 0.0)
        acc = acc + _dot((act * act).astype(BF16), wdn_ref[c * ff_chunk:(c + 1) * ff_chunk, :])
    h2 = h1 + acc
    if final:
        h2 = _rms(h2, nf_ref[...])
    o_ref[...] = h2


def _mix_mlp(h, y1, y2, w_out, nw, w_up, w_down, layer, nf, tm, final):
    B, L, D = h.shape
    T = B * L
    d_ff = w_up.shape[2]
    half = y1.shape[-1]
    row = lambda width: pl.BlockSpec((tm, width), lambda i: (i, 0))
    const = lambda shape: pl.BlockSpec(shape, lambda i: (0,) * len(shape))
    stacked = lambda shape: pl.BlockSpec((None,) + shape, lambda i: (layer, 0, 0))
    out = pl.pallas_call(
        functools.partial(_mix_mlp_kernel, ff_chunk=min(1024, d_ff), final=final),
        grid=(T // tm,),
        in_specs=[row(D), row(half), row(half), const((2 * half, D)), const((1, D)),
                  stacked((D, d_ff)), stacked((d_ff, D)), const((1, D))],
        out_specs=row(D),
        out_shape=jax.ShapeDtypeStruct((T, D), F32),
        compiler_params=_params(("arbitrary",)),
        name="mix_mlp",
    )(h.reshape(T, D), y1.reshape(T, half), y2.reshape(T, half), w_out.astype(BF16),
      nw.reshape(1, D), w_up, w_down, nf.reshape(1, D))
    return out.reshape(B, L, D)


GLA_QK = GLA_HEADS * GLA_DK
GLA_V = GLA_HEADS * GLA_DV
CD_COLS = 3 * ATT_W + 2 * GLA_QK + 2 * GLA_V + LANES


def _inproj_cd_kernel(x_ref, nw_ref, w_ref, sq_ref, sk_ref, sv_ref, gq_ref, gk_ref, gv_ref,
                      gr_ref, glow_ref):
    u = _rms(x_ref[...], nw_ref[...]).astype(BF16)

    def mm(a, b):
        return _dot(u, w_ref[:, a:b])

    o = 0
    sq_ref[...] = (mm(o, o + ATT_W) * (LOG2E * HEAD_DIM ** -0.5)).astype(BF16); o += ATT_W
    sk_ref[...] = mm(o, o + ATT_W).astype(BF16); o += ATT_W
    sv_ref[...] = mm(o, o + ATT_W).astype(BF16); o += ATT_W
    gq_ref[...] = mm(o, o + GLA_QK) * (GLA_DK ** -0.5); o += GLA_QK
    gk_ref[...] = mm(o, o + GLA_QK); o += GLA_QK
    gv_ref[...] = mm(o, o + GLA_V); o += GLA_V
    gr_ref[...] = mm(o, o + GLA_V); o += GLA_V
    glow_ref[...] = mm(o, o + LANES)


def _inproj_cd(h, nw, w_in, tm):
    B, L, D = h.shape
    T = B * L
    sq, sk, sv, gq, gk, gv, glow, gr = jnp.split(w_in, np.cumsum(
        [ATT_W, ATT_W, ATT_W, GLA_QK, GLA_QK, GLA_V, GLA_RANK]).tolist(), axis=1)
    pad = jnp.zeros((D, LANES - GLA_RANK), w_in.dtype)
    w = jnp.concatenate([sq, sk, sv, gq, gk, gv, gr, glow, pad], axis=1).astype(BF16)
    row = lambda width: pl.BlockSpec((tm, width), lambda i: (i, 0))
    const = lambda shape: pl.BlockSpec(shape, lambda i: (0,) * len(shape))
    widths = [ATT_W, ATT_W, ATT_W, GLA_QK, GLA_QK, GLA_V, GLA_V, LANES]
    dtypes = [BF16, BF16, BF16, F32, F32, F32, F32, F32]
    outs = pl.pallas_call(
        _inproj_cd_kernel,
        grid=(T // tm,),
        in_specs=[row(D), const((1, D)), const((D, CD_COLS))],
        out_specs=[row(wd) for wd in widths],
        out_shape=[jax.ShapeDtypeStruct((T, wd), dt) for wd, dt in zip(widths, dtypes)],
        compiler_params=_params(("arbitrary",)),
        name="inproj_cd",
    )(h.reshape(T, D), nw.reshape(1, D), w)
    return [o.reshape(B, L, -1) for o in outs]


SB_DEAD = EXP2_DEAD
SB_STRIP = 64
SB_CHAINS = 2


def _sb_kernel(q_ref, k_ref, v_ref, u2_ref, o_ref,
               qs_ref, z_ref, lr_ref, in_ref, a_ref, c_ref, cn_ref, acc_ref, *, tq):
    i = pl.program_id(2)
    tk = tq
    R = 2 * tq
    chains = range(SB_CHAINS)
    for ch in chains:
        qf = q_ref[0, :, ch * LANES:(ch + 1) * LANES].astype(F32)
        for e in range(2):
            qs_ref[ch, e * tq:(e + 1) * tq, :] = jnp.where(_head_lane_mask(e, (tq, LANES)), qf,
                                                           0.0).astype(BF16)
    c_ref[...] = jnp.zeros(c_ref.shape, F32)
    acc_ref[...] = jnp.zeros(acc_ref.shape, F32)

    def strict_mask(r0, shape):
        rows = lax.broadcasted_iota(jnp.int32, shape, 0) + (r0 % tq)
        return lax.broadcasted_iota(jnp.int32, shape, 1) < rows

    def step(j, masked):
        ks = pl.multiple_of(j * tk, tk)
        for ch in chains:
            z_ref[ch] = _dot_nt(qs_ref[ch], k_ref[0, pl.ds(ks, tk), ch * LANES:(ch + 1) * LANES])
        for ch in chains:
            for r0 in range(0, R, SB_STRIP):
                rs = slice(r0, r0 + SB_STRIP)
                z = z_ref[ch, rs, :]
                nz = -z
                lr = jnp.minimum(nz, 0.0) - jnp.log2(1.0 + jnp.exp2(jnp.minimum(z, nz)))
                if masked:
                    lr = jnp.where(strict_mask(r0, lr.shape), lr, 0.0)
                hi, lo = _split2(lr)
                lr_ref[ch, rs, :tk] = hi
                lr_ref[ch, rs, tk:] = lo
                cn_ref[ch, rs, :] = c_ref[ch, rs, :] + jnp.sum(lr, axis=1, keepdims=True)
        for ch in chains:
            in_ref[ch] = _dot(lr_ref[ch], u2_ref[...])
        for ch in chains:
            for r0 in range(0, R, SB_STRIP):
                rs = slice(r0, r0 + SB_STRIP)
                a = jnp.exp2(z_ref[ch, rs, :] + c_ref[ch, rs, :] + in_ref[ch, rs, :])
                if masked:
                    a = jnp.where(strict_mask(r0, a.shape), a, 0.0)
                a_ref[ch, rs, :] = a.astype(BF16)
        for ch in chains:
            acc_ref[ch] += _dot(a_ref[ch], v_ref[0, pl.ds(ks, tk), ch * LANES:(ch + 1) * LANES])
        cn = cn_ref[...]
        c_ref[...] = cn
        return jnp.max(cn)

    cmax = step(i, True)

    def cond(carry):
        j, cmax = carry
        return (j >= 0) & (cmax > SB_DEAD)

    def body(carry):
        j, _ = carry
        return j - 1, step(j, False)

    lax.while_loop(cond, body, (i - 1, cmax))
    for ch in chains:
        acc = acc_ref[ch]
        o_ref[0, :, ch * LANES:(ch + 1) * LANES] = jnp.where(
            _head_lane_mask(0, (tq, LANES)), acc[:tq], acc[tq:]).astype(BF16)


def _sb_attention(q, k, v, tq):
    B, L, _ = q.shape
    n_groups = N_ATT_HEADS // (2 * SB_CHAINS)
    R = 2 * tq
    W = SB_CHAINS * LANES
    r = np.arange(tq)
    u = (r[:, None] >= r[None, :]).astype(np.float32)
    u2 = jnp.asarray(np.concatenate([u, u], axis=0), BF16)
    qspec = pl.BlockSpec((1, tq, W), lambda b, p, i: (b, i, p))
    kspec = pl.BlockSpec((1, L, W), lambda b, p, i: (b, 0, p))
    return pl.pallas_call(
        functools.partial(_sb_kernel, tq=tq),
        grid=(B, n_groups, L // tq),
        in_specs=[qspec, kspec, kspec, pl.BlockSpec((2 * tq, tq), lambda b, p, i: (0, 0))],
        out_specs=qspec,
        out_shape=jax.ShapeDtypeStruct((B, L, ATT_W), BF16),
        scratch_shapes=[pltpu.VMEM((SB_CHAINS, R, LANES), BF16), pltpu.VMEM((SB_CHAINS, R, tq), F32),
                        pltpu.VMEM((SB_CHAINS, R, 2 * tq), BF16),
                        pltpu.VMEM((SB_CHAINS, R, tq), F32),
                        pltpu.VMEM((SB_CHAINS, R, tq), BF16), pltpu.VMEM((SB_CHAINS, R, 1), F32),
                        pltpu.VMEM((SB_CHAINS, R, 1), F32),
                        pltpu.VMEM((SB_CHAINS, R, LANES), F32)],
        compiler_params=_params(("arbitrary", "arbitrary", "arbitrary")),
        name="sb_attention",
    )(q, k, v, u2)


def _gla_tables():
    Q = CHUNK
    r = np.arange(Q)
    j = np.arange(Q)
    coef = [(j[None, :] <= r[:, None]), (j[None, :] > r[:, None])]
    masks = [np.eye(Q, dtype=bool)]
    for lvl in range(GLA_LEVELS):
        m = 1 << lvl
        c0 = (r // (2 * m)) * (2 * m)
        mid = c0 + m - 1
        upper = (r - c0) >= m
        up = (j[None, :] > mid[:, None]) & (j[None, :] <= r[:, None])
        lowr = (j[None, :] > r[:, None]) & (j[None, :] <= mid[:, None])
        coef.append(np.where(upper[:, None], up, lowr))
        masks.append((c0[:, None] == c0[None, :]) & upper[:, None] & (~upper)[None, :])
    coef = np.concatenate(coef, axis=0).astype(np.float32)
    coef2 = np.concatenate([coef, coef], axis=1)
    masks = np.stack(masks).astype(np.float32)
    hv = np.arange(GLA_V) // GLA_DV
    hk = np.arange(GLA_QK) // GLA_DK
    bdiag = (hv[:, None] == hk[None, :]).astype(np.float32)
    return jnp.asarray(coef2, BF16), jnp.asarray(masks, F32), jnp.asarray(bdiag, F32)


def _gla_kernel(gq_ref, gk_ref, gv_ref, glow_ref, gr_ref, w2_ref, gb_ref, coef_ref, mask_ref,
                bdiag_ref, nw_ref, o_ref, st_ref):
    @pl.when(pl.program_id(0) == 0)
    def _():
        st_ref[...] = jnp.zeros_like(st_ref)

    for b in range(gq_ref.shape[0]):
        _gla_chunk(gq_ref.at[b], gk_ref.at[b], gv_ref.at[b], glow_ref.at[b], gr_ref.at[b], w2_ref,
                   gb_ref, coef_ref, mask_ref, bdiag_ref, nw_ref, o_ref.at[b], st_ref.at[b])


def _gla_chunk(gq_ref, gk_ref, gv_ref, glow_ref, gr_ref, w2_ref, gb_ref, coef_ref, mask_ref,
               bdiag_ref, nw_ref, o_ref, st_ref):
    Q = CHUNK
    logits = _dot(glow_ref[...].astype(BF16), w2_ref[...]) + gb_ref[...]
    la = _log_sigmoid(logits) * (1.0 / GLA_GATE_NORM)
    hi, lo = _split2(la)
    expo = _dot(coef_ref[...], jnp.concatenate([hi, lo], axis=0))
    q = gq_ref[...]
    k = gk_ref[...]
    v = gv_ref[...]
    lane_head = lax.broadcasted_iota(jnp.int32, (Q, GLA_QK), 1) // GLA_DK
    row = lax.broadcasted_iota(jnp.int32, (Q, GLA_QK), 0)
    hmask = [lane_head == h for h in range(GLA_HEADS)]

    att = [None] * GLA_HEADS
    for lvl in range(-1, GLA_LEVELS):
        if lvl < 0:
            xq, xk = q, k.astype(BF16)
        else:
            m = 1 << lvl
            upper = (row & (2 * m - 1)) >= m
            xq = jnp.where(upper, q, k) * jnp.exp(expo[(2 + lvl) * Q:(3 + lvl) * Q, :])
            xk = xq.astype(BF16)
        msk = mask_ref[lvl + 1]
        for h in range(GLA_HEADS):
            part = _dot_nt(jnp.where(hmask[h], xq, 0.0).astype(BF16), xk) * msk
            att[h] = part if att[h] is None else att[h] + part

    st = st_ref[...]
    q_in = (q * jnp.exp(expo[0:Q, :])).astype(BF16)
    o = _dot_nt(q_in, st.astype(BF16))
    o_intra = [_dot(att[h].astype(BF16), v[:, h * GLA_DV:(h + 1) * GLA_DV].astype(BF16))
               for h in range(GLA_HEADS)]
    o = o + jnp.concatenate(o_intra, axis=1)

    k_dec = (k * jnp.exp(expo[Q:2 * Q, :])).astype(BF16)
    upd = _dot(v.T.astype(BF16), k_dec)
    g_last = expo[Q - 1:Q, :]
    st_ref[...] = st * jnp.exp(g_last) + upd * bdiag_ref[...]

    gr = gr_ref[...]
    o = jnp.concatenate([_rms(o[:, h * GLA_DV:(h + 1) * GLA_DV], nw_ref[...])
                         for h in range(GLA_HEADS)], axis=1)
    o_ref[...] = (o * _silu(gr)).astype(BF16)


def _gla(gq, gk, gv, glow, gr, gate_w2, gate_b, norm_w):
    B, L, _ = gq.shape
    Q = CHUNK
    coef2, masks, bdiag = _gla_tables()
    w2 = jnp.zeros((LANES, GLA_QK), F32).at[:GLA_RANK].set(gate_w2).astype(BF16)
    row = lambda width: pl.BlockSpec((B, Q, width), lambda c: (0, c, 0))
    const = lambda shape: pl.BlockSpec(shape, lambda c: (0,) * len(shape))
    return pl.pallas_call(
        _gla_kernel,
        grid=(L // Q,),
        in_specs=[row(GLA_QK), row(GLA_QK), row(GLA_V), row(LANES), row(GLA_V),
                  const((LANES, GLA_QK)), const((1, GLA_QK)), const(coef2.shape),
                  const(masks.shape), const(bdiag.shape), const((1, GLA_DV))],
        out_specs=row(GLA_V),
        out_shape=jax.ShapeDtypeStruct((B, L, GLA_V), BF16),
        scratch_shapes=[pltpu.VMEM((B, GLA_V, GLA_QK), F32)],
        compiler_params=_params(("arbitrary",)),
        name="gla_scan",
    )(gq, gk, gv, glow, gr, w2, gate_b.reshape(1, -1), coef2, masks, bdiag,
      norm_w.reshape(1, -1))


def _block(n, want):
    return want if n % want == 0 else n


def kernel(x, norm_mix, norm_mlp, norm_final, w_in_ab, fox_f_bias, ssd_conv_w, ssd_conv_b,
           ssd_dt_bias, ssd_a_log, ssd_d, ssd_norm, w_out_ab, w_in_cd, gla_gate_w2,
           gla_gate_b, gla_norm, w_out_cd, w_mlp_up, w_mlp_down):
    B, L, D = x.shape
    assert L % CHUNK == 0
    tm = _block(L, 512)
    tq = _block(L, 256)
    tq_fox = _block(L, 512)

    assert tm == tq_fox
    qd, kd, vd, qa, ka, z, xbc, dt, fend = _inproj_ab(x, norm_mix[0], w_in_ab[0], fox_f_bias[0],
                                                      ssd_dt_bias[0], tm)
    y_fox = _fox_attention(qd, qa, kd, ka, vd, fend.reshape(B, L // tm, LANES), tq_fox)
    y_ssd = _ssd(xbc, z, dt, ssd_conv_w[0], ssd_conv_b[0], ssd_a_log[0], ssd_d[0], ssd_norm[0])
    w_up = w_mlp_up.astype(BF16)
    w_down = w_mlp_down.astype(BF16)
    h = _mix_mlp(x, y_fox, y_ssd, w_out_ab[0], norm_mlp[0], w_up, w_down, 0, norm_final, tm,
                 final=False)

    sq, sk, sv, gq, gk, gv, gr, glow = _inproj_cd(h, norm_mix[1], w_in_cd[0], tm)
    y_sb = _sb_attention(sq, sk, sv, tq)
    y_gla = _gla(gq, gk, gv, glow, gr, gla_gate_w2[0], gla_gate_b[0], gla_norm[0])
    return _mix_mlp(h, y_sb, y_gla, w_out_cd[0], norm_mlp[1], w_up, w_down, 1, norm_final, tm,
                    final=True)
```

```python
import functools
import math

import numpy as np
import jax
import jax.numpy as jnp
from jax import lax
from jax.experimental import pallas as pl
from jax.experimental.pallas import tpu as pltpu

F32 = jnp.float32
BF16 = jnp.bfloat16

LANES = 128
SUBLANES = 8
VMEM_LIMIT_BYTES = 56 * 1024 * 1024

HEAD_DIM = 64
N_ATT_HEADS = 8
ATT_W = N_ATT_HEADS * HEAD_DIM
SSD_HEADS = 8
SSD_W = 512
SSD_GROUPS = 2
SSD_STATE = 64
SSD_CONV = 4
SSD_CONV_DIM = SSD_W + 2 * SSD_GROUPS * SSD_STATE
GLA_HEADS = 4
GLA_DK = 64
GLA_DV = 128
GLA_RANK = 16
GLA_GATE_NORM = 16.0
EPS = 1e-5
CHUNK = 128
GLA_LEVELS = 7
LOG2E = math.log2(math.e)
DT_LANE0 = 8


def _params(sem):
    return pltpu.CompilerParams(dimension_semantics=sem, vmem_limit_bytes=VMEM_LIMIT_BYTES)


def _split3(x):
    hi = x.astype(BF16)
    r = x - hi.astype(F32)
    mid = r.astype(BF16)
    lo = (r - mid.astype(F32)).astype(BF16)
    return hi, mid, lo


def _split2(x):
    hi = x.astype(BF16)
    lo = (x - hi.astype(F32)).astype(BF16)
    return hi, lo


def _softplus(x):
    return jnp.maximum(x, 0.0) + jnp.log1p(jnp.exp(-jnp.abs(x)))


def _log_sigmoid(x):
    return jnp.minimum(x, 0.0) - jnp.log1p(jnp.exp(-jnp.abs(x)))


def _silu(x):
    return x / (1.0 + jnp.exp(-x))


def _rms(x, w):
    ms = jnp.mean(x * x, axis=-1, keepdims=True)
    return x * lax.rsqrt(ms + EPS) * w


def _dot(a, b):
    return jnp.dot(a, b, preferred_element_type=F32)


def _dot_nt(a, b):
    return lax.dot_general(a, b, (((1,), (1,)), ((), ())), preferred_element_type=F32)


AB_COLS = 3 * ATT_W + SSD_W + SSD_CONV_DIM + LANES


def _inproj_ab_kernel(x_ref, nw_ref, w_ref, sb_ref, tri3_ref, sel_ref, cst_ref,
                      qd_ref, kd_ref, vd_ref, qa_ref, ka_ref, z_ref, xbc_ref, dt_ref, fend_ref,
                      carry_ref):
    @pl.when(pl.program_id(1) == 0)
    def _():
        carry_ref[...] = jnp.zeros_like(carry_ref)

    u = _rms(x_ref[0], nw_ref[...]).astype(BF16)

    def mm(a, b):
        return _dot(u, w_ref[:, a:b])

    o = 0
    qd_ref[0] = (mm(o, o + ATT_W) * (LOG2E * HEAD_DIM ** -0.5)).astype(BF16); o += ATT_W
    kd_ref[0] = mm(o, o + ATT_W).astype(BF16); o += ATT_W
    vd_ref[0] = mm(o, o + ATT_W).astype(BF16); o += ATT_W
    z_ref[0] = mm(o, o + SSD_W); o += SSD_W
    xbc_ref[0] = mm(o, o + SSD_CONV_DIM); o += SSD_CONV_DIM
    small = mm(o, o + LANES) + sb_ref[...]

    lane = lax.broadcasted_iota(jnp.int32, small.shape, 1)
    is_f = lane < N_ATT_HEADS
    is_dt = (lane >= DT_LANE0) & (lane < DT_LANE0 + SSD_HEADS)
    log_f = jnp.where(is_f, _log_sigmoid(small), 0.0)
    dt_ref[0] = jnp.where(is_dt, _softplus(small), 0.0)

    hi, mid, lo = _split3(log_f)
    cum = _dot(tri3_ref[...], jnp.concatenate([hi, mid, lo], axis=0)) + carry_ref[...]
    tm = cum.shape[0]
    carry_ref[...] = cum[tm - 1:tm, :]

    cum2 = cum * LOG2E
    fend_ref[0, 0] = cum2[tm - 1:tm, :]
    fh, fm, fl = _split3(cum2)
    fcat = (fh.astype(F32) + pltpu.roll(fm.astype(F32), N_ATT_HEADS, 1)
            + pltpu.roll(fl.astype(F32), 2 * N_ATT_HEADS, 1)).astype(BF16)
    aug = _dot(fcat, sel_ref[...]) + cst_ref[...]
    qa_ref[0] = aug[:, :ATT_W].astype(BF16)
    ka_ref[0] = aug[:, ATT_W:].astype(BF16)


def _aug_tables():
    sel = np.zeros((LANES, 2 * ATT_W), np.float32)
    cst = np.zeros((1, 2 * ATT_W), np.float32)
    for h in range(N_ATT_HEADS):
        p, e = divmod(h, 2)
        for part in range(3):
            src = part * N_ATT_HEADS + h
            sel[src, p * LANES + 6 * e + part] = 1.0
            sel[src, ATT_W + p * LANES + 6 * e + 3 + part] = -1.0
            cst[0, p * LANES + 6 * e + 3 + part] = 1.0
            cst[0, ATT_W + p * LANES + 6 * e + part] = 1.0
    return jnp.asarray(sel, BF16), jnp.asarray(cst, F32)


def _tri_incl(n):
    r = np.arange(n)
    return (r[None, :] <= r[:, None]).astype(np.float32)


def _inproj_ab(h, nw, w_in, f_bias, dt_bias, tm):
    B, L, D = h.shape
    fq, fk, fv, fl, wz, wxbc, wdt = jnp.split(w_in, np.cumsum(
        [ATT_W, ATT_W, ATT_W, N_ATT_HEADS, SSD_W, SSD_CONV_DIM])[:].tolist(), axis=1)
    pad = jnp.zeros((D, LANES - N_ATT_HEADS - SSD_HEADS), w_in.dtype)
    w = jnp.concatenate([fq, fk, fv, wz, wxbc, fl, wdt, pad], axis=1).astype(BF16)
    sb = jnp.concatenate([f_bias, dt_bias, jnp.zeros((LANES - 16,), F32)]).reshape(1, LANES)
    tri = _tri_incl(tm)
    tri3 = jnp.asarray(np.concatenate([tri, tri, tri], axis=1), BF16)
    sel, cst = _aug_tables()

    row = lambda width: pl.BlockSpec((1, tm, width), lambda b, i: (b, i, 0))
    const = lambda shape: pl.BlockSpec(shape, lambda b, i: (0,) * len(shape))
    outs = [
        jax.ShapeDtypeStruct((B, L, ATT_W), BF16),
        jax.ShapeDtypeStruct((B, L, ATT_W), BF16),
        jax.ShapeDtypeStruct((B, L, ATT_W), BF16),
        jax.ShapeDtypeStruct((B, L, ATT_W), BF16),
        jax.ShapeDtypeStruct((B, L, ATT_W), BF16),
        jax.ShapeDtypeStruct((B, L, SSD_W), F32),
        jax.ShapeDtypeStruct((B, L, SSD_CONV_DIM), F32),
        jax.ShapeDtypeStruct((B, L, LANES), F32),
        jax.ShapeDtypeStruct((B, L // tm, 1, LANES), F32),
    ]
    return pl.pallas_call(
        _inproj_ab_kernel,
        grid=(B, L // tm),
        in_specs=[row(D), const((1, D)), const((D, AB_COLS)), const((1, LANES)),
                  const((tm, 3 * tm)), const((LANES, 2 * ATT_W)), const((1, 2 * ATT_W))],
        out_specs=[row(ATT_W)] * 5 + [row(SSD_W), row(SSD_CONV_DIM), row(LANES),
                   pl.BlockSpec((1, 1, 1, LANES), lambda b, i: (b, i, 0, 0))],
        out_shape=outs,
        scratch_shapes=[pltpu.VMEM((1, LANES), F32)],
        compiler_params=_params(("arbitrary", "arbitrary")),
        name="inproj_ab",
    )(h, nw.reshape(1, D), w, sb, tri3, sel, cst)


def _head_lane_mask(e, shape):
    lane = lax.broadcasted_iota(jnp.int32, shape, 1)
    return (lane < HEAD_DIM) if e == 0 else (lane >= HEAD_DIM)


EXP2_DEAD = -151.0
FOX_STRIP = 64


def _fox_kernel(qd_ref, qa_ref, kd_ref, ka_ref, v_ref, fend_ref, o_ref,
                q_ref, s0_ref, s1_ref, p0_ref, p1_ref, m_ref, al_ref, acc_ref, kn_ref, *, tq):
    i = pl.program_id(2)
    tk = tq
    R = 2 * tq
    qd = qd_ref[0].astype(F32)
    qa = qa_ref[0].astype(F32)
    lane = lax.broadcasted_iota(jnp.int32, (tq, LANES), 1)
    first = lane < HEAD_DIM
    for e in range(2):
        amask = (lane >= 6 * e) & (lane < 6 * e + 6)
        q_ref[e * tq:(e + 1) * tq, :] = jnp.concatenate(
            [jnp.where(first if e == 0 else ~first, qd, 0.0), jnp.where(amask, qa, 0.0)],
            axis=1).astype(BF16)
    m_ref[...] = jnp.full(m_ref.shape, -jnp.inf, F32)
    acc_ref[...] = jnp.zeros(acc_ref.shape, F32)
    ones = jnp.ones((tk, LANES), BF16)

    def head_sq_norms(x):
        sq = x * x
        return (jnp.sum(jnp.where(first, sq, 0.0), axis=1, keepdims=True),
                jnp.sum(jnp.where(first, 0.0, sq), axis=1, keepdims=True))

    @pl.when(i == 0)
    def _():
        def kbody(c, carry):
            ks = pl.multiple_of(c * tk, tk)
            n0, n1 = head_sq_norms(kd_ref[0, pl.ds(ks, tk), :].astype(F32))
            return jnp.maximum(carry[0], n0), jnp.maximum(carry[1], n1)
        zero = jnp.zeros((tk, 1), F32)
        n0, n1 = lax.fori_loop(0, kd_ref.shape[1] // tk, kbody, (zero, zero))
        for e, n in enumerate((n0, n1)):
            kn_ref[e] = jnp.broadcast_to(jnp.sqrt(jnp.max(n, axis=0, keepdims=True)), kn_ref.shape[1:])

    def scores(j, s_ref):
        ks = pl.multiple_of(j * tk, tk)
        k = jnp.concatenate([kd_ref[0, pl.ds(ks, tk), :], ka_ref[0, pl.ds(ks, tk), :]], axis=1)
        s_ref[...] = _dot_nt(q_ref[...], k)

    def update(j, s_ref, p_ref, masked):
        for r0 in range(0, R, FOX_STRIP):
            rs = slice(r0, r0 + FOX_STRIP)
            s = s_ref[rs, :]
            if masked:
                rows = lax.broadcasted_iota(jnp.int32, s.shape, 0) + (r0 % tq)
                cols = lax.broadcasted_iota(jnp.int32, s.shape, 1)
                s = jnp.where(rows >= cols, s, -jnp.inf)
            m_old = m_ref[rs, :]
            m_new = jnp.maximum(m_old, jnp.max(s, axis=1, keepdims=True))
            al_ref[rs, :] = jnp.exp2(m_old - m_new)
            m_ref[rs, :] = m_new
            p_ref[rs, :] = jnp.exp2(s - m_new).astype(BF16)
        ks = pl.multiple_of(j * tk, tk)
        v1 = jnp.concatenate([v_ref[0, pl.ds(ks, tk), :], ones], axis=1)
        acc_ref[...] = al_ref[...] * acc_ref[...] + _dot(p_ref[...], v1)

    scores(i, s0_ref)
    scores(jnp.maximum(i - 1, 0), s1_ref)
    update(i, s0_ref, p0_ref, True)

    qn = head_sq_norms(qd)
    fend = fend_ref[0]
    hl = lax.broadcasted_iota(jnp.int32, fend.shape, 1)
    jcol = lax.broadcasted_iota(jnp.int32, (fend.shape[0], 1), 0)
    jmin = i
    for e in range(2):
        head = 2 * pl.program_id(1) + e
        fe = jnp.sum(jnp.where(hl == head, fend, 0.0), axis=1, keepdims=True)
        f_top = jnp.sum(jnp.where(jcol == i - 1, fe, 0.0), axis=0, keepdims=True)
        qk = jnp.sqrt(jnp.max(qn[e], axis=0, keepdims=True)) * kn_ref[e, 0:1, 0:1]
        m_min = jnp.min(m_ref[e * tq:(e + 1) * tq, :], axis=0, keepdims=True)
        live = (jcol < i) & (qk + f_top - fe + 1.0 - m_min > EXP2_DEAD)
        jmin = jnp.minimum(jmin, jnp.min(jnp.where(live, jcol, i)))
    nb = i - jmin

    def pair(u, carry):
        j = i - 1 - 2 * u
        scores(j - 1, s0_ref)
        update(j, s1_ref, p1_ref, False)
        scores(jnp.maximum(j - 2, 0), s1_ref)
        update(j - 1, s0_ref, p0_ref, False)
        return carry

    lax.fori_loop(0, nb // 2, pair, 0)

    @pl.when(nb % 2 == 1)
    def _():
        update(jmin, s1_ref, p1_ref, False)

    acc = acc_ref[...]
    out = acc[:, :LANES] / acc[:, LANES:]
    o_ref[0] = jnp.where(first, out[:tq], out[tq:]).astype(BF16)


def _fox_attention(qd, qa, kd, ka, v, fend, tq):
    B, L, _ = qd.shape
    assert fend.shape == (B, L // tq, LANES)
    n_pairs = N_ATT_HEADS // 2
    R = 2 * tq
    qspec = pl.BlockSpec((1, tq, LANES), lambda b, p, i: (b, i, p))
    kspec = pl.BlockSpec((1, L, LANES), lambda b, p, i: (b, 0, p))
    return pl.pallas_call(
        functools.partial(_fox_kernel, tq=tq),
        grid=(B, n_pairs, L // tq),
        in_specs=[qspec, qspec, kspec, kspec, kspec,
                  pl.BlockSpec((1, L // tq, LANES), lambda b, p, i: (b, 0, 0))],
        out_specs=qspec,
        out_shape=jax.ShapeDtypeStruct((B, L, ATT_W), BF16),
        scratch_shapes=[pltpu.VMEM((R, 2 * LANES), BF16),
                        pltpu.VMEM((R, tq), F32), pltpu.VMEM((R, tq), F32),
                        pltpu.VMEM((R, tq), BF16), pltpu.VMEM((R, tq), BF16),
                        pltpu.VMEM((R, 1), F32), pltpu.VMEM((R, 1), F32),
                        pltpu.VMEM((R, 2 * LANES), F32),
                        pltpu.VMEM((2, SUBLANES, LANES), F32)],
        compiler_params=_params(("arbitrary", "arbitrary", "arbitrary")),
        name="fox_attention",
    )(qd, qa, kd, ka, v, fend)


def _ssd_kernel(xbc_ref, z_ref, dt_ref, cw_ref, cb_ref, alog_ref, dsk_ref, nw_ref, tri3_ref,
                y_ref, xext_ref, st_ref):
    @pl.when(pl.program_id(0) == 0)
    def _():
        xext_ref[:, 0:SUBLANES, :] = jnp.zeros((xext_ref.shape[0], SUBLANES, SSD_CONV_DIM), F32)
        st_ref[...] = jnp.zeros_like(st_ref)

    for b in range(xbc_ref.shape[0]):
        _ssd_chunk(xbc_ref.at[b], z_ref.at[b], dt_ref.at[b], cw_ref, cb_ref, alog_ref, dsk_ref,
                   nw_ref, tri3_ref, y_ref.at[b], xext_ref.at[b], st_ref.at[b])


def _ssd_chunk(xbc_ref, z_ref, dt_ref, cw_ref, cb_ref, alog_ref, dsk_ref, nw_ref, tri3_ref,
               y_ref, xext_ref, st_ref):
    Q = CHUNK
    xext_ref[SUBLANES:SUBLANES + Q, :] = xbc_ref[...]
    conv = cb_ref[...]
    for k in range(SSD_CONV):
        conv = conv + cw_ref[k:k + 1, :] * xext_ref[pl.ds(SUBLANES - (SSD_CONV - 1) + k, Q), :]
    xext_ref[0:SUBLANES, :] = xext_ref[Q:Q + SUBLANES, :]
    xc = _silu(conv)
    xs = xc[:, :SSD_W]
    bm = xc[:, SSD_W:SSD_W + LANES]
    cm = xc[:, SSD_W + LANES:]

    dt = dt_ref[...]
    a = dt * (-jnp.exp(alog_ref[...]))
    hi, mid, lo = _split3(a)
    a_cum = _dot(tri3_ref[...], jnp.concatenate([hi, mid, lo], axis=0))
    a_row = a_cum.T
    a_last_col = a_cum[Q - 1:Q, :]
    bm_t = bm.T

    rows = lax.broadcasted_iota(jnp.int32, (Q, Q), 0)
    cols = lax.broadcasted_iota(jnp.int32, (Q, Q), 1)
    causal = rows >= cols
    lane = lax.broadcasted_iota(jnp.int32, (Q, LANES), 1)
    first_half = lane < HEAD_DIM

    scores = []
    cmask = []
    for g in range(SSD_GROUPS):
        cg = jnp.where(_head_lane_mask(g, (Q, LANES)), cm, 0.0).astype(BF16)
        cmask.append(cg)
        scores.append(_dot_nt(cg, bm.astype(BF16)))

    y_pairs = []
    heads_per_group = SSD_HEADS // SSD_GROUPS
    for p in range(SSD_HEADS // 2):
        xs_pair = xs[:, p * LANES:(p + 1) * LANES]
        dt_pair = jnp.where(first_half, dt[:, DT_LANE0 + 2 * p:DT_LANE0 + 2 * p + 1],
                            dt[:, DT_LANE0 + 2 * p + 1:DT_LANE0 + 2 * p + 2])
        xdt = (xs_pair * dt_pair).astype(BF16)
        y_head = []
        for e in range(2):
            h = 2 * p + e
            g = h // heads_per_group
            hl = DT_LANE0 + h
            acol = a_cum[:, hl:hl + 1]
            arow = a_row[hl:hl + 1, :]
            alast = a_last_col[:, hl:hl + 1]
            lmat = jnp.exp(jnp.where(causal, acol - arow, -jnp.inf))
            y_diag = _dot((scores[g] * lmat).astype(BF16), xdt)
            prev = st_ref[h]
            y_off = _dot(cmask[g], prev.astype(BF16)) * jnp.exp(acol)
            y_head.append(y_diag + y_off)
            decay_row = jnp.exp(alast - arow)
            local = _dot((bm_t * decay_row).astype(BF16), xdt)
            st_ref[h] = prev * jnp.exp(alast) + local
        y_pairs.append(jnp.where(first_half, y_head[0], y_head[1]))
    y = jnp.concatenate(y_pairs, axis=1) + dsk_ref[...] * xs
    y = y * _silu(z_ref[...])
    gw = SSD_W // SSD_GROUPS
    y = jnp.concatenate([_rms(y[:, g * gw:(g + 1) * gw], nw_ref[:, g * gw:(g + 1) * gw])
                         for g in range(SSD_GROUPS)], axis=1)
    y_ref[...] = y.astype(BF16)


def _ssd(xbc, z, dt, conv_w, conv_b, a_log, d_skip, norm_w):
    B, L, _ = xbc.shape
    Q = CHUNK
    alog = jnp.zeros((1, LANES), F32).at[0, DT_LANE0:DT_LANE0 + SSD_HEADS].set(a_log)
    dsk = jnp.repeat(d_skip, SSD_W // SSD_HEADS).reshape(1, SSD_W)
    tri = _tri_incl(Q)
    tri3 = jnp.asarray(np.concatenate([tri, tri, tri], axis=1), BF16)
    row = lambda width: pl.BlockSpec((B, Q, width), lambda c: (0, c, 0))
    const = lambda shape: pl.BlockSpec(shape, lambda c: (0,) * len(shape))
    return pl.pallas_call(
        _ssd_kernel,
        grid=(L // Q,),
        in_specs=[row(SSD_CONV_DIM), row(SSD_W), row(LANES), const((SSD_CONV, SSD_CONV_DIM)),
                  const((1, SSD_CONV_DIM)), const((1, LANES)), const((1, SSD_W)),
                  const((1, SSD_W)), const((Q, 3 * Q))],
        out_specs=row(SSD_W),
        out_shape=jax.ShapeDtypeStruct((B, L, SSD_W), BF16),
        scratch_shapes=[pltpu.VMEM((B, Q + SUBLANES, SSD_CONV_DIM), F32),
                        pltpu.VMEM((B, SSD_HEADS, LANES, LANES), F32)],
        compiler_params=_params(("arbitrary",)),
        name="ssd_scan",
    )(xbc, z, dt, conv_w, conv_b.reshape(1, -1), alog, dsk, norm_w.reshape(1, -1), tri3)


def _mix_mlp_kernel(h_ref, y1_ref, y2_ref, wo_ref, nw_ref, wup_ref, wdn_ref, nf_ref, o_ref,
                    *, ff_chunk, final):
    half = y1_ref.shape[-1]
    mix = _dot(y1_ref[...], wo_ref[0:half, :]) + _dot(y2_ref[...], wo_ref[half:2 * half, :])
    h1 = h_ref[...] + mix
    u = _rms(h1, nw_ref[...]).astype(BF16)
    d_ff = wup_ref.shape[1]
    acc = jnp.zeros_like(h1)
    for c in range(d_ff // ff_chunk):
        act = jnp.maximum(_dot(u, wup_ref[:, c * ff_chunk:(c + 1) * ff_chunk]), 0.0)
        acc = acc + _dot((act * act).astype(BF16), wdn_ref[c * ff_chunk:(c + 1) * ff_chunk, :])
    h2 = h1 + acc
    if final:
        h2 = _rms(h2, nf_ref[...])
    o_ref[...] = h2


def _mix_mlp(h, y1, y2, w_out, nw, w_up, w_down, layer, nf, tm, final):
    B, L, D = h.shape
    T = B * L
    d_ff = w_up.shape[2]
    half = y1.shape[-1]
    row = lambda width: pl.BlockSpec((tm, width), lambda i: (i, 0))
    const = lambda shape: pl.BlockSpec(shape, lambda i: (0,) * len(shape))
    stacked = lambda shape: pl.BlockSpec((None,) + shape, lambda i: (layer, 0, 0))
    out = pl.pallas_call(
        functools.partial(_mix_mlp_kernel, ff_chunk=min(1024, d_ff), final=final),
        grid=(T // tm,),
        in_specs=[row(D), row(half), row(half), const((2 * half, D)), const((1, D)),
                  stacked((D, d_ff)), stacked((d_ff, D)), const((1, D))],
        out_specs=row(D),
        out_shape=jax.ShapeDtypeStruct((T, D), F32),
        compiler_params=_params(("arbitrary",)),
        name="mix_mlp",
    )(h.reshape(T, D), y1.reshape(T, half), y2.reshape(T, half), w_out.astype(BF16),
      nw.reshape(1, D), w_up, w_down, nf.reshape(1, D))
    return out.reshape(B, L, D)


GLA_QK = GLA_HEADS * GLA_DK
GLA_V = GLA_HEADS * GLA_DV
CD_COLS = 3 * ATT_W + 2 * GLA_QK + 2 * GLA_V + LANES


def _inproj_cd_kernel(x_ref, nw_ref, w_ref, sq_ref, sk_ref, sv_ref, gq_ref, gk_ref, gv_ref,
                      gr_ref, glow_ref):
    u = _rms(x_ref[...], nw_ref[...]).astype(BF16)

    def mm(a, b):
        return _dot(u, w_ref[:, a:b])

    o = 0
    sq_ref[...] = (mm(o, o + ATT_W) * (LOG2E * HEAD_DIM ** -0.5)).astype(BF16); o += ATT_W
    sk_ref[...] = mm(o, o + ATT_W).astype(BF16); o += ATT_W
    sv_ref[...] = mm(o, o + ATT_W).astype(BF16); o += ATT_W
    gq_ref[...] = mm(o, o + GLA_QK) * (GLA_DK ** -0.5); o += GLA_QK
    gk_ref[...] = mm(o, o + GLA_QK); o += GLA_QK
    gv_ref[...] = mm(o, o + GLA_V); o += GLA_V
    gr_ref[...] = mm(o, o + GLA_V); o += GLA_V
    glow_ref[...] = mm(o, o + LANES)


def _inproj_cd(h, nw, w_in, tm):
    B, L, D = h.shape
    T = B * L
    sq, sk, sv, gq, gk, gv, glow, gr = jnp.split(w_in, np.cumsum(
        [ATT_W, ATT_W, ATT_W, GLA_QK, GLA_QK, GLA_V, GLA_RANK]).tolist(), axis=1)
    pad = jnp.zeros((D, LANES - GLA_RANK), w_in.dtype)
    w = jnp.concatenate([sq, sk, sv, gq, gk, gv, gr, glow, pad], axis=1).astype(BF16)
    row = lambda width: pl.BlockSpec((tm, width), lambda i: (i, 0))
    const = lambda shape: pl.BlockSpec(shape, lambda i: (0,) * len(shape))
    widths = [ATT_W, ATT_W, ATT_W, GLA_QK, GLA_QK, GLA_V, GLA_V, LANES]
    dtypes = [BF16, BF16, BF16, F32, F32, F32, F32, F32]
    outs = pl.pallas_call(
        _inproj_cd_kernel,
        grid=(T // tm,),
        in_specs=[row(D), const((1, D)), const((D, CD_COLS))],
        out_specs=[row(wd) for wd in widths],
        out_shape=[jax.ShapeDtypeStruct((T, wd), dt) for wd, dt in zip(widths, dtypes)],
        compiler_params=_params(("arbitrary",)),
        name="inproj_cd",
    )(h.reshape(T, D), nw.reshape(1, D), w)
    return [o.reshape(B, L, -1) for o in outs]


SB_DEAD = EXP2_DEAD - 1.0
SB_STRIP = 64
SB_CHAINS = 2


def _sb_kernel(q_ref, k_ref, v_ref, u2_ref, o_ref,
               qs_ref, z_ref, sp_ref, in_ref, a_ref, c_ref, acc_ref, *, tq):
    i = pl.program_id(2)
    tk = tq
    R = 2 * tq
    chains = range(SB_CHAINS)
    for ch in chains:
        qf = q_ref[0, :, ch * LANES:(ch + 1) * LANES].astype(F32)
        for e in range(2):
            qs_ref[ch, e * tq:(e + 1) * tq, :] = jnp.where(_head_lane_mask(e, (tq, LANES)), qf,
                                                           0.0).astype(BF16)
    c_ref[...] = jnp.zeros(c_ref.shape, F32)
    acc_ref[...] = jnp.zeros(acc_ref.shape, F32)

    def strict_mask(r0, shape):
        rows = lax.broadcasted_iota(jnp.int32, shape, 0) + (r0 % tq)
        return lax.broadcasted_iota(jnp.int32, shape, 1) < rows

    def step(j, masked):
        ks = pl.multiple_of(j * tk, tk)
        for ch in chains:
            z_ref[ch] = _dot_nt(qs_ref[ch], k_ref[0, pl.ds(ks, tk), ch * LANES:(ch + 1) * LANES])
        for ch in chains:
            for r0 in range(0, R, SB_STRIP):
                rs = slice(r0, r0 + SB_STRIP)
                z = z_ref[ch, rs, :]
                sp = jnp.maximum(z, 0.0) + jnp.log2(1.0 + jnp.exp2(-jnp.abs(z)))
                if masked:
                    sp = jnp.where(strict_mask(r0, sp.shape), sp, 0.0)
                sp_ref[ch, rs, :] = sp.astype(BF16)
        for ch in chains:
            in_ref[ch] = _dot(sp_ref[ch], u2_ref[...])
        for ch in chains:
            for r0 in range(0, R, SB_STRIP):
                rs = slice(r0, r0 + SB_STRIP)
                inc = in_ref[ch, rs, :]
                a = jnp.exp2(z_ref[ch, rs, :] - c_ref[ch, rs, :] - inc)
                if masked:
                    a = jnp.where(strict_mask(r0, a.shape), a, 0.0)
                a_ref[ch, rs, :] = a.astype(BF16)
                c_ref[ch, rs, :] += inc[:, 0:1]
        for ch in chains:
            acc_ref[ch] += _dot(a_ref[ch], v_ref[0, pl.ds(ks, tk), ch * LANES:(ch + 1) * LANES])
        return jnp.min(c_ref[...])

    cmin = step(i, True)

    def cond(carry):
        j, cmin = carry
        return (j >= 0) & (cmin < -SB_DEAD)

    def body(carry):
        j, _ = carry
        return j - 1, step(j, False)

    lax.while_loop(cond, body, (i - 1, cmin))
    for ch in chains:
        acc = acc_ref[ch]
        o_ref[0, :, ch * LANES:(ch + 1) * LANES] = jnp.where(
            _head_lane_mask(0, (tq, LANES)), acc[:tq], acc[tq:]).astype(BF16)


def _sb_attention(q, k, v, tq):
    B, L, _ = q.shape
    n_groups = N_ATT_HEADS // (2 * SB_CHAINS)
    R = 2 * tq
    W = SB_CHAINS * LANES
    r = np.arange(tq)
    u = (r[:, None] >= r[None, :]).astype(np.float32)
    u2 = jnp.asarray(u, BF16)
    qspec = pl.BlockSpec((1, tq, W), lambda b, p, i: (b, i, p))
    kspec = pl.BlockSpec((1, L, W), lambda b, p, i: (b, 0, p))
    return pl.pallas_call(
        functools.partial(_sb_kernel, tq=tq),
        grid=(B, n_groups, L // tq),
        in_specs=[qspec, kspec, kspec, pl.BlockSpec((tq, tq), lambda b, p, i: (0, 0))],
        out_specs=qspec,
        out_shape=jax.ShapeDtypeStruct((B, L, ATT_W), BF16),
        scratch_shapes=[pltpu.VMEM((SB_CHAINS, R, LANES), BF16), pltpu.VMEM((SB_CHAINS, R, tq), F32),
                        pltpu.VMEM((SB_CHAINS, R, tq), BF16),
                        pltpu.VMEM((SB_CHAINS, R, tq), F32),
                        pltpu.VMEM((SB_CHAINS, R, tq), BF16), pltpu.VMEM((SB_CHAINS, R, 1), F32),
                        pltpu.VMEM((SB_CHAINS, R, LANES), F32)],
        compiler_params=_params(("arbitrary", "arbitrary", "arbitrary")),
        name="sb_attention",
    )(q, k, v, u2)


def _gla_tables():
    Q = CHUNK
    r = np.arange(Q)
    j = np.arange(Q)
    coef = [(j[None, :] <= r[:, None]), (j[None, :] > r[:, None])]
    masks = [np.eye(Q, dtype=bool)]
    for lvl in range(GLA_LEVELS):
        m = 1 << lvl
        c0 = (r // (2 * m)) * (2 * m)
        mid = c0 + m - 1
        upper = (r - c0) >= m
        up = (j[None, :] > mid[:, None]) & (j[None, :] <= r[:, None])
        lowr = (j[None, :] > r[:, None]) & (j[None, :] <= mid[:, None])
        coef.append(np.where(upper[:, None], up, lowr))
        masks.append((c0[:, None] == c0[None, :]) & upper[:, None] & (~upper)[None, :])
    coef = np.concatenate(coef, axis=0).astype(np.float32)
    coef2 = np.concatenate([coef, coef], axis=1)
    masks = np.stack(masks).astype(np.float32)
    hv = np.arange(GLA_V) // GLA_DV
    hk = np.arange(GLA_QK) // GLA_DK
    bdiag = (hv[:, None] == hk[None, :]).astype(np.float32)
    return jnp.asarray(coef2, BF16), jnp.asarray(masks, F32), jnp.asarray(bdiag, F32)


def _gla_kernel(gq_ref, gk_ref, gv_ref, glow_ref, gr_ref, w2_ref, gb_ref, coef_ref, mask_ref,
                bdiag_ref, nw_ref, o_ref, st_ref):
    @pl.when(pl.program_id(0) == 0)
    def _():
        st_ref[...] = jnp.zeros_like(st_ref)

    for b in range(gq_ref.shape[0]):
        _gla_chunk(gq_ref.at[b], gk_ref.at[b], gv_ref.at[b], glow_ref.at[b], gr_ref.at[b], w2_ref,
                   gb_ref, coef_ref, mask_ref, bdiag_ref, nw_ref, o_ref.at[b], st_ref.at[b])


def _gla_chunk(gq_ref, gk_ref, gv_ref, glow_ref, gr_ref, w2_ref, gb_ref, coef_ref, mask_ref,
               bdiag_ref, nw_ref, o_ref, st_ref):
    Q = CHUNK
    logits = _dot(glow_ref[...].astype(BF16), w2_ref[...]) + gb_ref[...]
    la = _log_sigmoid(logits) * (1.0 / GLA_GATE_NORM)
    hi, lo = _split2(la)
    expo = _dot(coef_ref[...], jnp.concatenate([hi, lo], axis=0))
    q = gq_ref[...]
    k = gk_ref[...]
    v = gv_ref[...]
    lane_head = lax.broadcasted_iota(jnp.int32, (Q, GLA_QK), 1) // GLA_DK
    row = lax.broadcasted_iota(jnp.int32, (Q, GLA_QK), 0)
    hmask = [lane_head == h for h in range(GLA_HEADS)]

    att = [None] * GLA_HEADS
    for lvl in range(-1, GLA_LEVELS):
        if lvl < 0:
            xq, xk = q, k.astype(BF16)
        else:
            m = 1 << lvl
            upper = (row & (2 * m - 1)) >= m
            xq = jnp.where(upper, q, k) * jnp.exp(expo[(2 + lvl) * Q:(3 + lvl) * Q, :])
            xk = xq.astype(BF16)
        msk = mask_ref[lvl + 1]
        for h in range(GLA_HEADS):
            part = _dot_nt(jnp.where(hmask[h], xq, 0.0).astype(BF16), xk) * msk
            att[h] = part if att[h] is None else att[h] + part

    st = st_ref[...]
    q_in = (q * jnp.exp(expo[0:Q, :])).astype(BF16)
    o = _dot_nt(q_in, st.astype(BF16))
    o_intra = [_dot(att[h].astype(BF16), v[:, h * GLA_DV:(h + 1) * GLA_DV].astype(BF16))
               for h in range(GLA_HEADS)]
    o = o + jnp.concatenate(o_intra, axis=1)

    k_dec = (k * jnp.exp(expo[Q:2 * Q, :])).astype(BF16)
    upd = _dot(v.T.astype(BF16), k_dec)
    g_last = expo[Q - 1:Q, :]
    st_ref[...] = st * jnp.exp(g_last) + upd * bdiag_ref[...]

    gr = gr_ref[...]
    o = jnp.concatenate([_rms(o[:, h * GLA_DV:(h + 1) * GLA_DV], nw_ref[...])
                         for h in range(GLA_HEADS)], axis=1)
    o_ref[...] = (o * _silu(gr)).astype(BF16)


def _gla(gq, gk, gv, glow, gr, gate_w2, gate_b, norm_w):
    B, L, _ = gq.shape
    Q = CHUNK
    coef2, masks, bdiag = _gla_tables()
    w2 = jnp.zeros((LANES, GLA_QK), F32).at[:GLA_RANK].set(gate_w2).astype(BF16)
    row = lambda width: pl.BlockSpec((B, Q, width), lambda c: (0, c, 0))
    const = lambda shape: pl.BlockSpec(shape, lambda c: (0,) * len(shape))
    return pl.pallas_call(
        _gla_kernel,
        grid=(L // Q,),
        in_specs=[row(GLA_QK), row(GLA_QK), row(GLA_V), row(LANES), row(GLA_V),
                  const((LANES, GLA_QK)), const((1, GLA_QK)), const(coef2.shape),
                  const(masks.shape), const(bdiag.shape), const((1, GLA_DV))],
        out_specs=row(GLA_V),
        out_shape=jax.ShapeDtypeStruct((B, L, GLA_V), BF16),
        scratch_shapes=[pltpu.VMEM((B, GLA_V, GLA_QK), F32)],
        compiler_params=_params(("arbitrary",)),
        name="gla_scan",
    )(gq, gk, gv, glow, gr, w2, gate_b.reshape(1, -1), coef2, masks, bdiag,
      norm_w.reshape(1, -1))


def _block(n, want):
    return want if n % want == 0 else n


def kernel(x, norm_mix, norm_mlp, norm_final, w_in_ab, fox_f_bias, ssd_conv_w, ssd_conv_b,
           ssd_dt_bias, ssd_a_log, ssd_d, ssd_norm, w_out_ab, w_in_cd, gla_gate_w2,
           gla_gate_b, gla_norm, w_out_cd, w_mlp_up, w_mlp_down):
    B, L, D = x.shape
    assert L % CHUNK == 0
    tm = _block(L, 512)
    tq = _block(L, 256)
    tq_fox = _block(L, 512)

    assert tm == tq_fox
    qd, kd, vd, qa, ka, z, xbc, dt, fend = _inproj_ab(x, norm_mix[0], w_in_ab[0], fox_f_bias[0],
                                                      ssd_dt_bias[0], tm)
    y_fox = _fox_attention(qd, qa, kd, ka, vd, fend.reshape(B, L // tm, LANES), tq_fox)
    y_ssd = _ssd(xbc, z, dt, ssd_conv_w[0], ssd_conv_b[0], ssd_a_log[0], ssd_d[0], ssd_norm[0])
    w_up = w_mlp_up.astype(BF16)
    w_down = w_mlp_down.astype(BF16)
    h = _mix_mlp(x, y_fox, y_ssd, w_out_ab[0], norm_mlp[0], w_up, w_down, 0, norm_final, tm,
                 final=False)

    sq, sk, sv, gq, gk, gv, gr, glow = _inproj_cd(h, norm_mix[1], w_in_cd[0], tm)
    y_sb = _sb_attention(sq, sk, sv, tq)
    y_gla = _gla(gq, gk, gv, glow, gr, gla_gate_w2[0], gla_gate_b[0], gla_norm[0])
    return _mix_mlp(h, y_sb, y_gla, w_out_cd[0], norm_mlp[1], w_up, w_down, 1, norm_final, tm,
                    final=True)
```

```python
import functools
import math

import numpy as np
import jax
import jax.numpy as jnp
from jax import lax
from jax.experimental import pallas as pl
from jax.experimental.pallas import tpu as pltpu

F32 = jnp.float32
BF16 = jnp.bfloat16

LANES = 128
SUBLANES = 8
VMEM_LIMIT_BYTES = 56 * 1024 * 1024

HEAD_DIM = 64
N_ATT_HEADS = 8
ATT_W = N_ATT_HEADS * HEAD_DIM
SSD_HEADS = 8
SSD_W = 512
SSD_GROUPS = 2
SSD_STATE = 64
SSD_CONV = 4
SSD_CONV_DIM = SSD_W + 2 * SSD_GROUPS * SSD_STATE
GLA_HEADS = 4
GLA_DK = 64
GLA_DV = 128
GLA_RANK = 16
GLA_GATE_NORM = 16.0
EPS = 1e-5
CHUNK = 128
GLA_LEVELS = 7
LOG2E = math.log2(math.e)
DT_LANE0 = 8


def _params(sem):
    return pltpu.CompilerParams(dimension_semantics=sem, vmem_limit_bytes=VMEM_LIMIT_BYTES)


def _split3(x):
    hi = x.astype(BF16)
    r = x - hi.astype(F32)
    mid = r.astype(BF16)
    lo = (r - mid.astype(F32)).astype(BF16)
    return hi, mid, lo


def _split2(x):
    hi = x.astype(BF16)
    lo = (x - hi.astype(F32)).astype(BF16)
    return hi, lo


def _softplus(x):
    return jnp.maximum(x, 0.0) + jnp.log1p(jnp.exp(-jnp.abs(x)))


def _log_sigmoid(x):
    return jnp.minimum(x, 0.0) - jnp.log1p(jnp.exp(-jnp.abs(x)))


def _silu(x):
    return x / (1.0 + jnp.exp(-x))


def _rms(x, w):
    ms = jnp.mean(x * x, axis=-1, keepdims=True)
    return x * lax.rsqrt(ms + EPS) * w


def _dot(a, b):
    return jnp.dot(a, b, preferred_element_type=F32)


def _dot_nt(a, b):
    return lax.dot_general(a, b, (((1,), (1,)), ((), ())), preferred_element_type=F32)


AB_COLS = 3 * ATT_W + SSD_W + SSD_CONV_DIM + LANES


def _inproj_ab_kernel(x_ref, nw_ref, w_ref, sb_ref, tri3_ref, sel_ref, cst_ref,
                      qd_ref, kd_ref, vd_ref, qa_ref, ka_ref, z_ref, xbc_ref, dt_ref, fend_ref,
                      carry_ref):
    @pl.when(pl.program_id(1) == 0)
    def _():
        carry_ref[...] = jnp.zeros_like(carry_ref)

    u = _rms(x_ref[0], nw_ref[...]).astype(BF16)

    def mm(a, b):
        return _dot(u, w_ref[:, a:b])

    o = 0
    qd_ref[0] = (mm(o, o + ATT_W) * (LOG2E * HEAD_DIM ** -0.5)).astype(BF16); o += ATT_W
    kd_ref[0] = mm(o, o + ATT_W).astype(BF16); o += ATT_W
    vd_ref[0] = mm(o, o + ATT_W).astype(BF16); o += ATT_W
    z_ref[0] = mm(o, o + SSD_W); o += SSD_W
    xbc_ref[0] = mm(o, o + SSD_CONV_DIM); o += SSD_CONV_DIM
    small = mm(o, o + LANES) + sb_ref[...]

    lane = lax.broadcasted_iota(jnp.int32, small.shape, 1)
    is_f = lane < N_ATT_HEADS
    is_dt = (lane >= DT_LANE0) & (lane < DT_LANE0 + SSD_HEADS)
    log_f = jnp.where(is_f, _log_sigmoid(small), 0.0)
    dt_ref[0] = jnp.where(is_dt, _softplus(small), 0.0)

    hi, mid, lo = _split3(log_f)
    tm = log_f.shape[0]
    carry = carry_ref[...]
    cums = []
    for r0 in range(0, tm, CHUNK):
        rs = slice(r0, r0 + CHUNK)
        piece = _dot(tri3_ref[...], jnp.concatenate([hi[rs], mid[rs], lo[rs]], axis=0)) + carry
        carry = piece[CHUNK - 1:CHUNK, :]
        cums.append(piece)
    cum = jnp.concatenate(cums, axis=0)
    carry_ref[...] = carry

    cum2 = cum * LOG2E
    fend_ref[0, 0] = cum2[tm - 1:tm, :]
    fh, fm, fl = _split3(cum2)
    fcat = (fh.astype(F32) + pltpu.roll(fm.astype(F32), N_ATT_HEADS, 1)
            + pltpu.roll(fl.astype(F32), 2 * N_ATT_HEADS, 1)).astype(BF16)
    aug = _dot(fcat, sel_ref[...]) + cst_ref[...]
    qa_ref[0] = aug[:, :ATT_W].astype(BF16)
    ka_ref[0] = aug[:, ATT_W:].astype(BF16)


def _aug_tables():
    sel = np.zeros((LANES, 2 * ATT_W), np.float32)
    cst = np.zeros((1, 2 * ATT_W), np.float32)
    for h in range(N_ATT_HEADS):
        p, e = divmod(h, 2)
        for part in range(3):
            src = part * N_ATT_HEADS + h
            sel[src, p * LANES + 6 * e + part] = 1.0
            sel[src, ATT_W + p * LANES + 6 * e + 3 + part] = -1.0
            cst[0, p * LANES + 6 * e + 3 + part] = 1.0
            cst[0, ATT_W + p * LANES + 6 * e + part] = 1.0
    return jnp.asarray(sel, BF16), jnp.asarray(cst, F32)


def _tri_incl(n):
    r = np.arange(n)
    return (r[None, :] <= r[:, None]).astype(np.float32)


def _inproj_ab(h, nw, w_in, f_bias, dt_bias, head_order, head_cols, tm):
    B, L, D = h.shape
    fq, fk, fv, fl, wz, wxbc, wdt = jnp.split(w_in, np.cumsum(
        [ATT_W, ATT_W, ATT_W, N_ATT_HEADS, SSD_W, SSD_CONV_DIM])[:].tolist(), axis=1)
    fq, fk, fv = (jnp.take(w, head_cols, axis=1) for w in (fq, fk, fv))
    fl = jnp.take(fl, head_order, axis=1)
    f_bias = f_bias[head_order]
    pad = jnp.zeros((D, LANES - N_ATT_HEADS - SSD_HEADS), w_in.dtype)
    w = jnp.concatenate([fq, fk, fv, wz, wxbc, fl, wdt, pad], axis=1).astype(BF16)
    sb = jnp.concatenate([f_bias, dt_bias, jnp.zeros((LANES - 16,), F32)]).reshape(1, LANES)
    assert tm % CHUNK == 0
    tri = _tri_incl(CHUNK)
    tri3 = jnp.asarray(np.concatenate([tri, tri, tri], axis=1), BF16)
    sel, cst = _aug_tables()

    row = lambda width: pl.BlockSpec((1, tm, width), lambda b, i: (b, i, 0))
    const = lambda shape: pl.BlockSpec(shape, lambda b, i: (0,) * len(shape))
    outs = [
        jax.ShapeDtypeStruct((B, L, ATT_W), BF16),
        jax.ShapeDtypeStruct((B, L, ATT_W), BF16),
        jax.ShapeDtypeStruct((B, L, ATT_W), BF16),
        jax.ShapeDtypeStruct((B, L, ATT_W), BF16),
        jax.ShapeDtypeStruct((B, L, ATT_W), BF16),
        jax.ShapeDtypeStruct((B, L, SSD_W), F32),
        jax.ShapeDtypeStruct((B, L, SSD_CONV_DIM), F32),
        jax.ShapeDtypeStruct((B, L, LANES), F32),
        jax.ShapeDtypeStruct((B, L // tm, 1, LANES), F32),
    ]
    return pl.pallas_call(
        _inproj_ab_kernel,
        grid=(B, L // tm),
        in_specs=[row(D), const((1, D)), const((D, AB_COLS)), const((1, LANES)),
                  const((CHUNK, 3 * CHUNK)), const((LANES, 2 * ATT_W)), const((1, 2 * ATT_W))],
        out_specs=[row(ATT_W)] * 5 + [row(SSD_W), row(SSD_CONV_DIM), row(LANES),
                   pl.BlockSpec((1, 1, 1, LANES), lambda b, i: (b, i, 0, 0))],
        out_shape=outs,
        scratch_shapes=[pltpu.VMEM((1, LANES), F32)],
        compiler_params=_params(("arbitrary", "arbitrary")),
        name="inproj_ab",
    )(h, nw.reshape(1, D), w, sb, tri3, sel, cst)


def _head_lane_mask(e, shape):
    lane = lax.broadcasted_iota(jnp.int32, shape, 1)
    return (lane < HEAD_DIM) if e == 0 else (lane >= HEAD_DIM)


EXP2_DEAD = -151.0
FOX_STRIP = 64


def _fox_kernel(qd_ref, qa_ref, kd_ref, ka_ref, v_ref, fend_ref, o_ref,
                q_ref, s0_ref, s1_ref, p0_ref, p1_ref, m_ref, al_ref, acc_ref, kn_ref, *, tq):
    i = pl.program_id(2)
    tk = tq
    R = 2 * tq
    qd = qd_ref[0].astype(F32)
    qa = qa_ref[0].astype(F32)
    lane = lax.broadcasted_iota(jnp.int32, (tq, LANES), 1)
    first = lane < HEAD_DIM
    for e in range(2):
        amask = (lane >= 6 * e) & (lane < 6 * e + 6)
        q_ref[e * tq:(e + 1) * tq, :] = jnp.concatenate(
            [jnp.where(first if e == 0 else ~first, qd, 0.0), jnp.where(amask, qa, 0.0)],
            axis=1).astype(BF16)
    m_ref[...] = jnp.full(m_ref.shape, -jnp.inf, F32)
    acc_ref[...] = jnp.zeros(acc_ref.shape, F32)
    ones = jnp.ones((tk, LANES), BF16)

    def head_sq_norms(x):
        sq = x * x
        return (jnp.sum(jnp.where(first, sq, 0.0), axis=1, keepdims=True),
                jnp.sum(jnp.where(first, 0.0, sq), axis=1, keepdims=True))

    @pl.when(i == 0)
    def _():
        def kbody(c, carry):
            ks = pl.multiple_of(c * tk, tk)
            n0, n1 = head_sq_norms(kd_ref[0, pl.ds(ks, tk), :].astype(F32))
            return jnp.maximum(carry[0], n0), jnp.maximum(carry[1], n1)
        zero = jnp.zeros((tk, 1), F32)
        n0, n1 = lax.fori_loop(0, kd_ref.shape[1] // tk, kbody, (zero, zero))
        for e, n in enumerate((n0, n1)):
            kn_ref[e] = jnp.broadcast_to(jnp.sqrt(jnp.max(n, axis=0, keepdims=True)), kn_ref.shape[1:])

    def scores(j, s_ref):
        ks = pl.multiple_of(j * tk, tk)
        k = jnp.concatenate([kd_ref[0, pl.ds(ks, tk), :], ka_ref[0, pl.ds(ks, tk), :]], axis=1)
        s_ref[...] = _dot_nt(q_ref[...], k)

    def update(j, s_ref, p_ref, masked):
        for r0 in range(0, R, FOX_STRIP):
            rs = slice(r0, r0 + FOX_STRIP)
            s = s_ref[rs, :]
            if masked:
                rows = lax.broadcasted_iota(jnp.int32, s.shape, 0) + (r0 % tq)
                cols = lax.broadcasted_iota(jnp.int32, s.shape, 1)
                s = jnp.where(rows >= cols, s, -jnp.inf)
            m_old = m_ref[rs, :]
            m_new = jnp.maximum(m_old, jnp.max(s, axis=1, keepdims=True))
            al_ref[rs, :] = jnp.exp2(m_old - m_new)
            m_ref[rs, :] = m_new
            p_ref[rs, :] = jnp.exp2(s - m_new).astype(BF16)
        ks = pl.multiple_of(j * tk, tk)
        v1 = jnp.concatenate([v_ref[0, pl.ds(ks, tk), :], ones], axis=1)
        acc_ref[...] = al_ref[...] * acc_ref[...] + _dot(p_ref[...], v1)

    scores(i, s0_ref)
    scores(jnp.maximum(i - 1, 0), s1_ref)
    update(i, s0_ref, p0_ref, True)

    qn = head_sq_norms(qd)
    fend = fend_ref[0]
    hl = lax.broadcasted_iota(jnp.int32, fend.shape, 1)
    jcol = lax.broadcasted_iota(jnp.int32, (fend.shape[0], 1), 0)
    jmin = i
    for e in range(2):
        head = 2 * pl.program_id(1) + e
        fe = jnp.sum(jnp.where(hl == head, fend, 0.0), axis=1, keepdims=True)
        f_top = jnp.sum(jnp.where(jcol == i - 1, fe, 0.0), axis=0, keepdims=True)
        qk = jnp.sqrt(jnp.max(qn[e], axis=0, keepdims=True)) * kn_ref[e, 0:1, 0:1]
        m_min = jnp.min(m_ref[e * tq:(e + 1) * tq, :], axis=0, keepdims=True)
        live = (jcol < i) & (qk + f_top - fe + 1.0 - m_min > EXP2_DEAD)
        jmin = jnp.minimum(jmin, jnp.min(jnp.where(live, jcol, i)))
    nb = i - jmin

    def pair(u, carry):
        j = i - 1 - 2 * u
        scores(j - 1, s0_ref)
        update(j, s1_ref, p1_ref, False)
        scores(jnp.maximum(j - 2, 0), s1_ref)
        update(j - 1, s0_ref, p0_ref, False)
        return carry

    lax.fori_loop(0, nb // 2, pair, 0)

    @pl.when(nb % 2 == 1)
    def _():
        update(jmin, s1_ref, p1_ref, False)

    acc = acc_ref[...]
    out = acc[:, :LANES] / acc[:, LANES:]
    o_ref[0] = jnp.where(first, out[:tq], out[tq:]).astype(BF16)


def _fox_attention(qd, qa, kd, ka, v, fend, tq):
    B, L, _ = qd.shape
    assert fend.shape == (B, L // tq, LANES)
    n_pairs = N_ATT_HEADS // 2
    R = 2 * tq
    qspec = pl.BlockSpec((1, tq, LANES), lambda b, p, i: (b, i, p))
    kspec = pl.BlockSpec((1, L, LANES), lambda b, p, i: (b, 0, p))
    return pl.pallas_call(
        functools.partial(_fox_kernel, tq=tq),
        grid=(B, n_pairs, L // tq),
        in_specs=[qspec, qspec, kspec, kspec, kspec,
                  pl.BlockSpec((1, L // tq, LANES), lambda b, p, i: (b, 0, 0))],
        out_specs=qspec,
        out_shape=jax.ShapeDtypeStruct((B, L, ATT_W), BF16),
        scratch_shapes=[pltpu.VMEM((R, 2 * LANES), BF16),
                        pltpu.VMEM((R, tq), F32), pltpu.VMEM((R, tq), F32),
                        pltpu.VMEM((R, tq), BF16), pltpu.VMEM((R, tq), BF16),
                        pltpu.VMEM((R, 1), F32), pltpu.VMEM((R, 1), F32),
                        pltpu.VMEM((R, 2 * LANES), F32),
                        pltpu.VMEM((2, SUBLANES, LANES), F32)],
        compiler_params=_params(("arbitrary", "arbitrary", "arbitrary")),
        name="fox_attention",
    )(qd, qa, kd, ka, v, fend)


def _ssd_kernel(xbc_ref, z_ref, dt_ref, cw_ref, cb_ref, alog_ref, dsk_ref, nw_ref, tri3_ref,
                y_ref, xext_ref, st_ref):
    @pl.when(pl.program_id(0) == 0)
    def _():
        xext_ref[:, 0:SUBLANES, :] = jnp.zeros((xext_ref.shape[0], SUBLANES, SSD_CONV_DIM), F32)
        st_ref[...] = jnp.zeros_like(st_ref)

    for b in range(xbc_ref.shape[0]):
        _ssd_chunk(xbc_ref.at[b], z_ref.at[b], dt_ref.at[b], cw_ref, cb_ref, alog_ref, dsk_ref,
                   nw_ref, tri3_ref, y_ref.at[b], xext_ref.at[b], st_ref.at[b])


def _ssd_chunk(xbc_ref, z_ref, dt_ref, cw_ref, cb_ref, alog_ref, dsk_ref, nw_ref, tri3_ref,
               y_ref, xext_ref, st_ref):
    Q = CHUNK
    xext_ref[SUBLANES:SUBLANES + Q, :] = xbc_ref[...]
    conv = cb_ref[...]
    for k in range(SSD_CONV):
        conv = conv + cw_ref[k:k + 1, :] * xext_ref[pl.ds(SUBLANES - (SSD_CONV - 1) + k, Q), :]
    xext_ref[0:SUBLANES, :] = xext_ref[Q:Q + SUBLANES, :]
    xc = _silu(conv)
    xs = xc[:, :SSD_W]
    bm = xc[:, SSD_W:SSD_W + LANES]
    cm = xc[:, SSD_W + LANES:]

    dt = dt_ref[...]
    a = dt * (-jnp.exp(alog_ref[...]))
    hi, mid, lo = _split3(a)
    a_cum = _dot(tri3_ref[...], jnp.concatenate([hi, mid, lo], axis=0))
    a_row = a_cum.T
    a_last_col = a_cum[Q - 1:Q, :]
    bm_t = bm.T

    rows = lax.broadcasted_iota(jnp.int32, (Q, Q), 0)
    cols = lax.broadcasted_iota(jnp.int32, (Q, Q), 1)
    causal = rows >= cols
    lane = lax.broadcasted_iota(jnp.int32, (Q, LANES), 1)
    first_half = lane < HEAD_DIM

    scores = []
    cmask = []
    for g in range(SSD_GROUPS):
        cg = jnp.where(_head_lane_mask(g, (Q, LANES)), cm, 0.0).astype(BF16)
        cmask.append(cg)
        scores.append(_dot_nt(cg, bm.astype(BF16)))

    y_pairs = []
    heads_per_group = SSD_HEADS // SSD_GROUPS
    for p in range(SSD_HEADS // 2):
        xs_pair = xs[:, p * LANES:(p + 1) * LANES]
        dt_pair = jnp.where(first_half, dt[:, DT_LANE0 + 2 * p:DT_LANE0 + 2 * p + 1],
                            dt[:, DT_LANE0 + 2 * p + 1:DT_LANE0 + 2 * p + 2])
        xdt = (xs_pair * dt_pair).astype(BF16)
        y_head = []
        for e in range(2):
            h = 2 * p + e
            g = h // heads_per_group
            hl = DT_LANE0 + h
            acol = a_cum[:, hl:hl + 1]
            arow = a_row[hl:hl + 1, :]
            alast = a_last_col[:, hl:hl + 1]
            lmat = jnp.exp(jnp.where(causal, acol - arow, -jnp.inf))
            y_diag = _dot((scores[g] * lmat).astype(BF16), xdt)
            prev = st_ref[h]
            y_off = _dot(cmask[g], prev.astype(BF16)) * jnp.exp(acol)
            y_head.append(y_diag + y_off)
            decay_row = jnp.exp(alast - arow)
            local = _dot((bm_t * decay_row).astype(BF16), xdt)
            st_ref[h] = prev * jnp.exp(alast) + local
        y_pairs.append(jnp.where(first_half, y_head[0], y_head[1]))
    y = jnp.concatenate(y_pairs, axis=1) + dsk_ref[...] * xs
    y = y * _silu(z_ref[...])
    gw = SSD_W // SSD_GROUPS
    y = jnp.concatenate([_rms(y[:, g * gw:(g + 1) * gw], nw_ref[:, g * gw:(g + 1) * gw])
                         for g in range(SSD_GROUPS)], axis=1)
    y_ref[...] = y.astype(BF16)


def _ssd(xbc, z, dt, conv_w, conv_b, a_log, d_skip, norm_w):
    B, L, _ = xbc.shape
    Q = CHUNK
    alog = jnp.zeros((1, LANES), F32).at[0, DT_LANE0:DT_LANE0 + SSD_HEADS].set(a_log)
    dsk = jnp.repeat(d_skip, SSD_W // SSD_HEADS).reshape(1, SSD_W)
    tri = _tri_incl(Q)
    tri3 = jnp.asarray(np.concatenate([tri, tri, tri], axis=1), BF16)
    row = lambda width: pl.BlockSpec((B, Q, width), lambda c: (0, c, 0))
    const = lambda shape: pl.BlockSpec(shape, lambda c: (0,) * len(shape))
    return pl.pallas_call(
        _ssd_kernel,
        grid=(L // Q,),
        in_specs=[row(SSD_CONV_DIM), row(SSD_W), row(LANES), const((SSD_CONV, SSD_CONV_DIM)),
                  const((1, SSD_CONV_DIM)), const((1, LANES)), const((1, SSD_W)),
                  const((1, SSD_W)), const((Q, 3 * Q))],
        out_specs=row(SSD_W),
        out_shape=jax.ShapeDtypeStruct((B, L, SSD_W), BF16),
        scratch_shapes=[pltpu.VMEM((B, Q + SUBLANES, SSD_CONV_DIM), F32),
                        pltpu.VMEM((B, SSD_HEADS, LANES, LANES), F32)],
        compiler_params=_params(("arbitrary",)),
        name="ssd_scan",
    )(xbc, z, dt, conv_w, conv_b.reshape(1, -1), alog, dsk, norm_w.reshape(1, -1), tri3)


def _mix_mlp_kernel(h_ref, y1_ref, y2_ref, wo_ref, nw_ref, wup_ref, wdn_ref, nf_ref, o_ref,
                    *, ff_chunk, final):
    half = y1_ref.shape[-1]
    mix = _dot(y1_ref[...], wo_ref[0:half, :]) + _dot(y2_ref[...], wo_ref[half:2 * half, :])
    h1 = h_ref[...] + mix
    u = _rms(h1, nw_ref[...]).astype(BF16)
    d_ff = wup_ref.shape[1]
    acc = jnp.zeros_like(h1)
    for c in range(d_ff // ff_chunk):
        act = jnp.maximum(_dot(u, wup_ref[:, c * ff_chunk:(c + 1) * ff_chunk]), 0.0)
        acc = acc + _dot((act * act).astype(BF16), wdn_ref[c * ff_chunk:(c + 1) * ff_chunk, :])
    h2 = h1 + acc
    if final:
        h2 = _rms(h2, nf_ref[...])
    o_ref[...] = h2


def _mix_mlp(h, y1, y2, w_out, nw, w_up, w_down, layer, nf, tm, final):
    B, L, D = h.shape
    T = B * L
    d_ff = w_up.shape[2]
    half = y1.shape[-1]
    row = lambda width: pl.BlockSpec((tm, width), lambda i: (i, 0))
    const = lambda shape: pl.BlockSpec(shape, lambda i: (0,) * len(shape))
    stacked = lambda shape: pl.BlockSpec((None,) + shape, lambda i: (layer, 0, 0))
    out = pl.pallas_call(
        functools.partial(_mix_mlp_kernel, ff_chunk=min(1024, d_ff), final=final),
        grid=(T // tm,),
        in_specs=[row(D), row(half), row(half), const((2 * half, D)), const((1, D)),
                  stacked((D, d_ff)), stacked((d_ff, D)), const((1, D))],
        out_specs=row(D),
        out_shape=jax.ShapeDtypeStruct((T, D), F32),
        compiler_params=_params(("arbitrary",)),
        name="mix_mlp",
    )(h.reshape(T, D), y1.reshape(T, half), y2.reshape(T, half), w_out.astype(BF16),
      nw.reshape(1, D), w_up, w_down, nf.reshape(1, D))
    return out.reshape(B, L, D)


GLA_QK = GLA_HEADS * GLA_DK
GLA_V = GLA_HEADS * GLA_DV
CD_COLS = 3 * ATT_W + 2 * GLA_QK + 2 * GLA_V + LANES


def _inproj_cd_kernel(x_ref, nw_ref, w_ref, sq_ref, sk_ref, sv_ref, gq_ref, gk_ref, gv_ref,
                      gr_ref, glow_ref):
    u = _rms(x_ref[...], nw_ref[...]).astype(BF16)

    def mm(a, b):
        return _dot(u, w_ref[:, a:b])

    o = 0
    sq_ref[...] = (mm(o, o + ATT_W) * (LOG2E * HEAD_DIM ** -0.5)).astype(BF16); o += ATT_W
    sk_ref[...] = mm(o, o + ATT_W).astype(BF16); o += ATT_W
    sv_ref[...] = mm(o, o + ATT_W).astype(BF16); o += ATT_W
    gq_ref[...] = mm(o, o + GLA_QK) * (GLA_DK ** -0.5); o += GLA_QK
    gk_ref[...] = mm(o, o + GLA_QK); o += GLA_QK
    gv_ref[...] = mm(o, o + GLA_V); o += GLA_V
    gr_ref[...] = mm(o, o + GLA_V); o += GLA_V
    glow_ref[...] = mm(o, o + LANES)


def _inproj_cd(h, nw, w_in, tm):
    B, L, D = h.shape
    T = B * L
    sq, sk, sv, gq, gk, gv, glow, gr = jnp.split(w_in, np.cumsum(
        [ATT_W, ATT_W, ATT_W, GLA_QK, GLA_QK, GLA_V, GLA_RANK]).tolist(), axis=1)
    pad = jnp.zeros((D, LANES - GLA_RANK), w_in.dtype)
    w = jnp.concatenate([sq, sk, sv, gq, gk, gv, gr, glow, pad], axis=1).astype(BF16)
    row = lambda width: pl.BlockSpec((tm, width), lambda i: (i, 0))
    const = lambda shape: pl.BlockSpec(shape, lambda i: (0,) * len(shape))
    widths = [ATT_W, ATT_W, ATT_W, GLA_QK, GLA_QK, GLA_V, GLA_V, LANES]
    dtypes = [BF16, BF16, BF16, F32, F32, F32, F32, F32]
    outs = pl.pallas_call(
        _inproj_cd_kernel,
        grid=(T // tm,),
        in_specs=[row(D), const((1, D)), const((D, CD_COLS))],
        out_specs=[row(wd) for wd in widths],
        out_shape=[jax.ShapeDtypeStruct((T, wd), dt) for wd, dt in zip(widths, dtypes)],
        compiler_params=_params(("arbitrary",)),
        name="inproj_cd",
    )(h.reshape(T, D), nw.reshape(1, D), w)
    return [o.reshape(B, L, -1) for o in outs]


SB_DEAD = EXP2_DEAD - 1.0
SB_STRIP = 64
SB_CHAINS = 2


def _sb_kernel(q_ref, k_ref, v_ref, u2_ref, o_ref,
               qs_ref, z_ref, sp_ref, in_ref, a_ref, c_ref, acc_ref, *, tq):
    i = pl.program_id(2)
    tk = tq
    R = 2 * tq
    chains = range(SB_CHAINS)
    for ch in chains:
        qf = q_ref[0, :, ch * LANES:(ch + 1) * LANES].astype(F32)
        for e in range(2):
            qs_ref[ch, e * tq:(e + 1) * tq, :] = jnp.where(_head_lane_mask(e, (tq, LANES)), qf,
                                                           0.0).astype(BF16)
    c_ref[...] = jnp.zeros(c_ref.shape, F32)
    acc_ref[...] = jnp.zeros(acc_ref.shape, F32)

    def strict_mask(r0, shape):
        rows = lax.broadcasted_iota(jnp.int32, shape, 0) + (r0 % tq)
        return lax.broadcasted_iota(jnp.int32, shape, 1) < rows

    def step(j, masked):
        ks = pl.multiple_of(j * tk, tk)
        for ch in chains:
            z_ref[ch] = _dot_nt(qs_ref[ch], k_ref[0, pl.ds(ks, tk), ch * LANES:(ch + 1) * LANES])
        for ch in chains:
            for r0 in range(0, R, SB_STRIP):
                rs = slice(r0, r0 + SB_STRIP)
                z = z_ref[ch, rs, :]
                sp = jnp.maximum(z, 0.0) + jnp.log2(1.0 + jnp.exp2(-jnp.abs(z)))
                if masked:
                    sp = jnp.where(strict_mask(r0, sp.shape), sp, 0.0)
                sp_ref[ch, rs, :] = sp.astype(BF16)
        for ch in chains:
            in_ref[ch] = _dot(sp_ref[ch], u2_ref[...])
        for ch in chains:
            for r0 in range(0, R, SB_STRIP):
                rs = slice(r0, r0 + SB_STRIP)
                inc = in_ref[ch, rs, :]
                a = jnp.exp2(z_ref[ch, rs, :] - c_ref[ch, rs, :] - inc)
                if masked:
                    a = jnp.where(strict_mask(r0, a.shape), a, 0.0)
                a_ref[ch, rs, :] = a.astype(BF16)
                c_ref[ch, rs, :] += inc[:, 0:1]
        for ch in chains:
            acc_ref[ch] += _dot(a_ref[ch], v_ref[0, pl.ds(ks, tk), ch * LANES:(ch + 1) * LANES])
        return jnp.min(c_ref[...])

    cmin = step(i, True)

    def cond(carry):
        j, cmin = carry
        return (j >= 0) & (cmin < -SB_DEAD)

    def body(carry):
        j, _ = carry
        return j - 1, step(j, False)

    lax.while_loop(cond, body, (i - 1, cmin))
    for ch in chains:
        acc = acc_ref[ch]
        o_ref[0, :, ch * LANES:(ch + 1) * LANES] = jnp.where(
            _head_lane_mask(0, (tq, LANES)), acc[:tq], acc[tq:]).astype(BF16)


def _sb_attention(q, k, v, tq):
    B, L, _ = q.shape
    n_groups = N_ATT_HEADS // (2 * SB_CHAINS)
    R = 2 * tq
    W = SB_CHAINS * LANES
    r = np.arange(tq)
    u = (r[:, None] >= r[None, :]).astype(np.float32)
    u2 = jnp.asarray(u, BF16)
    qspec = pl.BlockSpec((1, tq, W), lambda b, p, i: (b, i, p))
    kspec = pl.BlockSpec((1, L, W), lambda b, p, i: (b, 0, p))
    return pl.pallas_call(
        functools.partial(_sb_kernel, tq=tq),
        grid=(B, n_groups, L // tq),
        in_specs=[qspec, kspec, kspec, pl.BlockSpec((tq, tq), lambda b, p, i: (0, 0))],
        out_specs=qspec,
        out_shape=jax.ShapeDtypeStruct((B, L, ATT_W), BF16),
        scratch_shapes=[pltpu.VMEM((SB_CHAINS, R, LANES), BF16), pltpu.VMEM((SB_CHAINS, R, tq), F32),
                        pltpu.VMEM((SB_CHAINS, R, tq), BF16),
                        pltpu.VMEM((SB_CHAINS, R, tq), F32),
                        pltpu.VMEM((SB_CHAINS, R, tq), BF16), pltpu.VMEM((SB_CHAINS, R, 1), F32),
                        pltpu.VMEM((SB_CHAINS, R, LANES), F32)],
        compiler_params=_params(("arbitrary", "arbitrary", "arbitrary")),
        name="sb_attention",
    )(q, k, v, u2)


def _gla_tables():
    Q = CHUNK
    r = np.arange(Q)
    j = np.arange(Q)
    coef = [(j[None, :] <= r[:, None]), (j[None, :] > r[:, None])]
    masks = [np.eye(Q, dtype=bool)]
    for lvl in range(GLA_LEVELS):
        m = 1 << lvl
        c0 = (r // (2 * m)) * (2 * m)
        mid = c0 + m - 1
        upper = (r - c0) >= m
        up = (j[None, :] > mid[:, None]) & (j[None, :] <= r[:, None])
        lowr = (j[None, :] > r[:, None]) & (j[None, :] <= mid[:, None])
        coef.append(np.where(upper[:, None], up, lowr))
        masks.append((c0[:, None] == c0[None, :]) & upper[:, None] & (~upper)[None, :])
    coef = np.concatenate(coef, axis=0).astype(np.float32)
    coef2 = np.concatenate([coef, coef], axis=1)
    masks = np.stack(masks).astype(np.float32)
    hv = np.arange(GLA_V) // GLA_DV
    hk = np.arange(GLA_QK) // GLA_DK
    bdiag = (hv[:, None] == hk[None, :]).astype(np.float32)
    return jnp.asarray(coef2, BF16), jnp.asarray(masks, F32), jnp.asarray(bdiag, F32)


def _gla_kernel(gq_ref, gk_ref, gv_ref, glow_ref, gr_ref, w2_ref, gb_ref, coef_ref, mask_ref,
                bdiag_ref, nw_ref, o_ref, st_ref):
    @pl.when(pl.program_id(0) == 0)
    def _():
        st_ref[...] = jnp.zeros_like(st_ref)

    for b in range(gq_ref.shape[0]):
        _gla_chunk(gq_ref.at[b], gk_ref.at[b], gv_ref.at[b], glow_ref.at[b], gr_ref.at[b], w2_ref,
                   gb_ref, coef_ref, mask_ref, bdiag_ref, nw_ref, o_ref.at[b], st_ref.at[b])


def _gla_chunk(gq_ref, gk_ref, gv_ref, glow_ref, gr_ref, w2_ref, gb_ref, coef_ref, mask_ref,
               bdiag_ref, nw_ref, o_ref, st_ref):
    Q = CHUNK
    logits = _dot(glow_ref[...].astype(BF16), w2_ref[...]) + gb_ref[...]
    la = _log_sigmoid(logits) * (1.0 / GLA_GATE_NORM)
    hi, lo = _split2(la)
    expo = _dot(coef_ref[...], jnp.concatenate([hi, lo], axis=0))
    q = gq_ref[...]
    k = gk_ref[...]
    v = gv_ref[...]
    lane_head = lax.broadcasted_iota(jnp.int32, (Q, GLA_QK), 1) // GLA_DK
    row = lax.broadcasted_iota(jnp.int32, (Q, GLA_QK), 0)
    hmask = [lane_head == h for h in range(GLA_HEADS)]

    att = [None] * GLA_HEADS
    for lvl in range(-1, GLA_LEVELS):
        if lvl < 0:
            xq, xk = q, k.astype(BF16)
        else:
            m = 1 << lvl
            upper = (row & (2 * m - 1)) >= m
            xq = jnp.where(upper, q, k) * jnp.exp(expo[(2 + lvl) * Q:(3 + lvl) * Q, :])
            xk = xq.astype(BF16)
        msk = mask_ref[lvl + 1]
        for h in range(GLA_HEADS):
            part = _dot_nt(jnp.where(hmask[h], xq, 0.0).astype(BF16), xk) * msk
            att[h] = part if att[h] is None else att[h] + part

    st = st_ref[...]
    q_in = (q * jnp.exp(expo[0:Q, :])).astype(BF16)
    o = _dot_nt(q_in, st.astype(BF16))
    o_intra = [_dot(att[h].astype(BF16), v[:, h * GLA_DV:(h + 1) * GLA_DV].astype(BF16))
               for h in range(GLA_HEADS)]
    o = o + jnp.concatenate(o_intra, axis=1)

    k_dec = (k * jnp.exp(expo[Q:2 * Q, :])).astype(BF16)
    upd = _dot(v.T.astype(BF16), k_dec)
    g_last = expo[Q - 1:Q, :]
    st_ref[...] = st * jnp.exp(g_last) + upd * bdiag_ref[...]

    gr = gr_ref[...]
    o = jnp.concatenate([_rms(o[:, h * GLA_DV:(h + 1) * GLA_DV], nw_ref[...])
                         for h in range(GLA_HEADS)], axis=1)
    o_ref[...] = (o * _silu(gr)).astype(BF16)


def _gla(gq, gk, gv, glow, gr, gate_w2, gate_b, norm_w):
    B, L, _ = gq.shape
    Q = CHUNK
    coef2, masks, bdiag = _gla_tables()
    w2 = jnp.zeros((LANES, GLA_QK), F32).at[:GLA_RANK].set(gate_w2).astype(BF16)
    row = lambda width: pl.BlockSpec((B, Q, width), lambda c: (0, c, 0))
    const = lambda shape: pl.BlockSpec(shape, lambda c: (0,) * len(shape))
    return pl.pallas_call(
        _gla_kernel,
        grid=(L // Q,),
        in_specs=[row(GLA_QK), row(GLA_QK), row(GLA_V), row(LANES), row(GLA_V),
                  const((LANES, GLA_QK)), const((1, GLA_QK)), const(coef2.shape),
                  const(masks.shape), const(bdiag.shape), const((1, GLA_DV))],
        out_specs=row(GLA_V),
        out_shape=jax.ShapeDtypeStruct((B, L, GLA_V), BF16),
        scratch_shapes=[pltpu.VMEM((B, GLA_V, GLA_QK), F32)],
        compiler_params=_params(("arbitrary",)),
        name="gla_scan",
    )(gq, gk, gv, glow, gr, w2, gate_b.reshape(1, -1), coef2, masks, bdiag,
      norm_w.reshape(1, -1))


def _block(n, want):
    return want if n % want == 0 else n


def kernel(x, norm_mix, norm_mlp, norm_final, w_in_ab, fox_f_bias, ssd_conv_w, ssd_conv_b,
           ssd_dt_bias, ssd_a_log, ssd_d, ssd_norm, w_out_ab, w_in_cd, gla_gate_w2,
           gla_gate_b, gla_norm, w_out_cd, w_mlp_up, w_mlp_down):
    B, L, D = x.shape
    assert L % CHUNK == 0
    tm = _block(L, 512)
    tq = _block(L, 256)
    tq_fox = _block(L, 512)

    assert tm == tq_fox
    head_order = jnp.argsort(fox_f_bias[0])
    head_cols = (head_order[:, None] * HEAD_DIM + jnp.arange(HEAD_DIM)[None, :]).reshape(-1)
    qd, kd, vd, qa, ka, z, xbc, dt, fend = _inproj_ab(x, norm_mix[0], w_in_ab[0], fox_f_bias[0],
                                                      ssd_dt_bias[0], head_order, head_cols, tm)
    w_out0 = jnp.concatenate([jnp.take(w_out_ab[0, :ATT_W], head_cols, axis=0),
                              w_out_ab[0, ATT_W:]], axis=0)
    y_fox = _fox_attention(qd, qa, kd, ka, vd, fend.reshape(B, L // tm, LANES), tq_fox)
    y_ssd = _ssd(xbc, z, dt, ssd_conv_w[0], ssd_conv_b[0], ssd_a_log[0], ssd_d[0], ssd_norm[0])
    w_up = w_mlp_up.astype(BF16)
    w_down = w_mlp_down.astype(BF16)
    h = _mix_mlp(x, y_fox, y_ssd, w_out0, norm_mlp[0], w_up, w_down, 0, norm_final, tm,
                 final=False)

    sq, sk, sv, gq, gk, gv, gr, glow = _inproj_cd(h, norm_mix[1], w_in_cd[0], tm)
    y_sb = _sb_attention(sq, sk, sv, tq)
    y_gla = _gla(gq, gk, gv, glow, gr, gla_gate_w2[0], gla_gate_b[0], gla_norm[0])
    return _mix_mlp(h, y_sb, y_gla, w_out_cd[0], norm_mlp[1], w_up, w_down, 1, norm_final, tm,
                    final=True)
```

```python
import functools
import math

import numpy as np
import jax
import jax.numpy as jnp
from jax import lax
from jax.experimental import pallas as pl
from jax.experimental.pallas import tpu as pltpu

F32 = jnp.float32
BF16 = jnp.bfloat16

LANES = 128
SUBLANES = 8
VMEM_LIMIT_BYTES = 56 * 1024 * 1024

HEAD_DIM = 64
N_ATT_HEADS = 8
ATT_W = N_ATT_HEADS * HEAD_DIM
SSD_HEADS = 8
SSD_W = 512
SSD_GROUPS = 2
SSD_STATE = 64
SSD_CONV = 4
SSD_CONV_DIM = SSD_W + 2 * SSD_GROUPS * SSD_STATE
GLA_HEADS = 4
GLA_DK = 64
GLA_DV = 128
GLA_RANK = 16
GLA_GATE_NORM = 16.0
EPS = 1e-5
CHUNK = 128
GLA_LEVELS = 7
LOG2E = math.log2(math.e)
DT_LANE0 = 8


def _params(sem):
    return pltpu.CompilerParams(dimension_semantics=sem, vmem_limit_bytes=VMEM_LIMIT_BYTES)


def _split3(x):
    hi = x.astype(BF16)
    r = x - hi.astype(F32)
    mid = r.astype(BF16)
    lo = (r - mid.astype(F32)).astype(BF16)
    return hi, mid, lo


def _split2(x):
    hi = x.astype(BF16)
    lo = (x - hi.astype(F32)).astype(BF16)
    return hi, lo


def _softplus(x):
    return jnp.maximum(x, 0.0) + jnp.log1p(jnp.exp(-jnp.abs(x)))


def _log_sigmoid(x):
    return jnp.minimum(x, 0.0) - jnp.log1p(jnp.exp(-jnp.abs(x)))


def _silu(x):
    h = 0.5 * x
    return h + h * jnp.tanh(h)


def _rms(x, w):
    ms = jnp.mean(x * x, axis=-1, keepdims=True)
    return x * lax.rsqrt(ms + EPS) * w


def _dot(a, b):
    return jnp.dot(a, b, preferred_element_type=F32)


def _dot_nt(a, b):
    return lax.dot_general(a, b, (((1,), (1,)), ((), ())), preferred_element_type=F32)


AB_COLS = 3 * ATT_W + SSD_W + SSD_CONV_DIM + LANES


def _inproj_ab_kernel(x_ref, nw_ref, w_ref, sb_ref, tri3_ref, sel_ref, cst_ref,
                      qd_ref, kd_ref, vd_ref, qa_ref, ka_ref, z_ref, xbc_ref, dt_ref, fend_ref,
                      carry_ref):
    @pl.when(pl.program_id(1) == 0)
    def _():
        carry_ref[...] = jnp.zeros_like(carry_ref)

    u = _rms(x_ref[0], nw_ref[...]).astype(BF16)

    def mm(a, b):
        return _dot(u, w_ref[:, a:b])

    o = 0
    qd_ref[0] = (mm(o, o + ATT_W) * (LOG2E * HEAD_DIM ** -0.5)).astype(BF16); o += ATT_W
    kd_ref[0] = mm(o, o + ATT_W).astype(BF16); o += ATT_W
    vd_ref[0] = mm(o, o + ATT_W).astype(BF16); o += ATT_W
    z_ref[0] = mm(o, o + SSD_W); o += SSD_W
    xbc_ref[0] = mm(o, o + SSD_CONV_DIM); o += SSD_CONV_DIM
    small = mm(o, o + LANES) + sb_ref[...]

    lane = lax.broadcasted_iota(jnp.int32, small.shape, 1)
    is_f = lane < N_ATT_HEADS
    is_dt = (lane >= DT_LANE0) & (lane < DT_LANE0 + SSD_HEADS)
    log_f = jnp.where(is_f, _log_sigmoid(small), 0.0)
    dt_ref[0] = jnp.where(is_dt, _softplus(small), 0.0)

    hi, mid, lo = _split3(log_f)
    tm = log_f.shape[0]
    carry = carry_ref[...]
    cums = []
    for r0 in range(0, tm, CHUNK):
        rs = slice(r0, r0 + CHUNK)
        piece = _dot(tri3_ref[...], jnp.concatenate([hi[rs], mid[rs], lo[rs]], axis=0)) + carry
        carry = piece[CHUNK - 1:CHUNK, :]
        cums.append(piece)
    cum = jnp.concatenate(cums, axis=0)
    carry_ref[...] = carry

    cum2 = cum * LOG2E
    fend_ref[0, 0] = cum2[tm - 1:tm, :]
    fh, fm, fl = _split3(cum2)
    fcat = (fh.astype(F32) + pltpu.roll(fm.astype(F32), N_ATT_HEADS, 1)
            + pltpu.roll(fl.astype(F32), 2 * N_ATT_HEADS, 1)).astype(BF16)
    aug = _dot(fcat, sel_ref[...]) + cst_ref[...]
    qa_ref[0] = aug[:, :ATT_W].astype(BF16)
    ka_ref[0] = aug[:, ATT_W:].astype(BF16)


def _aug_tables():
    sel = np.zeros((LANES, 2 * ATT_W), np.float32)
    cst = np.zeros((1, 2 * ATT_W), np.float32)
    for h in range(N_ATT_HEADS):
        p, e = divmod(h, 2)
        for part in range(3):
            src = part * N_ATT_HEADS + h
            sel[src, p * LANES + 6 * e + part] = 1.0
            sel[src, ATT_W + p * LANES + 6 * e + 3 + part] = -1.0
            cst[0, p * LANES + 6 * e + 3 + part] = 1.0
            cst[0, ATT_W + p * LANES + 6 * e + part] = 1.0
    return jnp.asarray(sel, BF16), jnp.asarray(cst, F32)


def _tri_incl(n):
    r = np.arange(n)
    return (r[None, :] <= r[:, None]).astype(np.float32)


def _inproj_ab(h, nw, w_in, f_bias, dt_bias, head_order, head_cols, tm):
    B, L, D = h.shape
    fq, fk, fv, fl, wz, wxbc, wdt = jnp.split(w_in, np.cumsum(
        [ATT_W, ATT_W, ATT_W, N_ATT_HEADS, SSD_W, SSD_CONV_DIM])[:].tolist(), axis=1)
    fq, fk, fv = (jnp.take(w, head_cols, axis=1) for w in (fq, fk, fv))
    fl = jnp.take(fl, head_order, axis=1)
    f_bias = f_bias[head_order]
    pad = jnp.zeros((D, LANES - N_ATT_HEADS - SSD_HEADS), w_in.dtype)
    w = jnp.concatenate([fq, fk, fv, wz, wxbc, fl, wdt, pad], axis=1).astype(BF16)
    sb = jnp.concatenate([f_bias, dt_bias, jnp.zeros((LANES - 16,), F32)]).reshape(1, LANES)
    assert tm % CHUNK == 0
    tri = _tri_incl(CHUNK)
    tri3 = jnp.asarray(np.concatenate([tri, tri, tri], axis=1), BF16)
    sel, cst = _aug_tables()

    row = lambda width: pl.BlockSpec((1, tm, width), lambda b, i: (b, i, 0))
    const = lambda shape: pl.BlockSpec(shape, lambda b, i: (0,) * len(shape))
    outs = [
        jax.ShapeDtypeStruct((B, L, ATT_W), BF16),
        jax.ShapeDtypeStruct((B, L, ATT_W), BF16),
        jax.ShapeDtypeStruct((B, L, ATT_W), BF16),
        jax.ShapeDtypeStruct((B, L, ATT_W), BF16),
        jax.ShapeDtypeStruct((B, L, ATT_W), BF16),
        jax.ShapeDtypeStruct((B, L, SSD_W), F32),
        jax.ShapeDtypeStruct((B, L, SSD_CONV_DIM), F32),
        jax.ShapeDtypeStruct((B, L, LANES), F32),
        jax.ShapeDtypeStruct((B, L // tm, 1, LANES), F32),
    ]
    return pl.pallas_call(
        _inproj_ab_kernel,
        grid=(B, L // tm),
        in_specs=[row(D), const((1, D)), const((D, AB_COLS)), const((1, LANES)),
                  const((CHUNK, 3 * CHUNK)), const((LANES, 2 * ATT_W)), const((1, 2 * ATT_W))],
        out_specs=[row(ATT_W)] * 5 + [row(SSD_W), row(SSD_CONV_DIM), row(LANES),
                   pl.BlockSpec((1, 1, 1, LANES), lambda b, i: (b, i, 0, 0))],
        out_shape=outs,
        scratch_shapes=[pltpu.VMEM((1, LANES), F32)],
        compiler_params=_params(("arbitrary", "arbitrary")),
        name="inproj_ab",
    )(h, nw.reshape(1, D), w, sb, tri3, sel, cst)


def _head_lane_mask(e, shape):
    lane = lax.broadcasted_iota(jnp.int32, shape, 1)
    return (lane < HEAD_DIM) if e == 0 else (lane >= HEAD_DIM)


EXP2_DEAD = -151.0
FOX_STRIP = 64


def _fox_kernel(qd_ref, qa_ref, kd_ref, ka_ref, v_ref, fend_ref, o_ref,
                q_ref, s0_ref, s1_ref, p0_ref, p1_ref, m_ref, al_ref, acc_ref, kn_ref, *, tq):
    i = pl.program_id(2)
    tk = tq
    R = 2 * tq
    qd = qd_ref[0].astype(F32)
    qa = qa_ref[0].astype(F32)
    lane = lax.broadcasted_iota(jnp.int32, (tq, LANES), 1)
    first = lane < HEAD_DIM
    for e in range(2):
        amask = (lane >= 6 * e) & (lane < 6 * e + 6)
        q_ref[e * tq:(e + 1) * tq, :] = jnp.concatenate(
            [jnp.where(first if e == 0 else ~first, qd, 0.0), jnp.where(amask, qa, 0.0)],
            axis=1).astype(BF16)
    m_ref[...] = jnp.full(m_ref.shape, -jnp.inf, F32)
    acc_ref[...] = jnp.zeros(acc_ref.shape, F32)
    ones = jnp.ones((tk, LANES), BF16)

    def head_sq_norms(x):
        sq = x * x
        return (jnp.sum(jnp.where(first, sq, 0.0), axis=1, keepdims=True),
                jnp.sum(jnp.where(first, 0.0, sq), axis=1, keepdims=True))

    @pl.when(i == 0)
    def _():
        def kbody(c, carry):
            ks = pl.multiple_of(c * tk, tk)
            n0, n1 = head_sq_norms(kd_ref[0, pl.ds(ks, tk), :].astype(F32))
            return jnp.maximum(carry[0], n0), jnp.maximum(carry[1], n1)
        zero = jnp.zeros((tk, 1), F32)
        n0, n1 = lax.fori_loop(0, kd_ref.shape[1] // tk, kbody, (zero, zero))
        for e, n in enumerate((n0, n1)):
            kn_ref[e] = jnp.broadcast_to(jnp.sqrt(jnp.max(n, axis=0, keepdims=True)), kn_ref.shape[1:])

    def scores(j, s_ref):
        ks = pl.multiple_of(j * tk, tk)
        k = jnp.concatenate([kd_ref[0, pl.ds(ks, tk), :], ka_ref[0, pl.ds(ks, tk), :]], axis=1)
        s_ref[...] = _dot_nt(q_ref[...], k)

    def update(j, s_ref, p_ref, masked):
        for r0 in range(0, R, FOX_STRIP):
            rs = slice(r0, r0 + FOX_STRIP)
            s = s_ref[rs, :]
            if masked:
                rows = lax.broadcasted_iota(jnp.int32, s.shape, 0) + (r0 % tq)
                cols = lax.broadcasted_iota(jnp.int32, s.shape, 1)
                s = jnp.where(rows >= cols, s, -jnp.inf)
            m_old = m_ref[rs, :]
            m_new = jnp.maximum(m_old, jnp.max(s, axis=1, keepdims=True))
            al_ref[rs, :] = jnp.exp2(m_old - m_new)
            m_ref[rs, :] = m_new
            p_ref[rs, :] = jnp.exp2(s - m_new).astype(BF16)
        ks = pl.multiple_of(j * tk, tk)
        v1 = jnp.concatenate([v_ref[0, pl.ds(ks, tk), :], ones], axis=1)
        acc_ref[...] = al_ref[...] * acc_ref[...] + _dot(p_ref[...], v1)

    scores(i, s0_ref)
    scores(jnp.maximum(i - 1, 0), s1_ref)
    update(i, s0_ref, p0_ref, True)

    qn = head_sq_norms(qd)
    fend = fend_ref[0]
    hl = lax.broadcasted_iota(jnp.int32, fend.shape, 1)
    jcol = lax.broadcasted_iota(jnp.int32, (fend.shape[0], 1), 0)
    jmin = i
    for e in range(2):
        head = 2 * pl.program_id(1) + e
        fe = jnp.sum(jnp.where(hl == head, fend, 0.0), axis=1, keepdims=True)
        f_top = jnp.sum(jnp.where(jcol == i - 1, fe, 0.0), axis=0, keepdims=True)
        qk = jnp.sqrt(jnp.max(qn[e], axis=0, keepdims=True)) * kn_ref[e, 0:1, 0:1]
        m_min = jnp.min(m_ref[e * tq:(e + 1) * tq, :], axis=0, keepdims=True)
        live = (jcol < i) & (qk + f_top - fe + 1.0 - m_min > EXP2_DEAD)
        jmin = jnp.minimum(jmin, jnp.min(jnp.where(live, jcol, i)))
    nb = i - jmin

    def pair(u, carry):
        j = i - 1 - 2 * u
        scores(j - 1, s0_ref)
        update(j, s1_ref, p1_ref, False)
        scores(jnp.maximum(j - 2, 0), s1_ref)
        update(j - 1, s0_ref, p0_ref, False)
        return carry

    def quad(u, carry):
        pair(2 * u, carry)
        pair(2 * u + 1, carry)
        return carry

    lax.fori_loop(0, nb // 4, quad, 0)
    lax.fori_loop(2 * (nb // 4), nb // 2, pair, 0)

    @pl.when(nb % 2 == 1)
    def _():
        update(jmin, s1_ref, p1_ref, False)

    acc = acc_ref[...]
    out = acc[:, :LANES] / acc[:, LANES:]
    o_ref[0] = jnp.where(first, out[:tq], out[tq:]).astype(BF16)


def _fox_attention(qd, qa, kd, ka, v, fend, tq):
    B, L, _ = qd.shape
    assert fend.shape == (B, L // tq, LANES)
    n_pairs = N_ATT_HEADS // 2
    R = 2 * tq
    qspec = pl.BlockSpec((1, tq, LANES), lambda b, p, i: (b, i, p))
    kspec = pl.BlockSpec((1, L, LANES), lambda b, p, i: (b, 0, p))
    return pl.pallas_call(
        functools.partial(_fox_kernel, tq=tq),
        grid=(B, n_pairs, L // tq),
        in_specs=[qspec, qspec, kspec, kspec, kspec,
                  pl.BlockSpec((1, L // tq, LANES), lambda b, p, i: (b, 0, 0))],
        out_specs=qspec,
        out_shape=jax.ShapeDtypeStruct((B, L, ATT_W), BF16),
        scratch_shapes=[pltpu.VMEM((R, 2 * LANES), BF16),
                        pltpu.VMEM((R, tq), F32), pltpu.VMEM((R, tq), F32),
                        pltpu.VMEM((R, tq), BF16), pltpu.VMEM((R, tq), BF16),
                        pltpu.VMEM((R, 1), F32), pltpu.VMEM((R, 1), F32),
                        pltpu.VMEM((R, 2 * LANES), F32),
                        pltpu.VMEM((2, SUBLANES, LANES), F32)],
        compiler_params=_params(("arbitrary", "arbitrary", "arbitrary")),
        name="fox_attention",
    )(qd, qa, kd, ka, v, fend)


def _ssd_kernel(xbc_ref, z_ref, dt_ref, cw_ref, cb_ref, alog_ref, dsk_ref, nw_ref, tri3_ref,
                y_ref, xext_ref, st_ref):
    @pl.when(pl.program_id(0) == 0)
    def _():
        xext_ref[:, 0:SUBLANES, :] = jnp.zeros((xext_ref.shape[0], SUBLANES, SSD_CONV_DIM), F32)
        st_ref[...] = jnp.zeros_like(st_ref)

    for b in range(xbc_ref.shape[0]):
        _ssd_chunk(xbc_ref.at[b], z_ref.at[b], dt_ref.at[b], cw_ref, cb_ref, alog_ref, dsk_ref,
                   nw_ref, tri3_ref, y_ref.at[b], xext_ref.at[b], st_ref.at[b])


def _ssd_chunk(xbc_ref, z_ref, dt_ref, cw_ref, cb_ref, alog_ref, dsk_ref, nw_ref, tri3_ref,
               y_ref, xext_ref, st_ref):
    Q = CHUNK
    xext_ref[SUBLANES:SUBLANES + Q, :] = xbc_ref[...]
    conv = cb_ref[...]
    for k in range(SSD_CONV):
        conv = conv + cw_ref[k:k + 1, :] * xext_ref[pl.ds(SUBLANES - (SSD_CONV - 1) + k, Q), :]
    xext_ref[0:SUBLANES, :] = xext_ref[Q:Q + SUBLANES, :]
    xc = _silu(conv)
    xs = xc[:, :SSD_W]
    bm = xc[:, SSD_W:SSD_W + LANES]
    cm = xc[:, SSD_W + LANES:]

    dt = dt_ref[...]
    a = dt * (-jnp.exp(alog_ref[...]))
    hi, mid, lo = _split3(a)
    a_cum = _dot(tri3_ref[...], jnp.concatenate([hi, mid, lo], axis=0))
    a_row = a_cum.T
    a_last_col = a_cum[Q - 1:Q, :]
    bm_t = bm.T

    rows = lax.broadcasted_iota(jnp.int32, (Q, Q), 0)
    cols = lax.broadcasted_iota(jnp.int32, (Q, Q), 1)
    causal = rows >= cols
    lane = lax.broadcasted_iota(jnp.int32, (Q, LANES), 1)
    first_half = lane < HEAD_DIM

    scores = []
    cmask = []
    for g in range(SSD_GROUPS):
        cg = jnp.where(_head_lane_mask(g, (Q, LANES)), cm, 0.0).astype(BF16)
        cmask.append(cg)
        scores.append(_dot_nt(cg, bm.astype(BF16)))

    y_pairs = []
    heads_per_group = SSD_HEADS // SSD_GROUPS
    for p in range(SSD_HEADS // 2):
        xs_pair = xs[:, p * LANES:(p + 1) * LANES]
        dt_pair = jnp.where(first_half, dt[:, DT_LANE0 + 2 * p:DT_LANE0 + 2 * p + 1],
                            dt[:, DT_LANE0 + 2 * p + 1:DT_LANE0 + 2 * p + 2])
        xdt = (xs_pair * dt_pair).astype(BF16)
        y_head = []
        for e in range(2):
            h = 2 * p + e
            g = h // heads_per_group
            hl = DT_LANE0 + h
            acol = a_cum[:, hl:hl + 1]
            arow = a_row[hl:hl + 1, :]
            alast = a_last_col[:, hl:hl + 1]
            lmat = jnp.exp(jnp.where(causal, acol - arow, -jnp.inf))
            y_diag = _dot((scores[g] * lmat).astype(BF16), xdt)
            prev = st_ref[h]
            y_off = _dot(cmask[g], prev.astype(BF16)) * jnp.exp(acol)
            y_head.append(y_diag + y_off)
            decay_row = jnp.exp(alast - arow)
            local = _dot((bm_t * decay_row).astype(BF16), xdt)
            st_ref[h] = prev * jnp.exp(alast) + local
        y_pairs.append(jnp.where(first_half, y_head[0], y_head[1]))
    y = jnp.concatenate(y_pairs, axis=1) + dsk_ref[...] * xs
    y = y * _silu(z_ref[...])
    gw = SSD_W // SSD_GROUPS
    y = jnp.concatenate([_rms(y[:, g * gw:(g + 1) * gw], nw_ref[:, g * gw:(g + 1) * gw])
                         for g in range(SSD_GROUPS)], axis=1)
    y_ref[...] = y.astype(BF16)


def _ssd(xbc, z, dt, conv_w, conv_b, a_log, d_skip, norm_w):
    B, L, _ = xbc.shape
    Q = CHUNK
    alog = jnp.zeros((1, LANES), F32).at[0, DT_LANE0:DT_LANE0 + SSD_HEADS].set(a_log)
    dsk = jnp.repeat(d_skip, SSD_W // SSD_HEADS).reshape(1, SSD_W)
    tri = _tri_incl(Q)
    tri3 = jnp.asarray(np.concatenate([tri, tri, tri], axis=1), BF16)
    row = lambda width: pl.BlockSpec((B, Q, width), lambda c: (0, c, 0))
    const = lambda shape: pl.BlockSpec(shape, lambda c: (0,) * len(shape))
    return pl.pallas_call(
        _ssd_kernel,
        grid=(L // Q,),
        in_specs=[row(SSD_CONV_DIM), row(SSD_W), row(LANES), const((SSD_CONV, SSD_CONV_DIM)),
                  const((1, SSD_CONV_DIM)), const((1, LANES)), const((1, SSD_W)),
                  const((1, SSD_W)), const((Q, 3 * Q))],
        out_specs=row(SSD_W),
        out_shape=jax.ShapeDtypeStruct((B, L, SSD_W), BF16),
        scratch_shapes=[pltpu.VMEM((B, Q + SUBLANES, SSD_CONV_DIM), F32),
                        pltpu.VMEM((B, SSD_HEADS, LANES, LANES), F32)],
        compiler_params=_params(("arbitrary",)),
        name="ssd_scan",
    )(xbc, z, dt, conv_w, conv_b.reshape(1, -1), alog, dsk, norm_w.reshape(1, -1), tri3)


def _mix_mlp_kernel(h_ref, y1_ref, y2_ref, wo_ref, nw_ref, wup_ref, wdn_ref, nf_ref, o_ref,
                    *, ff_chunk, final):
    half = y1_ref.shape[-1]
    mix = _dot(y1_ref[...], wo_ref[0:half, :]) + _dot(y2_ref[...], wo_ref[half:2 * half, :])
    h1 = h_ref[...] + mix
    u = _rms(h1, nw_ref[...]).astype(BF16)
    d_ff = wup_ref.shape[1]
    acc = jnp.zeros_like(h1)
    for c in range(d_ff // ff_chunk):
        act = jnp.maximum(_dot(u, wup_ref[:, c * ff_chunk:(c + 1) * ff_chunk]), 0.0)
        acc = acc + _dot((act * act).astype(BF16), wdn_ref[c * ff_chunk:(c + 1) * ff_chunk, :])
    h2 = h1 + acc
    if final:
        h2 = _rms(h2, nf_ref[...])
    o_ref[...] = h2


def _mix_mlp(h, y1, y2, w_out, nw, w_up, w_down, layer, nf, tm, final):
    B, L, D = h.shape
    T = B * L
    d_ff = w_up.shape[2]
    half = y1.shape[-1]
    row = lambda width: pl.BlockSpec((tm, width), lambda i: (i, 0))
    const = lambda shape: pl.BlockSpec(shape, lambda i: (0,) * len(shape))
    stacked = lambda shape: pl.BlockSpec((None,) + shape, lambda i: (layer, 0, 0))
    out = pl.pallas_call(
        functools.partial(_mix_mlp_kernel, ff_chunk=min(1024, d_ff), final=final),
        grid=(T // tm,),
        in_specs=[row(D), row(half), row(half), const((2 * half, D)), const((1, D)),
                  stacked((D, d_ff)), stacked((d_ff, D)), const((1, D))],
        out_specs=row(D),
        out_shape=jax.ShapeDtypeStruct((T, D), F32),
        compiler_params=_params(("arbitrary",)),
        name="mix_mlp",
    )(h.reshape(T, D), y1.reshape(T, half), y2.reshape(T, half), w_out.astype(BF16),
      nw.reshape(1, D), w_up, w_down, nf.reshape(1, D))
    return out.reshape(B, L, D)


GLA_QK = GLA_HEADS * GLA_DK
GLA_V = GLA_HEADS * GLA_DV
CD_COLS = 3 * ATT_W + 2 * GLA_QK + 2 * GLA_V + LANES


def _inproj_cd_kernel(x_ref, nw_ref, w_ref, sq_ref, sk_ref, sv_ref, gq_ref, gk_ref, gv_ref,
                      gr_ref, glow_ref):
    u = _rms(x_ref[...], nw_ref[...]).astype(BF16)

    def mm(a, b):
        return _dot(u, w_ref[:, a:b])

    o = 0
    sq_ref[...] = (mm(o, o + ATT_W) * (LOG2E * HEAD_DIM ** -0.5)).astype(BF16); o += ATT_W
    sk_ref[...] = mm(o, o + ATT_W).astype(BF16); o += ATT_W
    sv_ref[...] = mm(o, o + ATT_W).astype(BF16); o += ATT_W
    gq_ref[...] = mm(o, o + GLA_QK) * (GLA_DK ** -0.5); o += GLA_QK
    gk_ref[...] = mm(o, o + GLA_QK); o += GLA_QK
    gv_ref[...] = mm(o, o + GLA_V); o += GLA_V
    gr_ref[...] = mm(o, o + GLA_V); o += GLA_V
    glow_ref[...] = mm(o, o + LANES)


def _inproj_cd(h, nw, w_in, tm):
    B, L, D = h.shape
    T = B * L
    sq, sk, sv, gq, gk, gv, glow, gr = jnp.split(w_in, np.cumsum(
        [ATT_W, ATT_W, ATT_W, GLA_QK, GLA_QK, GLA_V, GLA_RANK]).tolist(), axis=1)
    pad = jnp.zeros((D, LANES - GLA_RANK), w_in.dtype)
    w = jnp.concatenate([sq, sk, sv, gq, gk, gv, gr, glow, pad], axis=1).astype(BF16)
    row = lambda width: pl.BlockSpec((tm, width), lambda i: (i, 0))
    const = lambda shape: pl.BlockSpec(shape, lambda i: (0,) * len(shape))
    widths = [ATT_W, ATT_W, ATT_W, GLA_QK, GLA_QK, GLA_V, GLA_V, LANES]
    dtypes = [BF16, BF16, BF16, F32, F32, F32, F32, F32]
    outs = pl.pallas_call(
        _inproj_cd_kernel,
        grid=(T // tm,),
        in_specs=[row(D), const((1, D)), const((D, CD_COLS))],
        out_specs=[row(wd) for wd in widths],
        out_shape=[jax.ShapeDtypeStruct((T, wd), dt) for wd, dt in zip(widths, dtypes)],
        compiler_params=_params(("arbitrary",)),
        name="inproj_cd",
    )(h.reshape(T, D), nw.reshape(1, D), w)
    return [o.reshape(B, L, -1) for o in outs]


SB_DEAD = EXP2_DEAD - 1.0
SB_STRIP = 64
SB_CHAINS = 2


def _sb_kernel(q_ref, k_ref, v_ref, u2_ref, o_ref,
               qs_ref, z_ref, sp_ref, in_ref, a_ref, c_ref, acc_ref, *, tq):
    i = pl.program_id(2)
    tk = tq
    R = 2 * tq
    chains = range(SB_CHAINS)
    for ch in chains:
        qf = q_ref[0, :, ch * LANES:(ch + 1) * LANES].astype(F32)
        for e in range(2):
            qs_ref[ch, e * tq:(e + 1) * tq, :] = jnp.where(_head_lane_mask(e, (tq, LANES)), qf,
                                                           0.0).astype(BF16)
    c_ref[...] = jnp.zeros(c_ref.shape, F32)
    acc_ref[...] = jnp.zeros(acc_ref.shape, F32)

    def strict_mask(r0, shape):
        rows = lax.broadcasted_iota(jnp.int32, shape, 0) + (r0 % tq)
        return lax.broadcasted_iota(jnp.int32, shape, 1) < rows

    def step(j, masked):
        ks = pl.multiple_of(j * tk, tk)
        for ch in chains:
            z_ref[ch] = _dot_nt(qs_ref[ch], k_ref[0, pl.ds(ks, tk), ch * LANES:(ch + 1) * LANES])
        for ch in chains:
            for r0 in range(0, R, SB_STRIP):
                rs = slice(r0, r0 + SB_STRIP)
                z = z_ref[ch, rs, :]
                sp = jnp.maximum(z, 0.0) + jnp.log2(1.0 + jnp.exp2(-jnp.abs(z)))
                if masked:
                    sp = jnp.where(strict_mask(r0, sp.shape), sp, 0.0)
                sp_ref[ch, rs, :] = sp.astype(BF16)
        for ch in chains:
            in_ref[ch] = _dot(sp_ref[ch], u2_ref[...])
        for ch in chains:
            for r0 in range(0, R, SB_STRIP):
                rs = slice(r0, r0 + SB_STRIP)
                inc = in_ref[ch, rs, :]
                a = jnp.exp2(z_ref[ch, rs, :] - c_ref[ch, rs, :] - inc)
                if masked:
                    a = jnp.where(strict_mask(r0, a.shape), a, 0.0)
                a_ref[ch, rs, :] = a.astype(BF16)
                c_ref[ch, rs, :] += inc[:, 0:1]
        for ch in chains:
            acc_ref[ch] += _dot(a_ref[ch], v_ref[0, pl.ds(ks, tk), ch * LANES:(ch + 1) * LANES])
        return jnp.min(c_ref[...])

    cmin = step(i, True)

    def cond(carry):
        j, cmin = carry
        return (j >= 0) & (cmin < -SB_DEAD)

    def body(carry):
        j, _ = carry
        return j - 1, step(j, False)

    lax.while_loop(cond, body, (i - 1, cmin))
    for ch in chains:
        acc = acc_ref[ch]
        o_ref[0, :, ch * LANES:(ch + 1) * LANES] = jnp.where(
            _head_lane_mask(0, (tq, LANES)), acc[:tq], acc[tq:]).astype(BF16)


def _sb_attention(q, k, v, tq):
    B, L, _ = q.shape
    n_groups = N_ATT_HEADS // (2 * SB_CHAINS)
    R = 2 * tq
    W = SB_CHAINS * LANES
    r = np.arange(tq)
    u = (r[:, None] >= r[None, :]).astype(np.float32)
    u2 = jnp.asarray(u, BF16)
    qspec = pl.BlockSpec((1, tq, W), lambda b, p, i: (b, i, p))
    kspec = pl.BlockSpec((1, L, W), lambda b, p, i: (b, 0, p))
    return pl.pallas_call(
        functools.partial(_sb_kernel, tq=tq),
        grid=(B, n_groups, L // tq),
        in_specs=[qspec, kspec, kspec, pl.BlockSpec((tq, tq), lambda b, p, i: (0, 0))],
        out_specs=qspec,
        out_shape=jax.ShapeDtypeStruct((B, L, ATT_W), BF16),
        scratch_shapes=[pltpu.VMEM((SB_CHAINS, R, LANES), BF16), pltpu.VMEM((SB_CHAINS, R, tq), F32),
                        pltpu.VMEM((SB_CHAINS, R, tq), BF16),
                        pltpu.VMEM((SB_CHAINS, R, tq), F32),
                        pltpu.VMEM((SB_CHAINS, R, tq), BF16), pltpu.VMEM((SB_CHAINS, R, 1), F32),
                        pltpu.VMEM((SB_CHAINS, R, LANES), F32)],
        compiler_params=_params(("arbitrary", "arbitrary", "arbitrary")),
        name="sb_attention",
    )(q, k, v, u2)


def _gla_tables():
    Q = CHUNK
    r = np.arange(Q)
    j = np.arange(Q)
    coef = [(j[None, :] <= r[:, None]), (j[None, :] > r[:, None])]
    masks = [np.eye(Q, dtype=bool)]
    for lvl in range(GLA_LEVELS):
        m = 1 << lvl
        c0 = (r // (2 * m)) * (2 * m)
        mid = c0 + m - 1
        upper = (r - c0) >= m
        up = (j[None, :] > mid[:, None]) & (j[None, :] <= r[:, None])
        lowr = (j[None, :] > r[:, None]) & (j[None, :] <= mid[:, None])
        coef.append(np.where(upper[:, None], up, lowr))
        masks.append((c0[:, None] == c0[None, :]) & upper[:, None] & (~upper)[None, :])
    coef = np.concatenate(coef, axis=0).astype(np.float32)
    coef2 = np.concatenate([coef, coef], axis=1)
    masks = np.stack(masks).astype(np.float32)
    hv = np.arange(GLA_V) // GLA_DV
    hk = np.arange(GLA_QK) // GLA_DK
    bdiag = (hv[:, None] == hk[None, :]).astype(np.float32)
    return jnp.asarray(coef2, BF16), jnp.asarray(masks, F32), jnp.asarray(bdiag, F32)


def _gla_kernel(gq_ref, gk_ref, gv_ref, glow_ref, gr_ref, w2_ref, gb_ref, coef_ref, mask_ref,
                bdiag_ref, nw_ref, o_ref, st_ref):
    @pl.when(pl.program_id(0) == 0)
    def _():
        st_ref[...] = jnp.zeros_like(st_ref)

    for b in range(gq_ref.shape[0]):
        _gla_chunk(gq_ref.at[b], gk_ref.at[b], gv_ref.at[b], glow_ref.at[b], gr_ref.at[b], w2_ref,
                   gb_ref, coef_ref, mask_ref, bdiag_ref, nw_ref, o_ref.at[b], st_ref.at[b])


def _gla_chunk(gq_ref, gk_ref, gv_ref, glow_ref, gr_ref, w2_ref, gb_ref, coef_ref, mask_ref,
               bdiag_ref, nw_ref, o_ref, st_ref):
    Q = CHUNK
    logits = _dot(glow_ref[...].astype(BF16), w2_ref[...]) + gb_ref[...]
    la = _log_sigmoid(logits) * (1.0 / GLA_GATE_NORM)
    hi, lo = _split2(la)
    expo = _dot(coef_ref[...], jnp.concatenate([hi, lo], axis=0))
    q = gq_ref[...]
    k = gk_ref[...]
    v = gv_ref[...]
    lane_head = lax.broadcasted_iota(jnp.int32, (Q, GLA_QK), 1) // GLA_DK
    row = lax.broadcasted_iota(jnp.int32, (Q, GLA_QK), 0)
    hmask = [lane_head == h for h in range(GLA_HEADS)]

    att = [None] * GLA_HEADS
    for lvl in range(-1, GLA_LEVELS):
        if lvl < 0:
            xq, xk = q, k.astype(BF16)
        else:
            m = 1 << lvl
            upper = (row & (2 * m - 1)) >= m
            xq = jnp.where(upper, q, k) * jnp.exp(expo[(2 + lvl) * Q:(3 + lvl) * Q, :])
            xk = xq.astype(BF16)
        msk = mask_ref[lvl + 1]
        lhs = jnp.concatenate([jnp.where(hmask[h], xq, 0.0) for h in range(GLA_HEADS)], axis=0)
        prod = _dot_nt(lhs.astype(BF16), xk)
        for h in range(GLA_HEADS):
            part = prod[h * Q:(h + 1) * Q] * msk
            att[h] = part if att[h] is None else att[h] + part

    st = st_ref[...]
    q_in = (q * jnp.exp(expo[0:Q, :])).astype(BF16)
    o = _dot_nt(q_in, st.astype(BF16))
    o_intra = [_dot(att[h].astype(BF16), v[:, h * GLA_DV:(h + 1) * GLA_DV].astype(BF16))
               for h in range(GLA_HEADS)]
    o = o + jnp.concatenate(o_intra, axis=1)

    k_dec = (k * jnp.exp(expo[Q:2 * Q, :])).astype(BF16)
    upd = _dot(v.T.astype(BF16), k_dec)
    g_last = expo[Q - 1:Q, :]
    st_ref[...] = st * jnp.exp(g_last) + upd * bdiag_ref[...]

    gr = gr_ref[...]
    o = jnp.concatenate([_rms(o[:, h * GLA_DV:(h + 1) * GLA_DV], nw_ref[...])
                         for h in range(GLA_HEADS)], axis=1)
    o_ref[...] = (o * _silu(gr)).astype(BF16)


def _gla(gq, gk, gv, glow, gr, gate_w2, gate_b, norm_w):
    B, L, _ = gq.shape
    Q = CHUNK
    coef2, masks, bdiag = _gla_tables()
    w2 = jnp.zeros((LANES, GLA_QK), F32).at[:GLA_RANK].set(gate_w2).astype(BF16)
    row = lambda width: pl.BlockSpec((B, Q, width), lambda c: (0, c, 0))
    const = lambda shape: pl.BlockSpec(shape, lambda c: (0,) * len(shape))
    return pl.pallas_call(
        _gla_kernel,
        grid=(L // Q,),
        in_specs=[row(GLA_QK), row(GLA_QK), row(GLA_V), row(LANES), row(GLA_V),
                  const((LANES, GLA_QK)), const((1, GLA_QK)), const(coef2.shape),
                  const(masks.shape), const(bdiag.shape), const((1, GLA_DV))],
        out_specs=row(GLA_V),
        out_shape=jax.ShapeDtypeStruct((B, L, GLA_V), BF16),
        scratch_shapes=[pltpu.VMEM((B, GLA_V, GLA_QK), F32)],
        compiler_params=_params(("arbitrary",)),
        name="gla_scan",
    )(gq, gk, gv, glow, gr, w2, gate_b.reshape(1, -1), coef2, masks, bdiag,
      norm_w.reshape(1, -1))


def _block(n, want):
    return want if n % want == 0 else n


def kernel(x, norm_mix, norm_mlp, norm_final, w_in_ab, fox_f_bias, ssd_conv_w, ssd_conv_b,
           ssd_dt_bias, ssd_a_log, ssd_d, ssd_norm, w_out_ab, w_in_cd, gla_gate_w2,
           gla_gate_b, gla_norm, w_out_cd, w_mlp_up, w_mlp_down):
    B, L, D = x.shape
    assert L % CHUNK == 0
    tm = _block(L, 512)
    tq = _block(L, 256)
    tq_fox = _block(L, 512)

    assert tm == tq_fox
    head_order = jnp.argsort(fox_f_bias[0])
    head_cols = (head_order[:, None] * HEAD_DIM + jnp.arange(HEAD_DIM)[None, :]).reshape(-1)
    qd, kd, vd, qa, ka, z, xbc, dt, fend = _inproj_ab(x, norm_mix[0], w_in_ab[0], fox_f_bias[0],
                                                      ssd_dt_bias[0], head_order, head_cols, tm)
    w_out0 = jnp.concatenate([jnp.take(w_out_ab[0, :ATT_W], head_cols, axis=0),
                              w_out_ab[0, ATT_W:]], axis=0)
    y_fox = _fox_attention(qd, qa, kd, ka, vd, fend.reshape(B, L // tm, LANES), tq_fox)
    y_ssd = _ssd(xbc, z, dt, ssd_conv_w[0], ssd_conv_b[0], ssd_a_log[0], ssd_d[0], ssd_norm[0])
    w_up = w_mlp_up.astype(BF16)
    w_down = w_mlp_down.astype(BF16)
    h = _mix_mlp(x, y_fox, y_ssd, w_out0, norm_mlp[0], w_up, w_down, 0, norm_final, tm,
                 final=False)

    sq, sk, sv, gq, gk, gv, gr, glow = _inproj_cd(h, norm_mix[1], w_in_cd[0], tm)
    y_sb = _sb_attention(sq, sk, sv, tq)
    y_gla = _gla(gq, gk, gv, glow, gr, gla_gate_w2[0], gla_gate_b[0], gla_norm[0])
    return _mix_mlp(h, y_sb, y_gla, w_out_cd[0], norm_mlp[1], w_up, w_down, 1, norm_final, tm,
                    final=True)
```

```python
import functools
import math

import numpy as np
import jax
import jax.numpy as jnp
from jax import lax
from jax.experimental import pallas as pl
from jax.experimental.pallas import tpu as pltpu

F32 = jnp.float32
BF16 = jnp.bfloat16

LANES = 128
SUBLANES = 8
VMEM_LIMIT_BYTES = 56 * 1024 * 1024

HEAD_DIM = 64
N_ATT_HEADS = 8
ATT_W = N_ATT_HEADS * HEAD_DIM
SSD_HEADS = 8
SSD_W = 512
SSD_GROUPS = 2
SSD_STATE = 64
SSD_CONV = 4
SSD_CONV_DIM = SSD_W + 2 * SSD_GROUPS * SSD_STATE
GLA_HEADS = 4
GLA_DK = 64
GLA_DV = 128
GLA_RANK = 16
GLA_GATE_NORM = 16.0
EPS = 1e-5
CHUNK = 128
GLA_LEVELS = 7
LOG2E = math.log2(math.e)
DT_LANE0 = 8


def _params(sem):
    return pltpu.CompilerParams(dimension_semantics=sem, vmem_limit_bytes=VMEM_LIMIT_BYTES)


def _split3(x):
    hi = x.astype(BF16)
    r = x - hi.astype(F32)
    mid = r.astype(BF16)
    lo = (r - mid.astype(F32)).astype(BF16)
    return hi, mid, lo


def _split2(x):
    hi = x.astype(BF16)
    lo = (x - hi.astype(F32)).astype(BF16)
    return hi, lo


def _softplus(x):
    return jnp.maximum(x, 0.0) + jnp.log1p(jnp.exp(-jnp.abs(x)))


def _log_sigmoid(x):
    return jnp.minimum(x, 0.0) - jnp.log1p(jnp.exp(-jnp.abs(x)))


def _silu(x):
    h = 0.5 * x
    return h + h * jnp.tanh(h)


def _rms(x, w):
    ms = jnp.mean(x * x, axis=-1, keepdims=True)
    return x * lax.rsqrt(ms + EPS) * w


def _dot(a, b):
    return jnp.dot(a, b, preferred_element_type=F32)


def _dot_nt(a, b):
    return lax.dot_general(a, b, (((1,), (1,)), ((), ())), preferred_element_type=F32)


AB_COLS = 3 * ATT_W + SSD_W + SSD_CONV_DIM + LANES


def _inproj_ab_kernel(x_ref, nw_ref, w_ref, sb_ref, tri3_ref, sel_ref, cst_ref,
                      qd_ref, kd_ref, vd_ref, qa_ref, ka_ref, z_ref, xbc_ref, dt_ref, fend_ref,
                      carry_ref):
    @pl.when(pl.program_id(1) == 0)
    def _():
        carry_ref[...] = jnp.zeros_like(carry_ref)

    u = _rms(x_ref[0], nw_ref[...]).astype(BF16)

    def mm(a, b):
        return _dot(u, w_ref[:, a:b])

    o = 0
    qd_ref[0] = (mm(o, o + ATT_W) * (LOG2E * HEAD_DIM ** -0.5)).astype(BF16); o += ATT_W
    kd_ref[0] = mm(o, o + ATT_W).astype(BF16); o += ATT_W
    vd_ref[0] = mm(o, o + ATT_W).astype(BF16); o += ATT_W
    z_ref[0] = mm(o, o + SSD_W); o += SSD_W
    xbc_ref[0] = mm(o, o + SSD_CONV_DIM); o += SSD_CONV_DIM
    small = mm(o, o + LANES) + sb_ref[...]

    lane = lax.broadcasted_iota(jnp.int32, small.shape, 1)
    is_f = lane < N_ATT_HEADS
    is_dt = (lane >= DT_LANE0) & (lane < DT_LANE0 + SSD_HEADS)
    log_f = jnp.where(is_f, _log_sigmoid(small), 0.0)
    dt_ref[0] = jnp.where(is_dt, _softplus(small), 0.0)

    hi, mid, lo = _split3(log_f)
    tm = log_f.shape[0]
    carry = carry_ref[...]
    cums = []
    for r0 in range(0, tm, CHUNK):
        rs = slice(r0, r0 + CHUNK)
        piece = _dot(tri3_ref[...], jnp.concatenate([hi[rs], mid[rs], lo[rs]], axis=0)) + carry
        carry = piece[CHUNK - 1:CHUNK, :]
        cums.append(piece)
    cum = jnp.concatenate(cums, axis=0)
    carry_ref[...] = carry

    cum2 = cum * LOG2E
    fend_ref[0, 0] = cum2[tm - 1:tm, :]
    fh, fm, fl = _split3(cum2)
    fcat = (fh.astype(F32) + pltpu.roll(fm.astype(F32), N_ATT_HEADS, 1)
            + pltpu.roll(fl.astype(F32), 2 * N_ATT_HEADS, 1)).astype(BF16)
    aug = _dot(fcat, sel_ref[...]) + cst_ref[...]
    qa_ref[0] = aug[:, :ATT_W].astype(BF16)
    ka_ref[0] = aug[:, ATT_W:].astype(BF16)


def _aug_tables():
    sel = np.zeros((LANES, 2 * ATT_W), np.float32)
    cst = np.zeros((1, 2 * ATT_W), np.float32)
    for h in range(N_ATT_HEADS):
        p, e = divmod(h, 2)
        for part in range(3):
            src = part * N_ATT_HEADS + h
            sel[src, p * LANES + 6 * e + part] = 1.0
            sel[src, ATT_W + p * LANES + 6 * e + 3 + part] = -1.0
            cst[0, p * LANES + 6 * e + 3 + part] = 1.0
            cst[0, ATT_W + p * LANES + 6 * e + part] = 1.0
    return jnp.asarray(sel, BF16), jnp.asarray(cst, F32)


def _tri_incl(n):
    r = np.arange(n)
    return (r[None, :] <= r[:, None]).astype(np.float32)


def _inproj_ab(h, nw, w_in, f_bias, dt_bias, head_order, head_cols, tm):
    B, L, D = h.shape
    fq, fk, fv, fl, wz, wxbc, wdt = jnp.split(w_in, np.cumsum(
        [ATT_W, ATT_W, ATT_W, N_ATT_HEADS, SSD_W, SSD_CONV_DIM])[:].tolist(), axis=1)
    fq, fk, fv = (jnp.take(w, head_cols, axis=1) for w in (fq, fk, fv))
    fl = jnp.take(fl, head_order, axis=1)
    f_bias = f_bias[head_order]
    pad = jnp.zeros((D, LANES - N_ATT_HEADS - SSD_HEADS), w_in.dtype)
    w = jnp.concatenate([fq, fk, fv, wz, wxbc, fl, wdt, pad], axis=1).astype(BF16)
    sb = jnp.concatenate([f_bias, dt_bias, jnp.zeros((LANES - 16,), F32)]).reshape(1, LANES)
    assert tm % CHUNK == 0
    tri = _tri_incl(CHUNK)
    tri3 = jnp.asarray(np.concatenate([tri, tri, tri], axis=1), BF16)
    sel, cst = _aug_tables()

    row = lambda width: pl.BlockSpec((1, tm, width), lambda b, i: (b, i, 0))
    const = lambda shape: pl.BlockSpec(shape, lambda b, i: (0,) * len(shape))
    outs = [
        jax.ShapeDtypeStruct((B, L, ATT_W), BF16),
        jax.ShapeDtypeStruct((B, L, ATT_W), BF16),
        jax.ShapeDtypeStruct((B, L, ATT_W), BF16),
        jax.ShapeDtypeStruct((B, L, ATT_W), BF16),
        jax.ShapeDtypeStruct((B, L, ATT_W), BF16),
        jax.ShapeDtypeStruct((B, L, SSD_W), F32),
        jax.ShapeDtypeStruct((B, L, SSD_CONV_DIM), F32),
        jax.ShapeDtypeStruct((B, L, LANES), F32),
        jax.ShapeDtypeStruct((B, L // tm, 1, LANES), F32),
    ]
    return pl.pallas_call(
        _inproj_ab_kernel,
        grid=(B, L // tm),
        in_specs=[row(D), const((1, D)), const((D, AB_COLS)), const((1, LANES)),
                  const((CHUNK, 3 * CHUNK)), const((LANES, 2 * ATT_W)), const((1, 2 * ATT_W))],
        out_specs=[row(ATT_W)] * 5 + [row(SSD_W), row(SSD_CONV_DIM), row(LANES),
                   pl.BlockSpec((1, 1, 1, LANES), lambda b, i: (b, i, 0, 0))],
        out_shape=outs,
        scratch_shapes=[pltpu.VMEM((1, LANES), F32)],
        compiler_params=_params(("arbitrary", "arbitrary")),
        name="inproj_ab",
    )(h, nw.reshape(1, D), w, sb, tri3, sel, cst)


def _head_lane_mask(e, shape):
    lane = lax.broadcasted_iota(jnp.int32, shape, 1)
    return (lane < HEAD_DIM) if e == 0 else (lane >= HEAD_DIM)


EXP2_DEAD = -151.0
FOX_STRIP = 64


def _fox_kernel(qd_ref, qa_ref, kd_ref, ka_ref, v_ref, fend_ref, o_ref,
                q_ref, s0_ref, s1_ref, p0_ref, p1_ref, m_ref, al_ref, acc_ref, kn_ref, *, tk):
    it = pl.program_id(2)
    ka = 2 * it
    R = 4 * tk
    lower = slice(2 * tk, R)
    everything = slice(0, R)
    qd = qd_ref[0].astype(F32)
    qa = qa_ref[0].astype(F32)
    lane = lax.broadcasted_iota(jnp.int32, (tk, LANES), 1)
    first = lane < HEAD_DIM
    for half in range(2):
        hs = slice(half * tk, (half + 1) * tk)
        for e in range(2):
            amask = (lane >= 6 * e) & (lane < 6 * e + 6)
            g = 2 * half + e
            q_ref[g * tk:(g + 1) * tk, :] = jnp.concatenate(
                [jnp.where(first if e == 0 else ~first, qd[hs], 0.0),
                 jnp.where(amask, qa[hs], 0.0)], axis=1).astype(BF16)
    m_ref[...] = jnp.full(m_ref.shape, -jnp.inf, F32)
    acc_ref[...] = jnp.zeros(acc_ref.shape, F32)
    ones = jnp.ones((tk, LANES), BF16)

    def head_sq_norms(x):
        lane_x = lax.broadcasted_iota(jnp.int32, x.shape, 1)
        sq = x * x
        return (jnp.sum(jnp.where(lane_x < HEAD_DIM, sq, 0.0), axis=1, keepdims=True),
                jnp.sum(jnp.where(lane_x < HEAD_DIM, 0.0, sq), axis=1, keepdims=True))

    @pl.when(it == 0)
    def _():
        def kbody(c, carry):
            ks = pl.multiple_of(c * tk, tk)
            n0, n1 = head_sq_norms(kd_ref[0, pl.ds(ks, tk), :].astype(F32))
            return jnp.maximum(carry[0], n0), jnp.maximum(carry[1], n1)
        zero = jnp.zeros((tk, 1), F32)
        n0, n1 = lax.fori_loop(0, kd_ref.shape[1] // tk, kbody, (zero, zero))
        for e, n in enumerate((n0, n1)):
            kn_ref[e] = jnp.broadcast_to(jnp.sqrt(jnp.max(n, axis=0, keepdims=True)), kn_ref.shape[1:])

    def scores(j, s_ref, rows=everything):
        ks = pl.multiple_of(j * tk, tk)
        k = jnp.concatenate([kd_ref[0, pl.ds(ks, tk), :], ka_ref[0, pl.ds(ks, tk), :]], axis=1)
        s_ref[rows, :] = _dot_nt(q_ref[rows, :], k)

    def update(j, s_ref, p_ref, rows=everything, diag_rows=0):
        for r0 in range(rows.start, rows.stop, FOX_STRIP):
            rs = slice(r0, r0 + FOX_STRIP)
            s = s_ref[rs, :]
            if r0 < diag_rows:
                rws = lax.broadcasted_iota(jnp.int32, s.shape, 0) + (r0 % tk)
                cols = lax.broadcasted_iota(jnp.int32, s.shape, 1)
                s = jnp.where(rws >= cols, s, -jnp.inf)
            m_old = m_ref[rs, :]
            m_new = jnp.maximum(m_old, jnp.max(s, axis=1, keepdims=True))
            al_ref[rs, :] = jnp.exp2(m_old - m_new)
            m_ref[rs, :] = m_new
            p_ref[rs, :] = jnp.exp2(s - m_new).astype(BF16)
        ks = pl.multiple_of(j * tk, tk)
        v1 = jnp.concatenate([v_ref[0, pl.ds(ks, tk), :], ones], axis=1)
        acc_ref[rows, :] = al_ref[rows, :] * acc_ref[rows, :] + _dot(p_ref[rows, :], v1)

    scores(ka + 1, s0_ref, lower)
    scores(ka, s1_ref)
    update(ka + 1, s0_ref, p0_ref, rows=lower, diag_rows=R)
    scores(jnp.maximum(ka - 1, 0), s0_ref)
    update(ka, s1_ref, p1_ref, diag_rows=2 * tk)

    qn = head_sq_norms(qd)
    fend = fend_ref[0]
    hl = lax.broadcasted_iota(jnp.int32, fend.shape, 1)
    jcol = lax.broadcasted_iota(jnp.int32, (fend.shape[0], 1), 0)
    jmin = ka
    for e in range(2):
        head = 2 * pl.program_id(1) + e
        fe = jnp.sum(jnp.where(hl == head, fend, 0.0), axis=1, keepdims=True)
        f_top = jnp.sum(jnp.where(jcol == ka - 1, fe, 0.0), axis=0, keepdims=True)
        qk = jnp.sqrt(jnp.max(qn[e], axis=0, keepdims=True)) * kn_ref[e, 0:1, 0:1]
        m_min = jnp.minimum(jnp.min(m_ref[e * tk:(e + 1) * tk, :], axis=0, keepdims=True),
                            jnp.min(m_ref[(2 + e) * tk:(3 + e) * tk, :], axis=0, keepdims=True))
        live = (jcol < ka) & (qk + f_top - fe + 1.0 - m_min > EXP2_DEAD)
        jmin = jnp.minimum(jmin, jnp.min(jnp.where(live, jcol, ka)))
    nb = ka - jmin

    def pair(u, carry):
        j = ka - 1 - 2 * u
        scores(j - 1, s1_ref)
        update(j, s0_ref, p0_ref)
        scores(jnp.maximum(j - 2, 0), s0_ref)
        update(j - 1, s1_ref, p1_ref)
        return carry

    lax.fori_loop(0, nb // 2, pair, 0)

    @pl.when(nb % 2 == 1)
    def _():
        update(jmin, s0_ref, p0_ref)

    acc = acc_ref[...]
    out = acc[:, :LANES] / acc[:, LANES:]
    for half in range(2):
        g = 2 * half
        o_ref[0, half * tk:(half + 1) * tk, :] = jnp.where(
            first, out[g * tk:(g + 1) * tk], out[(g + 1) * tk:(g + 2) * tk]).astype(BF16)


def _fox_attention(qd, qa, kd, ka, v, fend, tk):
    B, L, _ = qd.shape
    assert fend.shape == (B, L // tk, LANES)
    n_pairs = N_ATT_HEADS // 2
    tq = 2 * tk
    R = 2 * tq
    qspec = pl.BlockSpec((1, tq, LANES), lambda b, p, i: (b, i, p))
    kspec = pl.BlockSpec((1, L, LANES), lambda b, p, i: (b, 0, p))
    return pl.pallas_call(
        functools.partial(_fox_kernel, tk=tk),
        grid=(B, n_pairs, L // tq),
        in_specs=[qspec, qspec, kspec, kspec, kspec,
                  pl.BlockSpec((1, L // tk, LANES), lambda b, p, i: (b, 0, 0))],
        out_specs=qspec,
        out_shape=jax.ShapeDtypeStruct((B, L, ATT_W), BF16),
        scratch_shapes=[pltpu.VMEM((R, 2 * LANES), BF16),
                        pltpu.VMEM((R, tk), F32), pltpu.VMEM((R, tk), F32),
                        pltpu.VMEM((R, tk), BF16), pltpu.VMEM((R, tk), BF16),
                        pltpu.VMEM((R, 1), F32), pltpu.VMEM((R, 1), F32),
                        pltpu.VMEM((R, 2 * LANES), F32),
                        pltpu.VMEM((2, SUBLANES, LANES), F32)],
        compiler_params=_params(("arbitrary", "arbitrary", "arbitrary")),
        name="fox_attention",
    )(qd, qa, kd, ka, v, fend)


def _ssd_kernel(xbc_ref, z_ref, dt_ref, cw_ref, cb_ref, alog_ref, dsk_ref, nw_ref, tri3_ref,
                y_ref, xext_ref, st_ref):
    @pl.when(pl.program_id(0) == 0)
    def _():
        xext_ref[:, 0:SUBLANES, :] = jnp.zeros((xext_ref.shape[0], SUBLANES, SSD_CONV_DIM), F32)
        st_ref[...] = jnp.zeros_like(st_ref)

    for b in range(xbc_ref.shape[0]):
        _ssd_chunk(xbc_ref.at[b], z_ref.at[b], dt_ref.at[b], cw_ref, cb_ref, alog_ref, dsk_ref,
                   nw_ref, tri3_ref, y_ref.at[b], xext_ref.at[b], st_ref.at[b])


def _ssd_chunk(xbc_ref, z_ref, dt_ref, cw_ref, cb_ref, alog_ref, dsk_ref, nw_ref, tri3_ref,
               y_ref, xext_ref, st_ref):
    Q = CHUNK
    xext_ref[SUBLANES:SUBLANES + Q, :] = xbc_ref[...]
    conv = cb_ref[...]
    for k in range(SSD_CONV):
        conv = conv + cw_ref[k:k + 1, :] * xext_ref[pl.ds(SUBLANES - (SSD_CONV - 1) + k, Q), :]
    xext_ref[0:SUBLANES, :] = xext_ref[Q:Q + SUBLANES, :]
    xc = _silu(conv)
    xs = xc[:, :SSD_W]
    bm = xc[:, SSD_W:SSD_W + LANES]
    cm = xc[:, SSD_W + LANES:]

    dt = dt_ref[...]
    a = dt * (-jnp.exp(alog_ref[...]))
    hi, mid, lo = _split3(a)
    a_cum = _dot(tri3_ref[...], jnp.concatenate([hi, mid, lo], axis=0))
    a_row = a_cum.T
    a_last_col = a_cum[Q - 1:Q, :]
    bm_t = bm.T

    rows = lax.broadcasted_iota(jnp.int32, (Q, Q), 0)
    cols = lax.broadcasted_iota(jnp.int32, (Q, Q), 1)
    causal = rows >= cols
    lane = lax.broadcasted_iota(jnp.int32, (Q, LANES), 1)
    first_half = lane < HEAD_DIM

    scores = []
    cmask = []
    for g in range(SSD_GROUPS):
        cg = jnp.where(_head_lane_mask(g, (Q, LANES)), cm, 0.0).astype(BF16)
        cmask.append(cg)
        scores.append(_dot_nt(cg, bm.astype(BF16)))

    y_pairs = []
    heads_per_group = SSD_HEADS // SSD_GROUPS
    for p in range(SSD_HEADS // 2):
        xs_pair = xs[:, p * LANES:(p + 1) * LANES]
        dt_pair = jnp.where(first_half, dt[:, DT_LANE0 + 2 * p:DT_LANE0 + 2 * p + 1],
                            dt[:, DT_LANE0 + 2 * p + 1:DT_LANE0 + 2 * p + 2])
        xdt = (xs_pair * dt_pair).astype(BF16)
        y_head = []
        for e in range(2):
            h = 2 * p + e
            g = h // heads_per_group
            hl = DT_LANE0 + h
            acol = a_cum[:, hl:hl + 1]
            arow = a_row[hl:hl + 1, :]
            alast = a_last_col[:, hl:hl + 1]
            lmat = jnp.exp(jnp.where(causal, acol - arow, -jnp.inf))
            y_diag = _dot((scores[g] * lmat).astype(BF16), xdt)
            prev = st_ref[h]
            y_off = _dot(cmask[g], prev.astype(BF16)) * jnp.exp(acol)
            y_head.append(y_diag + y_off)
            decay_row = jnp.exp(alast - arow)
            local = _dot((bm_t * decay_row).astype(BF16), xdt)
            st_ref[h] = prev * jnp.exp(alast) + local
        y_pairs.append(jnp.where(first_half, y_head[0], y_head[1]))
    y = jnp.concatenate(y_pairs, axis=1) + dsk_ref[...] * xs
    y = y * _silu(z_ref[...])
    gw = SSD_W // SSD_GROUPS
    y = jnp.concatenate([_rms(y[:, g * gw:(g + 1) * gw], nw_ref[:, g * gw:(g + 1) * gw])
                         for g in range(SSD_GROUPS)], axis=1)
    y_ref[...] = y.astype(BF16)


def _ssd(xbc, z, dt, conv_w, conv_b, a_log, d_skip, norm_w):
    B, L, _ = xbc.shape
    Q = CHUNK
    alog = jnp.zeros((1, LANES), F32).at[0, DT_LANE0:DT_LANE0 + SSD_HEADS].set(a_log)
    dsk = jnp.repeat(d_skip, SSD_W // SSD_HEADS).reshape(1, SSD_W)
    tri = _tri_incl(Q)
    tri3 = jnp.asarray(np.concatenate([tri, tri, tri], axis=1), BF16)
    row = lambda width: pl.BlockSpec((B, Q, width), lambda c: (0, c, 0))
    const = lambda shape: pl.BlockSpec(shape, lambda c: (0,) * len(shape))
    return pl.pallas_call(
        _ssd_kernel,
        grid=(L // Q,),
        in_specs=[row(SSD_CONV_DIM), row(SSD_W), row(LANES), const((SSD_CONV, SSD_CONV_DIM)),
                  const((1, SSD_CONV_DIM)), const((1, LANES)), const((1, SSD_W)),
                  const((1, SSD_W)), const((Q, 3 * Q))],
        out_specs=row(SSD_W),
        out_shape=jax.ShapeDtypeStruct((B, L, SSD_W), BF16),
        scratch_shapes=[pltpu.VMEM((B, Q + SUBLANES, SSD_CONV_DIM), F32),
                        pltpu.VMEM((B, SSD_HEADS, LANES, LANES), F32)],
        compiler_params=_params(("arbitrary",)),
        name="ssd_scan",
    )(xbc, z, dt, conv_w, conv_b.reshape(1, -1), alog, dsk, norm_w.reshape(1, -1), tri3)


def _mix_mlp_kernel(h_ref, y1_ref, y2_ref, wo_ref, nw_ref, wup_ref, wdn_ref, nf_ref, o_ref,
                    *, ff_chunk, final):
    half = y1_ref.shape[-1]
    mix = _dot(y1_ref[...], wo_ref[0:half, :]) + _dot(y2_ref[...], wo_ref[half:2 * half, :])
    h1 = h_ref[...] + mix
    u = _rms(h1, nw_ref[...]).astype(BF16)
    d_ff = wup_ref.shape[1]
    acc = jnp.zeros_like(h1)
    for c in range(d_ff // ff_chunk):
        act = jnp.maximum(_dot(u, wup_ref[:, c * ff_chunk:(c + 1) * ff_chunk]), 0.0)
        acc = acc + _dot((act * act).astype(BF16), wdn_ref[c * ff_chunk:(c + 1) * ff_chunk, :])
    h2 = h1 + acc
    if final:
        h2 = _rms(h2, nf_ref[...])
    o_ref[...] = h2


def _mix_mlp(h, y1, y2, w_out, nw, w_up, w_down, layer, nf, tm, final):
    B, L, D = h.shape
    T = B * L
    d_ff = w_up.shape[2]
    half = y1.shape[-1]
    row = lambda width: pl.BlockSpec((tm, width), lambda i: (i, 0))
    const = lambda shape: pl.BlockSpec(shape, lambda i: (0,) * len(shape))
    stacked = lambda shape: pl.BlockSpec((None,) + shape, lambda i: (layer, 0, 0))
    out = pl.pallas_call(
        functools.partial(_mix_mlp_kernel, ff_chunk=min(1024, d_ff), final=final),
        grid=(T // tm,),
        in_specs=[row(D), row(half), row(half), const((2 * half, D)), const((1, D)),
                  stacked((D, d_ff)), stacked((d_ff, D)), const((1, D))],
        out_specs=row(D),
        out_shape=jax.ShapeDtypeStruct((T, D), F32),
        compiler_params=_params(("arbitrary",)),
        name="mix_mlp",
    )(h.reshape(T, D), y1.reshape(T, half), y2.reshape(T, half), w_out.astype(BF16),
      nw.reshape(1, D), w_up, w_down, nf.reshape(1, D))
    return out.reshape(B, L, D)


GLA_QK = GLA_HEADS * GLA_DK
GLA_V = GLA_HEADS * GLA_DV
CD_COLS = 3 * ATT_W + 2 * GLA_QK + 2 * GLA_V + LANES


def _inproj_cd_kernel(x_ref, nw_ref, w_ref, sq_ref, sk_ref, sv_ref, gq_ref, gk_ref, gv_ref,
                      gr_ref, glow_ref):
    u = _rms(x_ref[...], nw_ref[...]).astype(BF16)

    def mm(a, b):
        return _dot(u, w_ref[:, a:b])

    o = 0
    sq_ref[...] = (mm(o, o + ATT_W) * (LOG2E * HEAD_DIM ** -0.5)).astype(BF16); o += ATT_W
    sk_ref[...] = mm(o, o + ATT_W).astype(BF16); o += ATT_W
    sv_ref[...] = mm(o, o + ATT_W).astype(BF16); o += ATT_W
    gq_ref[...] = mm(o, o + GLA_QK) * (GLA_DK ** -0.5); o += GLA_QK
    gk_ref[...] = mm(o, o + GLA_QK); o += GLA_QK
    gv_ref[...] = mm(o, o + GLA_V); o += GLA_V
    gr_ref[...] = mm(o, o + GLA_V); o += GLA_V
    glow_ref[...] = mm(o, o + LANES)


def _inproj_cd(h, nw, w_in, tm):
    B, L, D = h.shape
    T = B * L
    sq, sk, sv, gq, gk, gv, glow, gr = jnp.split(w_in, np.cumsum(
        [ATT_W, ATT_W, ATT_W, GLA_QK, GLA_QK, GLA_V, GLA_RANK]).tolist(), axis=1)
    pad = jnp.zeros((D, LANES - GLA_RANK), w_in.dtype)
    w = jnp.concatenate([sq, sk, sv, gq, gk, gv, gr, glow, pad], axis=1).astype(BF16)
    row = lambda width: pl.BlockSpec((tm, width), lambda i: (i, 0))
    const = lambda shape: pl.BlockSpec(shape, lambda i: (0,) * len(shape))
    widths = [ATT_W, ATT_W, ATT_W, GLA_QK, GLA_QK, GLA_V, GLA_V, LANES]
    dtypes = [BF16, BF16, BF16, F32, F32, F32, F32, F32]
    outs = pl.pallas_call(
        _inproj_cd_kernel,
        grid=(T // tm,),
        in_specs=[row(D), const((1, D)), const((D, CD_COLS))],
        out_specs=[row(wd) for wd in widths],
        out_shape=[jax.ShapeDtypeStruct((T, wd), dt) for wd, dt in zip(widths, dtypes)],
        compiler_params=_params(("arbitrary",)),
        name="inproj_cd",
    )(h.reshape(T, D), nw.reshape(1, D), w)
    return [o.reshape(B, L, -1) for o in outs]


SB_DEAD = EXP2_DEAD - 1.0
SB_STRIP = 64
SB_CHAINS = 2


def _sb_kernel(q_ref, k_ref, v_ref, u2_ref, o_ref,
               qs_ref, z_ref, sp_ref, in_ref, a_ref, c_ref, acc_ref, *, tq):
    i = pl.program_id(2)
    tk = tq
    R = 2 * tq
    chains = range(SB_CHAINS)
    for ch in chains:
        qf = q_ref[0, :, ch * LANES:(ch + 1) * LANES].astype(F32)
        for e in range(2):
            qs_ref[ch, e * tq:(e + 1) * tq, :] = jnp.where(_head_lane_mask(e, (tq, LANES)), qf,
                                                           0.0).astype(BF16)
    c_ref[...] = jnp.zeros(c_ref.shape, F32)
    acc_ref[...] = jnp.zeros(acc_ref.shape, F32)

    def strict_mask(r0, shape):
        rows = lax.broadcasted_iota(jnp.int32, shape, 0) + (r0 % tq)
        return lax.broadcasted_iota(jnp.int32, shape, 1) < rows

    def step(j, masked):
        ks = pl.multiple_of(j * tk, tk)
        for ch in chains:
            z_ref[ch] = _dot_nt(qs_ref[ch], k_ref[0, pl.ds(ks, tk), ch * LANES:(ch + 1) * LANES])
        for ch in chains:
            for r0 in range(0, R, SB_STRIP):
                rs = slice(r0, r0 + SB_STRIP)
                z = z_ref[ch, rs, :]
                sp = jnp.maximum(z, 0.0) + jnp.log2(1.0 + jnp.exp2(-jnp.abs(z)))
                if masked:
                    sp = jnp.where(strict_mask(r0, sp.shape), sp, 0.0)
                sp_ref[ch, rs, :] = sp.astype(BF16)
        for ch in chains:
            in_ref[ch] = _dot(sp_ref[ch], u2_ref[...])
        for ch in chains:
            for r0 in range(0, R, SB_STRIP):
                rs = slice(r0, r0 + SB_STRIP)
                inc = in_ref[ch, rs, :]
                a = jnp.exp2(z_ref[ch, rs, :] - c_ref[ch, rs, :] - inc)
                if masked:
                    a = jnp.where(strict_mask(r0, a.shape), a, 0.0)
                a_ref[ch, rs, :] = a.astype(BF16)
                c_ref[ch, rs, :] += inc[:, 0:1]
        for ch in chains:
            acc_ref[ch] += _dot(a_ref[ch], v_ref[0, pl.ds(ks, tk), ch * LANES:(ch + 1) * LANES])
        return jnp.min(c_ref[...])

    cmin = step(i, True)

    def cond(carry):
        j, cmin = carry
        return (j >= 0) & (cmin < -SB_DEAD)

    def body(carry):
        j, _ = carry
        return j - 1, step(j, False)

    lax.while_loop(cond, body, (i - 1, cmin))
    for ch in chains:
        acc = acc_ref[ch]
        o_ref[0, :, ch * LANES:(ch + 1) * LANES] = jnp.where(
            _head_lane_mask(0, (tq, LANES)), acc[:tq], acc[tq:]).astype(BF16)


def _sb_attention(q, k, v, tq):
    B, L, _ = q.shape
    n_groups = N_ATT_HEADS // (2 * SB_CHAINS)
    R = 2 * tq
    W = SB_CHAINS * LANES
    r = np.arange(tq)
    u = (r[:, None] >= r[None, :]).astype(np.float32)
    u2 = jnp.asarray(u, BF16)
    qspec = pl.BlockSpec((1, tq, W), lambda b, p, i: (b, i, p))
    kspec = pl.BlockSpec((1, L, W), lambda b, p, i: (b, 0, p))
    return pl.pallas_call(
        functools.partial(_sb_kernel, tq=tq),
        grid=(B, n_groups, L // tq),
        in_specs=[qspec, kspec, kspec, pl.BlockSpec((tq, tq), lambda b, p, i: (0, 0))],
        out_specs=qspec,
        out_shape=jax.ShapeDtypeStruct((B, L, ATT_W), BF16),
        scratch_shapes=[pltpu.VMEM((SB_CHAINS, R, LANES), BF16), pltpu.VMEM((SB_CHAINS, R, tq), F32),
                        pltpu.VMEM((SB_CHAINS, R, tq), BF16),
                        pltpu.VMEM((SB_CHAINS, R, tq), F32),
                        pltpu.VMEM((SB_CHAINS, R, tq), BF16), pltpu.VMEM((SB_CHAINS, R, 1), F32),
                        pltpu.VMEM((SB_CHAINS, R, LANES), F32)],
        compiler_params=_params(("arbitrary", "arbitrary", "arbitrary")),
        name="sb_attention",
    )(q, k, v, u2)


def _gla_tables():
    Q = CHUNK
    r = np.arange(Q)
    j = np.arange(Q)
    coef = [(j[None, :] <= r[:, None]), (j[None, :] > r[:, None])]
    masks = [np.eye(Q, dtype=bool)]
    for lvl in range(GLA_LEVELS):
        m = 1 << lvl
        c0 = (r // (2 * m)) * (2 * m)
        mid = c0 + m - 1
        upper = (r - c0) >= m
        up = (j[None, :] > mid[:, None]) & (j[None, :] <= r[:, None])
        lowr = (j[None, :] > r[:, None]) & (j[None, :] <= mid[:, None])
        coef.append(np.where(upper[:, None], up, lowr))
        masks.append((c0[:, None] == c0[None, :]) & upper[:, None] & (~upper)[None, :])
    coef = np.concatenate(coef, axis=0).astype(np.float32)
    coef2 = np.concatenate([coef, coef], axis=1)
    masks = np.stack(masks).astype(np.float32)
    hv = np.arange(GLA_V) // GLA_DV
    hk = np.arange(GLA_QK) // GLA_DK
    bdiag = (hv[:, None] == hk[None, :]).astype(np.float32)
    return jnp.asarray(coef2, BF16), jnp.asarray(masks, F32), jnp.asarray(bdiag, F32)


def _gla_kernel(gq_ref, gk_ref, gv_ref, glow_ref, gr_ref, w2_ref, gb_ref, coef_ref, mask_ref,
                bdiag_ref, nw_ref, o_ref, st_ref):
    @pl.when(pl.program_id(0) == 0)
    def _():
        st_ref[...] = jnp.zeros_like(st_ref)

    for b in range(gq_ref.shape[0]):
        _gla_chunk(gq_ref.at[b], gk_ref.at[b], gv_ref.at[b], glow_ref.at[b], gr_ref.at[b], w2_ref,
                   gb_ref, coef_ref, mask_ref, bdiag_ref, nw_ref, o_ref.at[b], st_ref.at[b])


def _gla_chunk(gq_ref, gk_ref, gv_ref, glow_ref, gr_ref, w2_ref, gb_ref, coef_ref, mask_ref,
               bdiag_ref, nw_ref, o_ref, st_ref):
    Q = CHUNK
    logits = _dot(glow_ref[...].astype(BF16), w2_ref[...]) + gb_ref[...]
    la = _log_sigmoid(logits) * (1.0 / GLA_GATE_NORM)
    hi, lo = _split2(la)
    expo = _dot(coef_ref[...], jnp.concatenate([hi, lo], axis=0))
    q = gq_ref[...]
    k = gk_ref[...]
    v = gv_ref[...]
    lane_head = lax.broadcasted_iota(jnp.int32, (Q, GLA_QK), 1) // GLA_DK
    row = lax.broadcasted_iota(jnp.int32, (Q, GLA_QK), 0)
    hmask = [lane_head == h for h in range(GLA_HEADS)]

    att = [None] * GLA_HEADS
    for lvl in range(-1, GLA_LEVELS):
        if lvl < 0:
            xq, xk = q, k.astype(BF16)
        else:
            m = 1 << lvl
            upper = (row & (2 * m - 1)) >= m
            xq = jnp.where(upper, q, k) * jnp.exp(expo[(2 + lvl) * Q:(3 + lvl) * Q, :])
            xk = xq.astype(BF16)
        msk = mask_ref[lvl + 1]
        lhs = jnp.concatenate([jnp.where(hmask[h], xq, 0.0) for h in range(GLA_HEADS)], axis=0)
        prod = _dot_nt(lhs.astype(BF16), xk)
        for h in range(GLA_HEADS):
            part = prod[h * Q:(h + 1) * Q] * msk
            att[h] = part if att[h] is None else att[h] + part

    st = st_ref[...]
    q_in = (q * jnp.exp(expo[0:Q, :])).astype(BF16)
    o = _dot_nt(q_in, st.astype(BF16))
    o_intra = [_dot(att[h].astype(BF16), v[:, h * GLA_DV:(h + 1) * GLA_DV].astype(BF16))
               for h in range(GLA_HEADS)]
    o = o + jnp.concatenate(o_intra, axis=1)

    k_dec = (k * jnp.exp(expo[Q:2 * Q, :])).astype(BF16)
    upd = _dot(v.T.astype(BF16), k_dec)
    g_last = expo[Q - 1:Q, :]
    st_ref[...] = st * jnp.exp(g_last) + upd * bdiag_ref[...]

    gr = gr_ref[...]
    o = jnp.concatenate([_rms(o[:, h * GLA_DV:(h + 1) * GLA_DV], nw_ref[...])
                         for h in range(GLA_HEADS)], axis=1)
    o_ref[...] = (o * _silu(gr)).astype(BF16)


def _gla(gq, gk, gv, glow, gr, gate_w2, gate_b, norm_w):
    B, L, _ = gq.shape
    Q = CHUNK
    coef2, masks, bdiag = _gla_tables()
    w2 = jnp.zeros((LANES, GLA_QK), F32).at[:GLA_RANK].set(gate_w2).astype(BF16)
    row = lambda width: pl.BlockSpec((B, Q, width), lambda c: (0, c, 0))
    const = lambda shape: pl.BlockSpec(shape, lambda c: (0,) * len(shape))
    return pl.pallas_call(
        _gla_kernel,
        grid=(L // Q,),
        in_specs=[row(GLA_QK), row(GLA_QK), row(GLA_V), row(LANES), row(GLA_V),
                  const((LANES, GLA_QK)), const((1, GLA_QK)), const(coef2.shape),
                  const(masks.shape), const(bdiag.shape), const((1, GLA_DV))],
        out_specs=row(GLA_V),
        out_shape=jax.ShapeDtypeStruct((B, L, GLA_V), BF16),
        scratch_shapes=[pltpu.VMEM((B, GLA_V, GLA_QK), F32)],
        compiler_params=_params(("arbitrary",)),
        name="gla_scan",
    )(gq, gk, gv, glow, gr, w2, gate_b.reshape(1, -1), coef2, masks, bdiag,
      norm_w.reshape(1, -1))


def _block(n, want):
    return want if n % want == 0 else n


def kernel(x, norm_mix, norm_mlp, norm_final, w_in_ab, fox_f_bias, ssd_conv_w, ssd_conv_b,
           ssd_dt_bias, ssd_a_log, ssd_d, ssd_norm, w_out_ab, w_in_cd, gla_gate_w2,
           gla_gate_b, gla_norm, w_out_cd, w_mlp_up, w_mlp_down):
    B, L, D = x.shape
    assert L % CHUNK == 0
    tm = _block(L, 512)
    tq = _block(L, 256)
    tk_fox = _block(L // 2, 512)

    assert tm == tk_fox
    head_order = jnp.argsort(fox_f_bias[0])
    head_cols = (head_order[:, None] * HEAD_DIM + jnp.arange(HEAD_DIM)[None, :]).reshape(-1)
    qd, kd, vd, qa, ka, z, xbc, dt, fend = _inproj_ab(x, norm_mix[0], w_in_ab[0], fox_f_bias[0],
                                                      ssd_dt_bias[0], head_order, head_cols, tm)
    w_out0 = jnp.concatenate([jnp.take(w_out_ab[0, :ATT_W], head_cols, axis=0),
                              w_out_ab[0, ATT_W:]], axis=0)
    y_fox = _fox_attention(qd, qa, kd, ka, vd, fend.reshape(B, L // tm, LANES), tk_fox)
    y_ssd = _ssd(xbc, z, dt, ssd_conv_w[0], ssd_conv_b[0], ssd_a_log[0], ssd_d[0], ssd_norm[0])
    w_up = w_mlp_up.astype(BF16)
    w_down = w_mlp_down.astype(BF16)
    h = _mix_mlp(x, y_fox, y_ssd, w_out0, norm_mlp[0], w_up, w_down, 0, norm_final, tm,
                 final=False)

    sq, sk, sv, gq, gk, gv, gr, glow = _inproj_cd(h, norm_mix[1], w_in_cd[0], tm)
    y_sb = _sb_attention(sq, sk, sv, tq)
    y_gla = _gla(gq, gk, gv, glow, gr, gla_gate_w2[0], gla_gate_b[0], gla_norm[0])
    return _mix_mlp(h, y_sb, y_gla, w_out_cd[0], norm_mlp[1], w_up, w_down, 1, norm_final, tm,
                    final=True)
```

```python
import functools
import math

import numpy as np
import jax
import jax.numpy as jnp
from jax import lax
from jax.experimental import pallas as pl
from jax.experimental.pallas import tpu as pltpu

F32 = jnp.float32
BF16 = jnp.bfloat16

LANES = 128
SUBLANES = 8
VMEM_LIMIT_BYTES = 56 * 1024 * 1024

HEAD_DIM = 64
N_ATT_HEADS = 8
ATT_W = N_ATT_HEADS * HEAD_DIM
SSD_HEADS = 8
SSD_W = 512
SSD_GROUPS = 2
SSD_STATE = 64
SSD_CONV = 4
SSD_CONV_DIM = SSD_W + 2 * SSD_GROUPS * SSD_STATE
GLA_HEADS = 4
GLA_DK = 64
GLA_DV = 128
GLA_RANK = 16
GLA_GATE_NORM = 16.0
EPS = 1e-5
CHUNK = 128
GLA_LEVELS = 7
LOG2E = math.log2(math.e)
DT_LANE0 = 8


def _params(sem):
    return pltpu.CompilerParams(dimension_semantics=sem, vmem_limit_bytes=VMEM_LIMIT_BYTES)


def _split3(x):
    hi = x.astype(BF16)
    r = x - hi.astype(F32)
    mid = r.astype(BF16)
    lo = (r - mid.astype(F32)).astype(BF16)
    return hi, mid, lo


def _split2(x):
    hi = x.astype(BF16)
    lo = (x - hi.astype(F32)).astype(BF16)
    return hi, lo


def _softplus(x):
    return jnp.maximum(x, 0.0) + jnp.log1p(jnp.exp(-jnp.abs(x)))


def _log_sigmoid(x):
    return jnp.minimum(x, 0.0) - jnp.log1p(jnp.exp(-jnp.abs(x)))


def _silu(x):
    h = 0.5 * x
    return h + h * jnp.tanh(h)


def _rms(x, w):
    ms = jnp.mean(x * x, axis=-1, keepdims=True)
    return x * lax.rsqrt(ms + EPS) * w


def _dot(a, b):
    return jnp.dot(a, b, preferred_element_type=F32)


def _dot_nt(a, b):
    return lax.dot_general(a, b, (((1,), (1,)), ((), ())), preferred_element_type=F32)


AB_COLS = 3 * ATT_W + SSD_W + SSD_CONV_DIM + LANES


def _inproj_ab_kernel(x_ref, nw_ref, w_ref, sb_ref, tri3_ref, sel_ref, cst_ref,
                      qd_ref, kd_ref, vd_ref, qa_ref, ka_ref, z_ref, xbc_ref, dt_ref, fend_ref,
                      carry_ref):
    @pl.when(pl.program_id(1) == 0)
    def _():
        carry_ref[...] = jnp.zeros_like(carry_ref)

    u = _rms(x_ref[0], nw_ref[...]).astype(BF16)

    def mm(a, b):
        return _dot(u, w_ref[:, a:b].astype(BF16))

    o = 0
    qd_ref[0] = (mm(o, o + ATT_W) * (LOG2E * HEAD_DIM ** -0.5)).astype(BF16); o += ATT_W
    kd_ref[0] = mm(o, o + ATT_W).astype(BF16); o += ATT_W
    vd_ref[0] = mm(o, o + ATT_W).astype(BF16); o += ATT_W
    z_ref[0] = mm(o, o + SSD_W); o += SSD_W
    xbc_ref[0] = mm(o, o + SSD_CONV_DIM); o += SSD_CONV_DIM
    small = mm(o, o + LANES) + sb_ref[...]

    lane = lax.broadcasted_iota(jnp.int32, small.shape, 1)
    is_f = lane < N_ATT_HEADS
    is_dt = (lane >= DT_LANE0) & (lane < DT_LANE0 + SSD_HEADS)
    log_f = jnp.where(is_f, _log_sigmoid(small), 0.0)
    dt_ref[0] = jnp.where(is_dt, _softplus(small), 0.0)

    hi, mid, lo = _split3(log_f)
    tm = log_f.shape[0]
    carry = carry_ref[...]
    cums = []
    for r0 in range(0, tm, CHUNK):
        rs = slice(r0, r0 + CHUNK)
        piece = _dot(tri3_ref[...], jnp.concatenate([hi[rs], mid[rs], lo[rs]], axis=0)) + carry
        carry = piece[CHUNK - 1:CHUNK, :]
        cums.append(piece)
    cum = jnp.concatenate(cums, axis=0)
    carry_ref[...] = carry

    cum2 = cum * LOG2E
    fend_ref[0, 0] = cum2[tm - 1:tm, :]
    fh, fm, fl = _split3(cum2)
    fcat = (fh.astype(F32) + pltpu.roll(fm.astype(F32), N_ATT_HEADS, 1)
            + pltpu.roll(fl.astype(F32), 2 * N_ATT_HEADS, 1)).astype(BF16)
    aug = _dot(fcat, sel_ref[...]) + cst_ref[...]
    qa_ref[0] = aug[:, :ATT_W].astype(BF16)
    ka_ref[0] = aug[:, ATT_W:].astype(BF16)


def _aug_tables():
    sel = np.zeros((LANES, 2 * ATT_W), np.float32)
    cst = np.zeros((1, 2 * ATT_W), np.float32)
    for h in range(N_ATT_HEADS):
        p, e = divmod(h, 2)
        for part in range(3):
            src = part * N_ATT_HEADS + h
            sel[src, p * LANES + 6 * e + part] = 1.0
            sel[src, ATT_W + p * LANES + 6 * e + 3 + part] = -1.0
            cst[0, p * LANES + 6 * e + 3 + part] = 1.0
            cst[0, ATT_W + p * LANES + 6 * e + part] = 1.0
    return jnp.asarray(sel, BF16), jnp.asarray(cst, F32)


def _tri_incl(n):
    r = np.arange(n)
    return (r[None, :] <= r[:, None]).astype(np.float32)


def _inproj_ab(h, nw, w_in, f_bias, dt_bias, head_order, head_cols, tm):
    B, L, D = h.shape
    fq, fk, fv, fl, wz, wxbc, wdt = jnp.split(w_in, np.cumsum(
        [ATT_W, ATT_W, ATT_W, N_ATT_HEADS, SSD_W, SSD_CONV_DIM])[:].tolist(), axis=1)
    fq, fk, fv = (jnp.take(w, head_cols, axis=1) for w in (fq, fk, fv))
    fl = jnp.take(fl, head_order, axis=1)
    f_bias = f_bias[head_order]
    pad = jnp.zeros((D, LANES - N_ATT_HEADS - SSD_HEADS), w_in.dtype)
    w = jnp.concatenate([fq, fk, fv, wz, wxbc, fl, wdt, pad], axis=1)
    sb = jnp.concatenate([f_bias, dt_bias, jnp.zeros((LANES - 16,), F32)]).reshape(1, LANES)
    assert tm % CHUNK == 0
    tri = _tri_incl(CHUNK)
    tri3 = jnp.asarray(np.concatenate([tri, tri, tri], axis=1), BF16)
    sel, cst = _aug_tables()

    row = lambda width: pl.BlockSpec((1, tm, width), lambda b, i: (b, i, 0))
    const = lambda shape: pl.BlockSpec(shape, lambda b, i: (0,) * len(shape))
    outs = [
        jax.ShapeDtypeStruct((B, L, ATT_W), BF16),
        jax.ShapeDtypeStruct((B, L, ATT_W), BF16),
        jax.ShapeDtypeStruct((B, L, ATT_W), BF16),
        jax.ShapeDtypeStruct((B, L, ATT_W), BF16),
        jax.ShapeDtypeStruct((B, L, ATT_W), BF16),
        jax.ShapeDtypeStruct((B, L, SSD_W), F32),
        jax.ShapeDtypeStruct((B, L, SSD_CONV_DIM), F32),
        jax.ShapeDtypeStruct((B, L, LANES), F32),
        jax.ShapeDtypeStruct((B, L // tm, 1, LANES), F32),
    ]
    return pl.pallas_call(
        _inproj_ab_kernel,
        grid=(B, L // tm),
        in_specs=[row(D), const((1, D)), const((D, AB_COLS)), const((1, LANES)),
                  const((CHUNK, 3 * CHUNK)), const((LANES, 2 * ATT_W)), const((1, 2 * ATT_W))],
        out_specs=[row(ATT_W)] * 5 + [row(SSD_W), row(SSD_CONV_DIM), row(LANES),
                   pl.BlockSpec((1, 1, 1, LANES), lambda b, i: (b, i, 0, 0))],
        out_shape=outs,
        scratch_shapes=[pltpu.VMEM((1, LANES), F32)],
        compiler_params=_params(("arbitrary", "arbitrary")),
        name="inproj_ab",
    )(h, nw.reshape(1, D), w, sb, tri3, sel, cst)


def _head_lane_mask(e, shape):
    lane = lax.broadcasted_iota(jnp.int32, shape, 1)
    return (lane < HEAD_DIM) if e == 0 else (lane >= HEAD_DIM)


EXP2_DEAD = -151.0
FOX_STRIP = 64


def _fox_kernel(qd_ref, qa_ref, kd_ref, ka_ref, v_ref, fend_ref, o_ref,
                q_ref, s0_ref, s1_ref, p0_ref, p1_ref, m_ref, al_ref, acc_ref, kn_ref, *, tk):
    it = pl.program_id(2)
    ka = 2 * it
    R = 4 * tk
    lower = slice(2 * tk, R)
    everything = slice(0, R)
    qd = qd_ref[0].astype(F32)
    qa = qa_ref[0].astype(F32)
    lane = lax.broadcasted_iota(jnp.int32, (tk, LANES), 1)
    first = lane < HEAD_DIM
    for half in range(2):
        hs = slice(half * tk, (half + 1) * tk)
        for e in range(2):
            amask = (lane >= 6 * e) & (lane < 6 * e + 6)
            g = 2 * half + e
            q_ref[g * tk:(g + 1) * tk, :] = jnp.concatenate(
                [jnp.where(first if e == 0 else ~first, qd[hs], 0.0),
                 jnp.where(amask, qa[hs], 0.0)], axis=1).astype(BF16)
    m_ref[...] = jnp.full(m_ref.shape, -jnp.inf, F32)
    acc_ref[...] = jnp.zeros(acc_ref.shape, F32)
    ones = jnp.ones((tk, LANES), BF16)

    def head_sq_norms(x):
        lane_x = lax.broadcasted_iota(jnp.int32, x.shape, 1)
        sq = x * x
        return (jnp.sum(jnp.where(lane_x < HEAD_DIM, sq, 0.0), axis=1, keepdims=True),
                jnp.sum(jnp.where(lane_x < HEAD_DIM, 0.0, sq), axis=1, keepdims=True))

    @pl.when(it == 0)
    def _():
        def kbody(c, carry):
            ks = pl.multiple_of(c * tk, tk)
            n0, n1 = head_sq_norms(kd_ref[0, pl.ds(ks, tk), :].astype(F32))
            return jnp.maximum(carry[0], n0), jnp.maximum(carry[1], n1)
        zero = jnp.zeros((tk, 1), F32)
        n0, n1 = lax.fori_loop(0, kd_ref.shape[1] // tk, kbody, (zero, zero))
        for e, n in enumerate((n0, n1)):
            kn_ref[e] = jnp.broadcast_to(jnp.sqrt(jnp.max(n, axis=0, keepdims=True)), kn_ref.shape[1:])

    def scores(j, s_ref, rows=everything):
        ks = pl.multiple_of(j * tk, tk)
        k = jnp.concatenate([kd_ref[0, pl.ds(ks, tk), :], ka_ref[0, pl.ds(ks, tk), :]], axis=1)
        s_ref[rows, :] = _dot_nt(q_ref[rows, :], k)

    def update(j, s_ref, p_ref, rows=everything, diag_rows=0):
        for r0 in range(rows.start, rows.stop, FOX_STRIP):
            rs = slice(r0, r0 + FOX_STRIP)
            s = s_ref[rs, :]
            if r0 < diag_rows:
                rws = lax.broadcasted_iota(jnp.int32, s.shape, 0) + (r0 % tk)
                cols = lax.broadcasted_iota(jnp.int32, s.shape, 1)
                s = jnp.where(rws >= cols, s, -jnp.inf)
            m_old = m_ref[rs, :]
            m_new = jnp.maximum(m_old, jnp.max(s, axis=1, keepdims=True))
            al_ref[rs, :] = jnp.exp2(m_old - m_new)
            m_ref[rs, :] = m_new
            p_ref[rs, :] = jnp.exp2(s - m_new).astype(BF16)
        ks = pl.multiple_of(j * tk, tk)
        v1 = jnp.concatenate([v_ref[0, pl.ds(ks, tk), :], ones], axis=1)
        acc_ref[rows, :] = al_ref[rows, :] * acc_ref[rows, :] + _dot(p_ref[rows, :], v1)

    scores(ka + 1, s0_ref, lower)
    scores(ka, s1_ref)
    update(ka + 1, s0_ref, p0_ref, rows=lower, diag_rows=R)
    scores(jnp.maximum(ka - 1, 0), s0_ref)
    update(ka, s1_ref, p1_ref, diag_rows=2 * tk)

    qn = head_sq_norms(qd)
    fend = fend_ref[0]
    hl = lax.broadcasted_iota(jnp.int32, fend.shape, 1)
    jcol = lax.broadcasted_iota(jnp.int32, (fend.shape[0], 1), 0)
    jmin = ka
    for e in range(2):
        head = 2 * pl.program_id(1) + e
        fe = jnp.sum(jnp.where(hl == head, fend, 0.0), axis=1, keepdims=True)
        f_top = jnp.sum(jnp.where(jcol == ka - 1, fe, 0.0), axis=0, keepdims=True)
        qk = jnp.sqrt(jnp.max(qn[e], axis=0, keepdims=True)) * kn_ref[e, 0:1, 0:1]
        m_min = jnp.minimum(jnp.min(m_ref[e * tk:(e + 1) * tk, :], axis=0, keepdims=True),
                            jnp.min(m_ref[(2 + e) * tk:(3 + e) * tk, :], axis=0, keepdims=True))
        live = (jcol < ka) & (qk + f_top - fe + 1.0 - m_min > EXP2_DEAD)
        jmin = jnp.minimum(jmin, jnp.min(jnp.where(live, jcol, ka)))
    nb = ka - jmin

    def pair(u, carry):
        j = ka - 1 - 2 * u
        scores(j - 1, s1_ref)
        update(j, s0_ref, p0_ref)
        scores(jnp.maximum(j - 2, 0), s0_ref)
        update(j - 1, s1_ref, p1_ref)
        return carry

    lax.fori_loop(0, nb // 2, pair, 0)

    @pl.when(nb % 2 == 1)
    def _():
        update(jmin, s0_ref, p0_ref)

    acc = acc_ref[...]
    out = acc[:, :LANES] / acc[:, LANES:]
    for half in range(2):
        g = 2 * half
        o_ref[0, half * tk:(half + 1) * tk, :] = jnp.where(
            first, out[g * tk:(g + 1) * tk], out[(g + 1) * tk:(g + 2) * tk]).astype(BF16)


def _fox_attention(qd, qa, kd, ka, v, fend, tk):
    B, L, _ = qd.shape
    assert fend.shape == (B, L // tk, LANES)
    n_pairs = N_ATT_HEADS // 2
    tq = 2 * tk
    R = 2 * tq
    qspec = pl.BlockSpec((1, tq, LANES), lambda b, p, i: (b, i, p))
    kspec = pl.BlockSpec((1, L, LANES), lambda b, p, i: (b, 0, p))
    return pl.pallas_call(
        functools.partial(_fox_kernel, tk=tk),
        grid=(B, n_pairs, L // tq),
        in_specs=[qspec, qspec, kspec, kspec, kspec,
                  pl.BlockSpec((1, L // tk, LANES), lambda b, p, i: (b, 0, 0))],
        out_specs=qspec,
        out_shape=jax.ShapeDtypeStruct((B, L, ATT_W), BF16),
        scratch_shapes=[pltpu.VMEM((R, 2 * LANES), BF16),
                        pltpu.VMEM((R, tk), F32), pltpu.VMEM((R, tk), F32),
                        pltpu.VMEM((R, tk), BF16), pltpu.VMEM((R, tk), BF16),
                        pltpu.VMEM((R, 1), F32), pltpu.VMEM((R, 1), F32),
                        pltpu.VMEM((R, 2 * LANES), F32),
                        pltpu.VMEM((2, SUBLANES, LANES), F32)],
        compiler_params=_params(("arbitrary", "arbitrary", "arbitrary")),
        name="fox_attention",
    )(qd, qa, kd, ka, v, fend)


def _ssd_kernel(xbc_ref, z_ref, dt_ref, cw_ref, cb_ref, alog_ref, dsk_ref, nw_ref, tri3_ref,
                y_ref, xext_ref, st_ref):
    @pl.when(pl.program_id(0) == 0)
    def _():
        xext_ref[:, 0:SUBLANES, :] = jnp.zeros((xext_ref.shape[0], SUBLANES, SSD_CONV_DIM), F32)
        st_ref[...] = jnp.zeros_like(st_ref)

    for b in range(xbc_ref.shape[0]):
        _ssd_chunk(xbc_ref.at[b], z_ref.at[b], dt_ref.at[b], cw_ref, cb_ref, alog_ref, dsk_ref,
                   nw_ref, tri3_ref, y_ref.at[b], xext_ref.at[b], st_ref.at[b])


def _ssd_chunk(xbc_ref, z_ref, dt_ref, cw_ref, cb_ref, alog_ref, dsk_ref, nw_ref, tri3_ref,
               y_ref, xext_ref, st_ref):
    Q = CHUNK
    xext_ref[SUBLANES:SUBLANES + Q, :] = xbc_ref[...]
    conv = cb_ref[...]
    for k in range(SSD_CONV):
        conv = conv + cw_ref[k:k + 1, :] * xext_ref[pl.ds(SUBLANES - (SSD_CONV - 1) + k, Q), :]
    xext_ref[0:SUBLANES, :] = xext_ref[Q:Q + SUBLANES, :]
    xc = _silu(conv)
    xs = xc[:, :SSD_W]
    bm = xc[:, SSD_W:SSD_W + LANES]
    cm = xc[:, SSD_W + LANES:]

    dt = dt_ref[...]
    a = dt * (-jnp.exp(alog_ref[...]))
    hi, mid, lo = _split3(a)
    a_cum = _dot(tri3_ref[...], jnp.concatenate([hi, mid, lo], axis=0))
    a_row = a_cum.T
    a_last_col = a_cum[Q - 1:Q, :]
    bm_t = bm.T

    rows = lax.broadcasted_iota(jnp.int32, (Q, Q), 0)
    cols = lax.broadcasted_iota(jnp.int32, (Q, Q), 1)
    causal = rows >= cols
    lane = lax.broadcasted_iota(jnp.int32, (Q, LANES), 1)
    first_half = lane < HEAD_DIM

    scores = []
    cmask = []
    for g in range(SSD_GROUPS):
        cg = jnp.where(_head_lane_mask(g, (Q, LANES)), cm, 0.0).astype(BF16)
        cmask.append(cg)
        scores.append(_dot_nt(cg, bm.astype(BF16)))

    y_pairs = []
    heads_per_group = SSD_HEADS // SSD_GROUPS
    for p in range(SSD_HEADS // 2):
        xs_pair = xs[:, p * LANES:(p + 1) * LANES]
        dt_pair = jnp.where(first_half, dt[:, DT_LANE0 + 2 * p:DT_LANE0 + 2 * p + 1],
                            dt[:, DT_LANE0 + 2 * p + 1:DT_LANE0 + 2 * p + 2])
        xdt = (xs_pair * dt_pair).astype(BF16)
        y_head = []
        for e in range(2):
            h = 2 * p + e
            g = h // heads_per_group
            hl = DT_LANE0 + h
            acol = a_cum[:, hl:hl + 1]
            arow = a_row[hl:hl + 1, :]
            alast = a_last_col[:, hl:hl + 1]
            lmat = jnp.exp(jnp.where(causal, acol - arow, -jnp.inf))
            y_diag = _dot((scores[g] * lmat).astype(BF16), xdt)
            prev = st_ref[h]
            y_off = _dot(cmask[g], prev.astype(BF16)) * jnp.exp(acol)
            y_head.append(y_diag + y_off)
            decay_row = jnp.exp(alast - arow)
            local = _dot((bm_t * decay_row).astype(BF16), xdt)
            st_ref[h] = prev * jnp.exp(alast) + local
        y_pairs.append(jnp.where(first_half, y_head[0], y_head[1]))
    y = jnp.concatenate(y_pairs, axis=1) + dsk_ref[...] * xs
    y = y * _silu(z_ref[...])
    gw = SSD_W // SSD_GROUPS
    y = jnp.concatenate([_rms(y[:, g * gw:(g + 1) * gw], nw_ref[:, g * gw:(g + 1) * gw])
                         for g in range(SSD_GROUPS)], axis=1)
    y_ref[...] = y.astype(BF16)


def _ssd(xbc, z, dt, conv_w, conv_b, a_log, d_skip, norm_w):
    B, L, _ = xbc.shape
    Q = CHUNK
    alog = jnp.zeros((1, LANES), F32).at[0, DT_LANE0:DT_LANE0 + SSD_HEADS].set(a_log)
    dsk = jnp.repeat(d_skip, SSD_W // SSD_HEADS).reshape(1, SSD_W)
    tri = _tri_incl(Q)
    tri3 = jnp.asarray(np.concatenate([tri, tri, tri], axis=1), BF16)
    row = lambda width: pl.BlockSpec((B, Q, width), lambda c: (0, c, 0))
    const = lambda shape: pl.BlockSpec(shape, lambda c: (0,) * len(shape))
    return pl.pallas_call(
        _ssd_kernel,
        grid=(L // Q,),
        in_specs=[row(SSD_CONV_DIM), row(SSD_W), row(LANES), const((SSD_CONV, SSD_CONV_DIM)),
                  const((1, SSD_CONV_DIM)), const((1, LANES)), const((1, SSD_W)),
                  const((1, SSD_W)), const((Q, 3 * Q))],
        out_specs=row(SSD_W),
        out_shape=jax.ShapeDtypeStruct((B, L, SSD_W), BF16),
        scratch_shapes=[pltpu.VMEM((B, Q + SUBLANES, SSD_CONV_DIM), F32),
                        pltpu.VMEM((B, SSD_HEADS, LANES, LANES), F32)],
        compiler_params=_params(("arbitrary",)),
        name="ssd_scan",
    )(xbc, z, dt, conv_w, conv_b.reshape(1, -1), alog, dsk, norm_w.reshape(1, -1), tri3)


def _mix_mlp_kernel(h_ref, y1_ref, y2_ref, wo_ref, nw_ref, wup_ref, wdn_ref, nf_ref, o_ref,
                    *, ff_chunk, final):
    half = y1_ref.shape[-1]
    mix = (_dot(y1_ref[...], wo_ref[0:half, :].astype(BF16))
           + _dot(y2_ref[...], wo_ref[half:2 * half, :].astype(BF16)))
    h1 = h_ref[...] + mix
    u = _rms(h1, nw_ref[...]).astype(BF16)
    d_ff = wup_ref.shape[1]
    acc = jnp.zeros_like(h1)
    for c in range(d_ff // ff_chunk):
        act = jnp.maximum(_dot(u, wup_ref[:, c * ff_chunk:(c + 1) * ff_chunk].astype(BF16)), 0.0)
        acc = acc + _dot((act * act).astype(BF16),
                         wdn_ref[c * ff_chunk:(c + 1) * ff_chunk, :].astype(BF16))
    h2 = h1 + acc
    if final:
        h2 = _rms(h2, nf_ref[...])
    o_ref[...] = h2


def _mix_mlp(h, y1, y2, w_out, nw, w_up, w_down, layer, nf, tm, final):
    B, L, D = h.shape
    T = B * L
    d_ff = w_up.shape[2]
    half = y1.shape[-1]
    row = lambda width: pl.BlockSpec((tm, width), lambda i: (i, 0))
    const = lambda shape: pl.BlockSpec(shape, lambda i: (0,) * len(shape))
    stacked = lambda shape: pl.BlockSpec((None,) + shape, lambda i: (layer, 0, 0))
    out = pl.pallas_call(
        functools.partial(_mix_mlp_kernel, ff_chunk=min(1024, d_ff), final=final),
        grid=(T // tm,),
        in_specs=[row(D), row(half), row(half), const((2 * half, D)), const((1, D)),
                  stacked((D, d_ff)), stacked((d_ff, D)), const((1, D))],
        out_specs=row(D),
        out_shape=jax.ShapeDtypeStruct((T, D), F32),
        compiler_params=_params(("arbitrary",)),
        name="mix_mlp",
    )(h.reshape(T, D), y1.reshape(T, half), y2.reshape(T, half), w_out,
      nw.reshape(1, D), w_up, w_down, nf.reshape(1, D))
    return out.reshape(B, L, D)


GLA_QK = GLA_HEADS * GLA_DK
GLA_V = GLA_HEADS * GLA_DV
CD_COLS = 3 * ATT_W + 2 * GLA_QK + 2 * GLA_V + LANES


def _inproj_cd_kernel(x_ref, nw_ref, w_ref, sq_ref, sk_ref, sv_ref, gq_ref, gk_ref, gv_ref,
                      gr_ref, glow_ref):
    u = _rms(x_ref[...], nw_ref[...]).astype(BF16)

    def mm(a, b):
        return _dot(u, w_ref[:, a:b].astype(BF16))

    o = 0
    sq_ref[...] = (mm(o, o + ATT_W) * (LOG2E * HEAD_DIM ** -0.5)).astype(BF16); o += ATT_W
    sk_ref[...] = mm(o, o + ATT_W).astype(BF16); o += ATT_W
    sv_ref[...] = mm(o, o + ATT_W).astype(BF16); o += ATT_W
    gq_ref[...] = mm(o, o + GLA_QK) * (GLA_DK ** -0.5); o += GLA_QK
    gk_ref[...] = mm(o, o + GLA_QK); o += GLA_QK
    gv_ref[...] = mm(o, o + GLA_V); o += GLA_V
    gr_ref[...] = mm(o, o + GLA_V); o += GLA_V
    glow_ref[...] = mm(o, o + LANES)


def _inproj_cd(h, nw, w_in, tm):
    B, L, D = h.shape
    T = B * L
    sq, sk, sv, gq, gk, gv, glow, gr = jnp.split(w_in, np.cumsum(
        [ATT_W, ATT_W, ATT_W, GLA_QK, GLA_QK, GLA_V, GLA_RANK]).tolist(), axis=1)
    pad = jnp.zeros((D, LANES - GLA_RANK), w_in.dtype)
    w = jnp.concatenate([sq, sk, sv, gq, gk, gv, gr, glow, pad], axis=1)
    row = lambda width: pl.BlockSpec((tm, width), lambda i: (i, 0))
    const = lambda shape: pl.BlockSpec(shape, lambda i: (0,) * len(shape))
    widths = [ATT_W, ATT_W, ATT_W, GLA_QK, GLA_QK, GLA_V, GLA_V, LANES]
    dtypes = [BF16, BF16, BF16, F32, F32, F32, F32, F32]
    outs = pl.pallas_call(
        _inproj_cd_kernel,
        grid=(T // tm,),
        in_specs=[row(D), const((1, D)), const((D, CD_COLS))],
        out_specs=[row(wd) for wd in widths],
        out_shape=[jax.ShapeDtypeStruct((T, wd), dt) for wd, dt in zip(widths, dtypes)],
        compiler_params=_params(("arbitrary",)),
        name="inproj_cd",
    )(h.reshape(T, D), nw.reshape(1, D), w)
    return [o.reshape(B, L, -1) for o in outs]


SB_DEAD = EXP2_DEAD - 1.0
SB_STRIP = 64
SB_CHAINS = 2


def _sb_kernel(q_ref, k_ref, v_ref, u2_ref, o_ref,
               qs_ref, z_ref, sp_ref, in_ref, a_ref, c_ref, acc_ref, *, tq):
    i = pl.program_id(2)
    tk = tq
    R = 2 * tq
    chains = range(SB_CHAINS)
    for ch in chains:
        qf = q_ref[0, :, ch * LANES:(ch + 1) * LANES].astype(F32)
        for e in range(2):
            qs_ref[ch, e * tq:(e + 1) * tq, :] = jnp.where(_head_lane_mask(e, (tq, LANES)), qf,
                                                           0.0).astype(BF16)
    c_ref[...] = jnp.zeros(c_ref.shape, F32)
    acc_ref[...] = jnp.zeros(acc_ref.shape, F32)

    def strict_mask(r0, shape):
        rows = lax.broadcasted_iota(jnp.int32, shape, 0) + (r0 % tq)
        return lax.broadcasted_iota(jnp.int32, shape, 1) < rows

    def step(j, masked):
        ks = pl.multiple_of(j * tk, tk)
        for ch in chains:
            z_ref[ch] = _dot_nt(qs_ref[ch], k_ref[0, pl.ds(ks, tk), ch * LANES:(ch + 1) * LANES])
        for ch in chains:
            for r0 in range(0, R, SB_STRIP):
                rs = slice(r0, r0 + SB_STRIP)
                z = z_ref[ch, rs, :]
                sp = jnp.maximum(z, 0.0) + jnp.log2(1.0 + jnp.exp2(-jnp.abs(z)))
                if masked:
                    sp = jnp.where(strict_mask(r0, sp.shape), sp, 0.0)
                sp_ref[ch, rs, :] = sp.astype(BF16)
        for ch in chains:
            in_ref[ch] = _dot(sp_ref[ch], u2_ref[...])
        for ch in chains:
            for r0 in range(0, R, SB_STRIP):
                rs = slice(r0, r0 + SB_STRIP)
                inc = in_ref[ch, rs, :]
                a = jnp.exp2(z_ref[ch, rs, :] - c_ref[ch, rs, :] - inc)
                if masked:
                    a = jnp.where(strict_mask(r0, a.shape), a, 0.0)
                a_ref[ch, rs, :] = a.astype(BF16)
                c_ref[ch, rs, :] += inc[:, 0:1]
        for ch in chains:
            acc_ref[ch] += _dot(a_ref[ch], v_ref[0, pl.ds(ks, tk), ch * LANES:(ch + 1) * LANES])
        return jnp.min(c_ref[...])

    cmin = step(i, True)

    def cond(carry):
        j, cmin = carry
        return (j >= 0) & (cmin < -SB_DEAD)

    def body(carry):
        j, _ = carry
        return j - 1, step(j, False)

    lax.while_loop(cond, body, (i - 1, cmin))
    for ch in chains:
        acc = acc_ref[ch]
        o_ref[0, :, ch * LANES:(ch + 1) * LANES] = jnp.where(
            _head_lane_mask(0, (tq, LANES)), acc[:tq], acc[tq:]).astype(BF16)


def _sb_attention(q, k, v, tq):
    B, L, _ = q.shape
    n_groups = N_ATT_HEADS // (2 * SB_CHAINS)
    R = 2 * tq
    W = SB_CHAINS * LANES
    r = np.arange(tq)
    u = (r[:, None] >= r[None, :]).astype(np.float32)
    u2 = jnp.asarray(u, BF16)
    qspec = pl.BlockSpec((1, tq, W), lambda b, p, i: (b, i, p))
    kspec = pl.BlockSpec((1, L, W), lambda b, p, i: (b, 0, p))
    return pl.pallas_call(
        functools.partial(_sb_kernel, tq=tq),
        grid=(B, n_groups, L // tq),
        in_specs=[qspec, kspec, kspec, pl.BlockSpec((tq, tq), lambda b, p, i: (0, 0))],
        out_specs=qspec,
        out_shape=jax.ShapeDtypeStruct((B, L, ATT_W), BF16),
        scratch_shapes=[pltpu.VMEM((SB_CHAINS, R, LANES), BF16), pltpu.VMEM((SB_CHAINS, R, tq), F32),
                        pltpu.VMEM((SB_CHAINS, R, tq), BF16),
                        pltpu.VMEM((SB_CHAINS, R, tq), F32),
                        pltpu.VMEM((SB_CHAINS, R, tq), BF16), pltpu.VMEM((SB_CHAINS, R, 1), F32),
                        pltpu.VMEM((SB_CHAINS, R, LANES), F32)],
        compiler_params=_params(("arbitrary", "arbitrary", "arbitrary")),
        name="sb_attention",
    )(q, k, v, u2)


def _gla_tables():
    Q = CHUNK
    r = np.arange(Q)
    j = np.arange(Q)
    coef = [(j[None, :] <= r[:, None]), (j[None, :] > r[:, None])]
    masks = [np.eye(Q, dtype=bool)]
    for lvl in range(GLA_LEVELS):
        m = 1 << lvl
        c0 = (r // (2 * m)) * (2 * m)
        mid = c0 + m - 1
        upper = (r - c0) >= m
        up = (j[None, :] > mid[:, None]) & (j[None, :] <= r[:, None])
        lowr = (j[None, :] > r[:, None]) & (j[None, :] <= mid[:, None])
        coef.append(np.where(upper[:, None], up, lowr))
        masks.append((c0[:, None] == c0[None, :]) & upper[:, None] & (~upper)[None, :])
    coef = np.concatenate(coef, axis=0).astype(np.float32)
    coef2 = np.concatenate([coef, coef], axis=1)
    masks = np.stack(masks).astype(np.float32)
    hv = np.arange(GLA_V) // GLA_DV
    hk = np.arange(GLA_QK) // GLA_DK
    bdiag = (hv[:, None] == hk[None, :]).astype(np.float32)
    return jnp.asarray(coef2, BF16), jnp.asarray(masks, F32), jnp.asarray(bdiag, F32)


def _gla_kernel(gq_ref, gk_ref, gv_ref, glow_ref, gr_ref, w2_ref, gb_ref, coef_ref, mask_ref,
                bdiag_ref, nw_ref, o_ref, st_ref):
    @pl.when(pl.program_id(0) == 0)
    def _():
        st_ref[...] = jnp.zeros_like(st_ref)

    for b in range(gq_ref.shape[0]):
        _gla_chunk(gq_ref.at[b], gk_ref.at[b], gv_ref.at[b], glow_ref.at[b], gr_ref.at[b], w2_ref,
                   gb_ref, coef_ref, mask_ref, bdiag_ref, nw_ref, o_ref.at[b], st_ref.at[b])


def _gla_chunk(gq_ref, gk_ref, gv_ref, glow_ref, gr_ref, w2_ref, gb_ref, coef_ref, mask_ref,
               bdiag_ref, nw_ref, o_ref, st_ref):
    Q = CHUNK
    logits = _dot(glow_ref[...].astype(BF16), w2_ref[...]) + gb_ref[...]
    la = _log_sigmoid(logits) * (1.0 / GLA_GATE_NORM)
    hi, lo = _split2(la)
    expo = _dot(coef_ref[...], jnp.concatenate([hi, lo], axis=0))
    q = gq_ref[...]
    k = gk_ref[...]
    v = gv_ref[...]
    lane_head = lax.broadcasted_iota(jnp.int32, (Q, GLA_QK), 1) // GLA_DK
    row = lax.broadcasted_iota(jnp.int32, (Q, GLA_QK), 0)
    hmask = [lane_head == h for h in range(GLA_HEADS)]

    att = [None] * GLA_HEADS
    for lvl in range(-1, GLA_LEVELS):
        if lvl < 0:
            xq, xk = q, k.astype(BF16)
        else:
            m = 1 << lvl
            upper = (row & (2 * m - 1)) >= m
            xq = jnp.where(upper, q, k) * jnp.exp(expo[(2 + lvl) * Q:(3 + lvl) * Q, :])
            xk = xq.astype(BF16)
        msk = mask_ref[lvl + 1]
        lhs = jnp.concatenate([jnp.where(hmask[h], xq, 0.0) for h in range(GLA_HEADS)], axis=0)
        prod = _dot_nt(lhs.astype(BF16), xk)
        for h in range(GLA_HEADS):
            part = prod[h * Q:(h + 1) * Q] * msk
            att[h] = part if att[h] is None else att[h] + part

    st = st_ref[...]
    q_in = (q * jnp.exp(expo[0:Q, :])).astype(BF16)
    o = _dot_nt(q_in, st.astype(BF16))
    o_intra = [_dot(att[h].astype(BF16), v[:, h * GLA_DV:(h + 1) * GLA_DV].astype(BF16))
               for h in range(GLA_HEADS)]
    o = o + jnp.concatenate(o_intra, axis=1)

    k_dec = (k * jnp.exp(expo[Q:2 * Q, :])).astype(BF16)
    upd = _dot(v.T.astype(BF16), k_dec)
    g_last = expo[Q - 1:Q, :]
    st_ref[...] = st * jnp.exp(g_last) + upd * bdiag_ref[...]

    gr = gr_ref[...]
    o = jnp.concatenate([_rms(o[:, h * GLA_DV:(h + 1) * GLA_DV], nw_ref[...])
                         for h in range(GLA_HEADS)], axis=1)
    o_ref[...] = (o * _silu(gr)).astype(BF16)


def _gla(gq, gk, gv, glow, gr, gate_w2, gate_b, norm_w):
    B, L, _ = gq.shape
    Q = CHUNK
    coef2, masks, bdiag = _gla_tables()
    w2 = jnp.zeros((LANES, GLA_QK), F32).at[:GLA_RANK].set(gate_w2).astype(BF16)
    row = lambda width: pl.BlockSpec((B, Q, width), lambda c: (0, c, 0))
    const = lambda shape: pl.BlockSpec(shape, lambda c: (0,) * len(shape))
    return pl.pallas_call(
        _gla_kernel,
        grid=(L // Q,),
        in_specs=[row(GLA_QK), row(GLA_QK), row(GLA_V), row(LANES), row(GLA_V),
                  const((LANES, GLA_QK)), const((1, GLA_QK)), const(coef2.shape),
                  const(masks.shape), const(bdiag.shape), const((1, GLA_DV))],
        out_specs=row(GLA_V),
        out_shape=jax.ShapeDtypeStruct((B, L, GLA_V), BF16),
        scratch_shapes=[pltpu.VMEM((B, GLA_V, GLA_QK), F32)],
        compiler_params=_params(("arbitrary",)),
        name="gla_scan",
    )(gq, gk, gv, glow, gr, w2, gate_b.reshape(1, -1), coef2, masks, bdiag,
      norm_w.reshape(1, -1))


def _block(n, want):
    return want if n % want == 0 else n


def kernel(x, norm_mix, norm_mlp, norm_final, w_in_ab, fox_f_bias, ssd_conv_w, ssd_conv_b,
           ssd_dt_bias, ssd_a_log, ssd_d, ssd_norm, w_out_ab, w_in_cd, gla_gate_w2,
           gla_gate_b, gla_norm, w_out_cd, w_mlp_up, w_mlp_down):
    B, L, D = x.shape
    assert L % CHUNK == 0
    tm = _block(L, 512)
    tq = _block(L, 256)
    tk_fox = _block(L // 2, 512)

    assert tm == tk_fox
    head_order = jnp.argsort(fox_f_bias[0])
    head_cols = (head_order[:, None] * HEAD_DIM + jnp.arange(HEAD_DIM)[None, :]).reshape(-1)
    qd, kd, vd, qa, ka, z, xbc, dt, fend = _inproj_ab(x, norm_mix[0], w_in_ab[0], fox_f_bias[0],
                                                      ssd_dt_bias[0], head_order, head_cols, tm)
    w_out0 = jnp.concatenate([jnp.take(w_out_ab[0, :ATT_W], head_cols, axis=0),
                              w_out_ab[0, ATT_W:]], axis=0)
    y_fox = _fox_attention(qd, qa, kd, ka, vd, fend.reshape(B, L // tm, LANES), tk_fox)
    y_ssd = _ssd(xbc, z, dt, ssd_conv_w[0], ssd_conv_b[0], ssd_a_log[0], ssd_d[0], ssd_norm[0])
    w_up = w_mlp_up
    w_down = w_mlp_down
    h = _mix_mlp(x, y_fox, y_ssd, w_out0, norm_mlp[0], w_up, w_down, 0, norm_final, tm,
                 final=False)

    sq, sk, sv, gq, gk, gv, gr, glow = _inproj_cd(h, norm_mix[1], w_in_cd[0], tm)
    y_sb = _sb_attention(sq, sk, sv, tq)
    y_gla = _gla(gq, gk, gv, glow, gr, gla_gate_w2[0], gla_gate_b[0], gla_norm[0])
    return _mix_mlp(h, y_sb, y_gla, w_out_cd[0], norm_mlp[1], w_up, w_down, 1, norm_final, tm,
                    final=True)
```

```python
import functools
import math

import numpy as np
import jax
import jax.numpy as jnp
from jax import lax
from jax.experimental import pallas as pl
from jax.experimental.pallas import tpu as pltpu

F32 = jnp.float32
BF16 = jnp.bfloat16

LANES = 128
SUBLANES = 8
VMEM_LIMIT_BYTES = 56 * 1024 * 1024

HEAD_DIM = 64
N_ATT_HEADS = 8
ATT_W = N_ATT_HEADS * HEAD_DIM
SSD_HEADS = 8
SSD_W = 512
SSD_GROUPS = 2
SSD_STATE = 64
SSD_CONV = 4
SSD_CONV_DIM = SSD_W + 2 * SSD_GROUPS * SSD_STATE
GLA_HEADS = 4
GLA_DK = 64
GLA_DV = 128
GLA_RANK = 16
GLA_GATE_NORM = 16.0
EPS = 1e-5
CHUNK = 128
GLA_LEVELS = 7
LOG2E = math.log2(math.e)
DT_LANE0 = 8


def _params(sem):
    return pltpu.CompilerParams(dimension_semantics=sem, vmem_limit_bytes=VMEM_LIMIT_BYTES)


def _split3(x):
    hi = x.astype(BF16)
    r = x - hi.astype(F32)
    mid = r.astype(BF16)
    lo = (r - mid.astype(F32)).astype(BF16)
    return hi, mid, lo


def _split2(x):
    hi = x.astype(BF16)
    lo = (x - hi.astype(F32)).astype(BF16)
    return hi, lo


def _softplus(x):
    return jnp.maximum(x, 0.0) + jnp.log1p(jnp.exp(-jnp.abs(x)))


def _log_sigmoid(x):
    return jnp.minimum(x, 0.0) - jnp.log1p(jnp.exp(-jnp.abs(x)))


def _silu(x):
    h = 0.5 * x
    return h + h * jnp.tanh(h)


def _rms(x, w):
    ms = jnp.mean(x * x, axis=-1, keepdims=True)
    return x * lax.rsqrt(ms + EPS) * w


def _dot(a, b):
    return jnp.dot(a, b, preferred_element_type=F32)


def _dot_nt(a, b):
    return lax.dot_general(a, b, (((1,), (1,)), ((), ())), preferred_element_type=F32)


AB_COLS = 3 * ATT_W + SSD_W + SSD_CONV_DIM + LANES


def _inproj_ab_kernel(x_ref, nw_ref, w_ref, sb_ref, tri3_ref, sel_ref, cst_ref,
                      qd_ref, kd_ref, vd_ref, qa_ref, ka_ref, z_ref, xbc_ref, dt_ref, fend_ref,
                      carry_ref):
    @pl.when(pl.program_id(1) == 0)
    def _():
        carry_ref[...] = jnp.zeros_like(carry_ref)

    u = _rms(x_ref[0], nw_ref[...]).astype(BF16)

    def mm(a, b):
        return _dot(u, w_ref[:, a:b])

    o = 0
    qd_ref[0] = (mm(o, o + ATT_W) * (LOG2E * HEAD_DIM ** -0.5)).astype(BF16); o += ATT_W
    kd_ref[0] = mm(o, o + ATT_W).astype(BF16); o += ATT_W
    vd_ref[0] = mm(o, o + ATT_W).astype(BF16); o += ATT_W
    z_ref[0] = mm(o, o + SSD_W); o += SSD_W
    xbc_ref[0] = mm(o, o + SSD_CONV_DIM); o += SSD_CONV_DIM
    small = mm(o, o + LANES) + sb_ref[...]

    lane = lax.broadcasted_iota(jnp.int32, small.shape, 1)
    is_f = lane < N_ATT_HEADS
    is_dt = (lane >= DT_LANE0) & (lane < DT_LANE0 + SSD_HEADS)
    log_f = jnp.where(is_f, _log_sigmoid(small), 0.0)
    dt_ref[0] = jnp.where(is_dt, _softplus(small), 0.0)

    hi, mid, lo = _split3(log_f)
    tm = log_f.shape[0]
    carry = carry_ref[...]
    cums = []
    for r0 in range(0, tm, CHUNK):
        rs = slice(r0, r0 + CHUNK)
        piece = _dot(tri3_ref[...], jnp.concatenate([hi[rs], mid[rs], lo[rs]], axis=0)) + carry
        carry = piece[CHUNK - 1:CHUNK, :]
        cums.append(piece)
    cum = jnp.concatenate(cums, axis=0)
    carry_ref[...] = carry

    cum2 = cum * LOG2E
    fend_ref[0, 0] = cum2[tm - 1:tm, :]
    fh, fm, fl = _split3(cum2)
    fcat = (fh.astype(F32) + pltpu.roll(fm.astype(F32), N_ATT_HEADS, 1)
            + pltpu.roll(fl.astype(F32), 2 * N_ATT_HEADS, 1)).astype(BF16)
    aug = _dot(fcat, sel_ref[...]) + cst_ref[...]
    qa_ref[0] = aug[:, :ATT_W].astype(BF16)
    ka_ref[0] = aug[:, ATT_W:].astype(BF16)


def _aug_tables():
    sel = np.zeros((LANES, 2 * ATT_W), np.float32)
    cst = np.zeros((1, 2 * ATT_W), np.float32)
    for h in range(N_ATT_HEADS):
        p, e = divmod(h, 2)
        for part in range(3):
            src = part * N_ATT_HEADS + h
            sel[src, p * LANES + 6 * e + part] = 1.0
            sel[src, ATT_W + p * LANES + 6 * e + 3 + part] = -1.0
            cst[0, p * LANES + 6 * e + 3 + part] = 1.0
            cst[0, ATT_W + p * LANES + 6 * e + part] = 1.0
    return jnp.asarray(sel, BF16), jnp.asarray(cst, F32)


def _tri_incl(n):
    r = np.arange(n)
    return (r[None, :] <= r[:, None]).astype(np.float32)


def _inproj_ab(h, nw, w_in, f_bias, dt_bias, head_order, head_cols, tm):
    B, L, D = h.shape
    fq, fk, fv, fl, wz, wxbc, wdt = jnp.split(w_in, np.cumsum(
        [ATT_W, ATT_W, ATT_W, N_ATT_HEADS, SSD_W, SSD_CONV_DIM])[:].tolist(), axis=1)
    fq, fk, fv = (jnp.take(w, head_cols, axis=1) for w in (fq, fk, fv))
    fl = jnp.take(fl, head_order, axis=1)
    f_bias = f_bias[head_order]
    pad = jnp.zeros((D, LANES - N_ATT_HEADS - SSD_HEADS), w_in.dtype)
    w = jnp.concatenate([fq, fk, fv, wz, wxbc, fl, wdt, pad], axis=1).astype(BF16)
    sb = jnp.concatenate([f_bias, dt_bias, jnp.zeros((LANES - 16,), F32)]).reshape(1, LANES)
    assert tm % CHUNK == 0
    tri = _tri_incl(CHUNK)
    tri3 = jnp.asarray(np.concatenate([tri, tri, tri], axis=1), BF16)
    sel, cst = _aug_tables()

    row = lambda width: pl.BlockSpec((1, tm, width), lambda b, i: (b, i, 0))
    const = lambda shape: pl.BlockSpec(shape, lambda b, i: (0,) * len(shape))
    outs = [
        jax.ShapeDtypeStruct((B, L, ATT_W), BF16),
        jax.ShapeDtypeStruct((B, L, ATT_W), BF16),
        jax.ShapeDtypeStruct((B, L, ATT_W), BF16),
        jax.ShapeDtypeStruct((B, L, ATT_W), BF16),
        jax.ShapeDtypeStruct((B, L, ATT_W), BF16),
        jax.ShapeDtypeStruct((B, L, SSD_W), F32),
        jax.ShapeDtypeStruct((B, L, SSD_CONV_DIM), F32),
        jax.ShapeDtypeStruct((B, L, LANES), F32),
        jax.ShapeDtypeStruct((B, L // tm, 1, LANES), F32),
    ]
    return pl.pallas_call(
        _inproj_ab_kernel,
        grid=(B, L // tm),
        in_specs=[row(D), const((1, D)), const((D, AB_COLS)), const((1, LANES)),
                  const((CHUNK, 3 * CHUNK)), const((LANES, 2 * ATT_W)), const((1, 2 * ATT_W))],
        out_specs=[row(ATT_W)] * 5 + [row(SSD_W), row(SSD_CONV_DIM), row(LANES),
                   pl.BlockSpec((1, 1, 1, LANES), lambda b, i: (b, i, 0, 0))],
        out_shape=outs,
        scratch_shapes=[pltpu.VMEM((1, LANES), F32)],
        compiler_params=_params(("arbitrary", "arbitrary")),
        name="inproj_ab",
    )(h, nw.reshape(1, D), w, sb, tri3, sel, cst)


def _head_lane_mask(e, shape):
    lane = lax.broadcasted_iota(jnp.int32, shape, 1)
    return (lane < HEAD_DIM) if e == 0 else (lane >= HEAD_DIM)


EXP2_DEAD = -151.0
FOX_STRIP = 64


def _fox_kernel(qd_ref, qa_ref, kd_ref, ka_ref, v_ref, fend_ref, o_ref,
                q_ref, s0_ref, s1_ref, p0_ref, p1_ref, m_ref, al_ref, acc_ref, kn_ref, *, tk):
    it = pl.program_id(2)
    ka = 2 * it
    R = 4 * tk
    lower = slice(2 * tk, R)
    everything = slice(0, R)
    qd = qd_ref[0].astype(F32)
    qa = qa_ref[0].astype(F32)
    lane = lax.broadcasted_iota(jnp.int32, (tk, LANES), 1)
    first = lane < HEAD_DIM
    for half in range(2):
        hs = slice(half * tk, (half + 1) * tk)
        for e in range(2):
            amask = (lane >= 6 * e) & (lane < 6 * e + 6)
            g = 2 * half + e
            q_ref[g * tk:(g + 1) * tk, :] = jnp.concatenate(
                [jnp.where(first if e == 0 else ~first, qd[hs], 0.0),
                 jnp.where(amask, qa[hs], 0.0)], axis=1).astype(BF16)
    m_ref[...] = jnp.full(m_ref.shape, -jnp.inf, F32)
    acc_ref[...] = jnp.zeros(acc_ref.shape, F32)
    ones = jnp.ones((tk, LANES), BF16)

    def head_sq_norms(x):
        lane_x = lax.broadcasted_iota(jnp.int32, x.shape, 1)
        sq = x * x
        return (jnp.sum(jnp.where(lane_x < HEAD_DIM, sq, 0.0), axis=1, keepdims=True),
                jnp.sum(jnp.where(lane_x < HEAD_DIM, 0.0, sq), axis=1, keepdims=True))

    @pl.when(it == 0)
    def _():
        def kbody(c, carry):
            ks = pl.multiple_of(c * tk, tk)
            n0, n1 = head_sq_norms(kd_ref[0, pl.ds(ks, tk), :].astype(F32))
            return jnp.maximum(carry[0], n0), jnp.maximum(carry[1], n1)
        zero = jnp.zeros((tk, 1), F32)
        n0, n1 = lax.fori_loop(0, kd_ref.shape[1] // tk, kbody, (zero, zero))
        for e, n in enumerate((n0, n1)):
            kn_ref[e] = jnp.broadcast_to(jnp.sqrt(jnp.max(n, axis=0, keepdims=True)), kn_ref.shape[1:])

    def scores(j, s_ref, rows=everything):
        ks = pl.multiple_of(j * tk, tk)
        k = jnp.concatenate([kd_ref[0, pl.ds(ks, tk), :], ka_ref[0, pl.ds(ks, tk), :]], axis=1)
        s_ref[rows, :] = _dot_nt(q_ref[rows, :], k)

    def update(j, s_ref, p_ref, rows=everything, diag_rows=0):
        for r0 in range(rows.start, rows.stop, FOX_STRIP):
            rs = slice(r0, r0 + FOX_STRIP)
            s = s_ref[rs, :]
            if r0 < diag_rows:
                rws = lax.broadcasted_iota(jnp.int32, s.shape, 0) + (r0 % tk)
                cols = lax.broadcasted_iota(jnp.int32, s.shape, 1)
                s = jnp.where(rws >= cols, s, -jnp.inf)
            m_old = m_ref[rs, :]
            m_new = jnp.maximum(m_old, jnp.max(s, axis=1, keepdims=True))
            al_ref[rs, :] = jnp.exp2(m_old - m_new)
            m_ref[rs, :] = m_new
            p_ref[rs, :] = jnp.exp2(s - m_new).astype(BF16)
        ks = pl.multiple_of(j * tk, tk)
        v1 = jnp.concatenate([v_ref[0, pl.ds(ks, tk), :], ones], axis=1)
        acc_ref[rows, :] = al_ref[rows, :] * acc_ref[rows, :] + _dot(p_ref[rows, :], v1)

    scores(ka + 1, s0_ref, lower)
    scores(ka, s1_ref)
    update(ka + 1, s0_ref, p0_ref, rows=lower, diag_rows=R)
    scores(jnp.maximum(ka - 1, 0), s0_ref)
    update(ka, s1_ref, p1_ref, diag_rows=2 * tk)

    qn = head_sq_norms(qd)
    fend = fend_ref[0]
    hl = lax.broadcasted_iota(jnp.int32, fend.shape, 1)
    jcol = lax.broadcasted_iota(jnp.int32, (fend.shape[0], 1), 0)
    jmin = ka
    for e in range(2):
        head = 2 * pl.program_id(1) + e
        fe = jnp.sum(jnp.where(hl == head, fend, 0.0), axis=1, keepdims=True)
        f_top = jnp.sum(jnp.where(jcol == ka - 1, fe, 0.0), axis=0, keepdims=True)
        qk = jnp.sqrt(jnp.max(qn[e], axis=0, keepdims=True)) * kn_ref[e, 0:1, 0:1]
        m_min = jnp.minimum(jnp.min(m_ref[e * tk:(e + 1) * tk, :], axis=0, keepdims=True),
                            jnp.min(m_ref[(2 + e) * tk:(3 + e) * tk, :], axis=0, keepdims=True))
        live = (jcol < ka) & (qk + f_top - fe + 1.0 - m_min > EXP2_DEAD)
        jmin = jnp.minimum(jmin, jnp.min(jnp.where(live, jcol, ka)))
    nb = ka - jmin

    def pair(u, carry):
        j = ka - 1 - 2 * u
        scores(j - 1, s1_ref)
        update(j, s0_ref, p0_ref)
        scores(jnp.maximum(j - 2, 0), s0_ref)
        update(j - 1, s1_ref, p1_ref)
        return carry

    lax.fori_loop(0, nb // 2, pair, 0)

    @pl.when(nb % 2 == 1)
    def _():
        update(jmin, s0_ref, p0_ref)

    acc = acc_ref[...]
    out = acc[:, :LANES] / acc[:, LANES:]
    for half in range(2):
        g = 2 * half
        o_ref[0, half * tk:(half + 1) * tk, :] = jnp.where(
            first, out[g * tk:(g + 1) * tk], out[(g + 1) * tk:(g + 2) * tk]).astype(BF16)


def _fox_attention(qd, qa, kd, ka, v, fend, tk):
    B, L, _ = qd.shape
    assert fend.shape == (B, L // tk, LANES)
    n_pairs = N_ATT_HEADS // 2
    tq = 2 * tk
    R = 2 * tq
    qspec = pl.BlockSpec((1, tq, LANES), lambda b, p, i: (b, i, p))
    kspec = pl.BlockSpec((1, L, LANES), lambda b, p, i: (b, 0, p))
    return pl.pallas_call(
        functools.partial(_fox_kernel, tk=tk),
        grid=(B, n_pairs, L // tq),
        in_specs=[qspec, qspec, kspec, kspec, kspec,
                  pl.BlockSpec((1, L // tk, LANES), lambda b, p, i: (b, 0, 0))],
        out_specs=qspec,
        out_shape=jax.ShapeDtypeStruct((B, L, ATT_W), BF16),
        scratch_shapes=[pltpu.VMEM((R, 2 * LANES), BF16),
                        pltpu.VMEM((R, tk), F32), pltpu.VMEM((R, tk), F32),
                        pltpu.VMEM((R, tk), BF16), pltpu.VMEM((R, tk), BF16),
                        pltpu.VMEM((R, 1), F32), pltpu.VMEM((R, 1), F32),
                        pltpu.VMEM((R, 2 * LANES), F32),
                        pltpu.VMEM((2, SUBLANES, LANES), F32)],
        compiler_params=_params(("arbitrary", "arbitrary", "arbitrary")),
        name="fox_attention",
    )(qd, qa, kd, ka, v, fend)


def _ssd_kernel(xbc_ref, z_ref, dt_ref, cw_ref, cb_ref, alog_ref, dsk_ref, nw_ref, tri3_ref,
                y_ref, xext_ref, st_ref):
    @pl.when(pl.program_id(0) == 0)
    def _():
        xext_ref[:, 0:SUBLANES, :] = jnp.zeros((xext_ref.shape[0], SUBLANES, SSD_CONV_DIM), F32)
        st_ref[...] = jnp.zeros_like(st_ref)

    for b in range(xbc_ref.shape[0]):
        _ssd_chunk(xbc_ref.at[b], z_ref.at[b], dt_ref.at[b], cw_ref, cb_ref, alog_ref, dsk_ref,
                   nw_ref, tri3_ref, y_ref.at[b], xext_ref.at[b], st_ref.at[b])


def _ssd_chunk(xbc_ref, z_ref, dt_ref, cw_ref, cb_ref, alog_ref, dsk_ref, nw_ref, tri3_ref,
               y_ref, xext_ref, st_ref):
    Q = CHUNK
    xext_ref[SUBLANES:SUBLANES + Q, :] = xbc_ref[...]
    conv = cb_ref[...]
    for k in range(SSD_CONV):
        conv = conv + cw_ref[k:k + 1, :] * xext_ref[pl.ds(SUBLANES - (SSD_CONV - 1) + k, Q), :]
    xext_ref[0:SUBLANES, :] = xext_ref[Q:Q + SUBLANES, :]
    xc = _silu(conv)
    xs = xc[:, :SSD_W]
    bm = xc[:, SSD_W:SSD_W + LANES]
    cm = xc[:, SSD_W + LANES:]

    dt = dt_ref[...]
    a = dt * (-jnp.exp(alog_ref[...]))
    hi, mid, lo = _split3(a)
    a_cum = _dot(tri3_ref[...], jnp.concatenate([hi, mid, lo], axis=0))
    a_row = a_cum.T
    a_last_col = a_cum[Q - 1:Q, :]
    bm_t = bm.T

    rows = lax.broadcasted_iota(jnp.int32, (Q, Q), 0)
    cols = lax.broadcasted_iota(jnp.int32, (Q, Q), 1)
    causal = rows >= cols
    lane = lax.broadcasted_iota(jnp.int32, (Q, LANES), 1)
    first_half = lane < HEAD_DIM

    scores = []
    cmask = []
    for g in range(SSD_GROUPS):
        cg = jnp.where(_head_lane_mask(g, (Q, LANES)), cm, 0.0).astype(BF16)
        cmask.append(cg)
        scores.append(_dot_nt(cg, bm.astype(BF16)))

    y_pairs = []
    heads_per_group = SSD_HEADS // SSD_GROUPS
    for p in range(SSD_HEADS // 2):
        xs_pair = xs[:, p * LANES:(p + 1) * LANES]
        dt_pair = jnp.where(first_half, dt[:, DT_LANE0 + 2 * p:DT_LANE0 + 2 * p + 1],
                            dt[:, DT_LANE0 + 2 * p + 1:DT_LANE0 + 2 * p + 2])
        xdt = (xs_pair * dt_pair).astype(BF16)
        y_head = []
        for e in range(2):
            h = 2 * p + e
            g = h // heads_per_group
            hl = DT_LANE0 + h
            acol = a_cum[:, hl:hl + 1]
            arow = a_row[hl:hl + 1, :]
            alast = a_last_col[:, hl:hl + 1]
            lmat = jnp.exp(jnp.where(causal, acol - arow, -jnp.inf))
            y_diag = _dot((scores[g] * lmat).astype(BF16), xdt)
            prev = st_ref[h]
            y_off = _dot(cmask[g], prev.astype(BF16)) * jnp.exp(acol)
            y_head.append(y_diag + y_off)
            decay_row = jnp.exp(alast - arow)
            local = _dot((bm_t * decay_row).astype(BF16), xdt)
            st_ref[h] = prev * jnp.exp(alast) + local
        y_pairs.append(jnp.where(first_half, y_head[0], y_head[1]))
    y = jnp.concatenate(y_pairs, axis=1) + dsk_ref[...] * xs
    y = y * _silu(z_ref[...])
    gw = SSD_W // SSD_GROUPS
    y = jnp.concatenate([_rms(y[:, g * gw:(g + 1) * gw], nw_ref[:, g * gw:(g + 1) * gw])
                         for g in range(SSD_GROUPS)], axis=1)
    y_ref[...] = y.astype(BF16)


def _ssd(xbc, z, dt, conv_w, conv_b, a_log, d_skip, norm_w):
    B, L, _ = xbc.shape
    Q = CHUNK
    alog = jnp.zeros((1, LANES), F32).at[0, DT_LANE0:DT_LANE0 + SSD_HEADS].set(a_log)
    dsk = jnp.repeat(d_skip, SSD_W // SSD_HEADS).reshape(1, SSD_W)
    tri = _tri_incl(Q)
    tri3 = jnp.asarray(np.concatenate([tri, tri, tri], axis=1), BF16)
    row = lambda width: pl.BlockSpec((B, Q, width), lambda c: (0, c, 0))
    const = lambda shape: pl.BlockSpec(shape, lambda c: (0,) * len(shape))
    return pl.pallas_call(
        _ssd_kernel,
        grid=(L // Q,),
        in_specs=[row(SSD_CONV_DIM), row(SSD_W), row(LANES), const((SSD_CONV, SSD_CONV_DIM)),
                  const((1, SSD_CONV_DIM)), const((1, LANES)), const((1, SSD_W)),
                  const((1, SSD_W)), const((Q, 3 * Q))],
        out_specs=row(SSD_W),
        out_shape=jax.ShapeDtypeStruct((B, L, SSD_W), BF16),
        scratch_shapes=[pltpu.VMEM((B, Q + SUBLANES, SSD_CONV_DIM), F32),
                        pltpu.VMEM((B, SSD_HEADS, LANES, LANES), F32)],
        compiler_params=_params(("arbitrary",)),
        name="ssd_scan",
    )(xbc, z, dt, conv_w, conv_b.reshape(1, -1), alog, dsk, norm_w.reshape(1, -1), tri3)


def _mix_mlp_kernel(h_ref, y1_ref, y2_ref, wo_ref, nw_ref, wup_ref, wdn_ref, nf_ref, o_ref,
                    *, ff_chunk, final):
    half = y1_ref.shape[-1]
    mix = (_dot(y1_ref[...], wo_ref[0:half, :].astype(BF16))
           + _dot(y2_ref[...], wo_ref[half:2 * half, :].astype(BF16)))
    h1 = h_ref[...] + mix
    u = _rms(h1, nw_ref[...]).astype(BF16)
    d_ff = wup_ref.shape[1]
    acc = jnp.zeros_like(h1)
    for c in range(d_ff // ff_chunk):
        act = jnp.maximum(_dot(u, wup_ref[:, c * ff_chunk:(c + 1) * ff_chunk].astype(BF16)), 0.0)
        acc = acc + _dot((act * act).astype(BF16),
                         wdn_ref[c * ff_chunk:(c + 1) * ff_chunk, :].astype(BF16))
    h2 = h1 + acc
    if final:
        h2 = _rms(h2, nf_ref[...])
    o_ref[...] = h2


def _mix_mlp(h, y1, y2, w_out, nw, w_up, w_down, layer, nf, tm, final):
    B, L, D = h.shape
    T = B * L
    d_ff = w_up.shape[2]
    half = y1.shape[-1]
    row = lambda width: pl.BlockSpec((tm, width), lambda i: (i, 0))
    const = lambda shape: pl.BlockSpec(shape, lambda i: (0,) * len(shape))
    stacked = lambda shape: pl.BlockSpec((None,) + shape, lambda i: (layer, 0, 0))
    out = pl.pallas_call(
        functools.partial(_mix_mlp_kernel, ff_chunk=min(1024, d_ff), final=final),
        grid=(T // tm,),
        in_specs=[row(D), row(half), row(half), const((2 * half, D)), const((1, D)),
                  stacked((D, d_ff)), stacked((d_ff, D)), const((1, D))],
        out_specs=row(D),
        out_shape=jax.ShapeDtypeStruct((T, D), F32),
        compiler_params=_params(("arbitrary",)),
        name="mix_mlp",
    )(h.reshape(T, D), y1.reshape(T, half), y2.reshape(T, half), w_out,
      nw.reshape(1, D), w_up, w_down, nf.reshape(1, D))
    return out.reshape(B, L, D)


GLA_QK = GLA_HEADS * GLA_DK
GLA_V = GLA_HEADS * GLA_DV
CD_COLS = 3 * ATT_W + 2 * GLA_QK + 2 * GLA_V + LANES


def _inproj_cd_kernel(x_ref, nw_ref, w_ref, sq_ref, sk_ref, sv_ref, gq_ref, gk_ref, gv_ref,
                      gr_ref, glow_ref):
    u = _rms(x_ref[...], nw_ref[...]).astype(BF16)

    def mm(a, b):
        return _dot(u, w_ref[:, a:b])

    o = 0
    sq_ref[...] = (mm(o, o + ATT_W) * (LOG2E * HEAD_DIM ** -0.5)).astype(BF16); o += ATT_W
    sk_ref[...] = mm(o, o + ATT_W).astype(BF16); o += ATT_W
    sv_ref[...] = mm(o, o + ATT_W).astype(BF16); o += ATT_W
    gq_ref[...] = mm(o, o + GLA_QK) * (GLA_DK ** -0.5); o += GLA_QK
    gk_ref[...] = mm(o, o + GLA_QK); o += GLA_QK
    gv_ref[...] = mm(o, o + GLA_V); o += GLA_V
    gr_ref[...] = mm(o, o + GLA_V); o += GLA_V
    glow_ref[...] = mm(o, o + LANES)


def _inproj_cd(h, nw, w_in, tm):
    B, L, D = h.shape
    T = B * L
    sq, sk, sv, gq, gk, gv, glow, gr = jnp.split(w_in, np.cumsum(
        [ATT_W, ATT_W, ATT_W, GLA_QK, GLA_QK, GLA_V, GLA_RANK]).tolist(), axis=1)
    pad = jnp.zeros((D, LANES - GLA_RANK), w_in.dtype)
    w = jnp.concatenate([sq, sk, sv, gq, gk, gv, gr, glow, pad], axis=1).astype(BF16)
    row = lambda width: pl.BlockSpec((tm, width), lambda i: (i, 0))
    const = lambda shape: pl.BlockSpec(shape, lambda i: (0,) * len(shape))
    widths = [ATT_W, ATT_W, ATT_W, GLA_QK, GLA_QK, GLA_V, GLA_V, LANES]
    dtypes = [BF16, BF16, BF16, F32, F32, F32, F32, F32]
    outs = pl.pallas_call(
        _inproj_cd_kernel,
        grid=(T // tm,),
        in_specs=[row(D), const((1, D)), const((D, CD_COLS))],
        out_specs=[row(wd) for wd in widths],
        out_shape=[jax.ShapeDtypeStruct((T, wd), dt) for wd, dt in zip(widths, dtypes)],
        compiler_params=_params(("arbitrary",)),
        name="inproj_cd",
    )(h.reshape(T, D), nw.reshape(1, D), w)
    return [o.reshape(B, L, -1) for o in outs]


SB_DEAD = EXP2_DEAD - 1.0
SB_STRIP = 64
SB_CHAINS = 2


def _sb_kernel(q_ref, k_ref, v_ref, u2_ref, o_ref,
               qs_ref, z_ref, sp_ref, in_ref, a_ref, c_ref, acc_ref, *, tq):
    i = pl.program_id(2)
    tk = tq
    R = 2 * tq
    chains = range(SB_CHAINS)
    for ch in chains:
        qf = q_ref[0, :, ch * LANES:(ch + 1) * LANES].astype(F32)
        for e in range(2):
            qs_ref[ch, e * tq:(e + 1) * tq, :] = jnp.where(_head_lane_mask(e, (tq, LANES)), qf,
                                                           0.0).astype(BF16)
    c_ref[...] = jnp.zeros(c_ref.shape, F32)
    acc_ref[...] = jnp.zeros(acc_ref.shape, F32)

    def strict_mask(r0, shape):
        rows = lax.broadcasted_iota(jnp.int32, shape, 0) + (r0 % tq)
        return lax.broadcasted_iota(jnp.int32, shape, 1) < rows

    def step(j, masked):
        ks = pl.multiple_of(j * tk, tk)
        for ch in chains:
            z_ref[ch] = _dot_nt(qs_ref[ch], k_ref[0, pl.ds(ks, tk), ch * LANES:(ch + 1) * LANES])
        for ch in chains:
            for r0 in range(0, R, SB_STRIP):
                rs = slice(r0, r0 + SB_STRIP)
                z = z_ref[ch, rs, :]
                sp = jnp.maximum(z, 0.0) + jnp.log2(1.0 + jnp.exp2(-jnp.abs(z)))
                if masked:
                    sp = jnp.where(strict_mask(r0, sp.shape), sp, 0.0)
                sp_ref[ch, rs, :] = sp.astype(BF16)
        for ch in chains:
            in_ref[ch] = _dot(sp_ref[ch], u2_ref[...])
        for ch in chains:
            for r0 in range(0, R, SB_STRIP):
                rs = slice(r0, r0 + SB_STRIP)
                inc = in_ref[ch, rs, :]
                a = jnp.exp2(z_ref[ch, rs, :] - c_ref[ch, rs, :] - inc)
                if masked:
                    a = jnp.where(strict_mask(r0, a.shape), a, 0.0)
                a_ref[ch, rs, :] = a.astype(BF16)
                c_ref[ch, rs, :] += inc[:, 0:1]
        for ch in chains:
            acc_ref[ch] += _dot(a_ref[ch], v_ref[0, pl.ds(ks, tk), ch * LANES:(ch + 1) * LANES])
        return jnp.min(c_ref[...])

    cmin = step(i, True)

    def cond(carry):
        j, cmin = carry
        return (j >= 0) & (cmin < -SB_DEAD)

    def body(carry):
        j, _ = carry
        return j - 1, step(j, False)

    lax.while_loop(cond, body, (i - 1, cmin))
    for ch in chains:
        acc = acc_ref[ch]
        o_ref[0, :, ch * LANES:(ch + 1) * LANES] = jnp.where(
            _head_lane_mask(0, (tq, LANES)), acc[:tq], acc[tq:]).astype(BF16)


def _sb_attention(q, k, v, tq):
    B, L, _ = q.shape
    n_groups = N_ATT_HEADS // (2 * SB_CHAINS)
    R = 2 * tq
    W = SB_CHAINS * LANES
    r = np.arange(tq)
    u = (r[:, None] >= r[None, :]).astype(np.float32)
    u2 = jnp.asarray(u, BF16)
    qspec = pl.BlockSpec((1, tq, W), lambda b, p, i: (b, i, p))
    kspec = pl.BlockSpec((1, L, W), lambda b, p, i: (b, 0, p))
    return pl.pallas_call(
        functools.partial(_sb_kernel, tq=tq),
        grid=(B, n_groups, L // tq),
        in_specs=[qspec, kspec, kspec, pl.BlockSpec((tq, tq), lambda b, p, i: (0, 0))],
        out_specs=qspec,
        out_shape=jax.ShapeDtypeStruct((B, L, ATT_W), BF16),
        scratch_shapes=[pltpu.VMEM((SB_CHAINS, R, LANES), BF16), pltpu.VMEM((SB_CHAINS, R, tq), F32),
                        pltpu.VMEM((SB_CHAINS, R, tq), BF16),
                        pltpu.VMEM((SB_CHAINS, R, tq), F32),
                        pltpu.VMEM((SB_CHAINS, R, tq), BF16), pltpu.VMEM((SB_CHAINS, R, 1), F32),
                        pltpu.VMEM((SB_CHAINS, R, LANES), F32)],
        compiler_params=_params(("arbitrary", "arbitrary", "arbitrary")),
        name="sb_attention",
    )(q, k, v, u2)


def _gla_tables():
    Q = CHUNK
    r = np.arange(Q)
    j = np.arange(Q)
    coef = [(j[None, :] <= r[:, None]), (j[None, :] > r[:, None])]
    masks = [np.eye(Q, dtype=bool)]
    for lvl in range(GLA_LEVELS):
        m = 1 << lvl
        c0 = (r // (2 * m)) * (2 * m)
        mid = c0 + m - 1
        upper = (r - c0) >= m
        up = (j[None, :] > mid[:, None]) & (j[None, :] <= r[:, None])
        lowr = (j[None, :] > r[:, None]) & (j[None, :] <= mid[:, None])
        coef.append(np.where(upper[:, None], up, lowr))
        masks.append((c0[:, None] == c0[None, :]) & upper[:, None] & (~upper)[None, :])
    coef = np.concatenate(coef, axis=0).astype(np.float32)
    coef2 = np.concatenate([coef, coef], axis=1)
    masks = np.stack(masks).astype(np.float32)
    hv = np.arange(GLA_V) // GLA_DV
    hk = np.arange(GLA_QK) // GLA_DK
    bdiag = (hv[:, None] == hk[None, :]).astype(np.float32)
    return jnp.asarray(coef2, BF16), jnp.asarray(masks, F32), jnp.asarray(bdiag, F32)


def _gla_kernel(gq_ref, gk_ref, gv_ref, glow_ref, gr_ref, w2_ref, gb_ref, coef_ref, mask_ref,
                bdiag_ref, nw_ref, o_ref, st_ref):
    @pl.when(pl.program_id(0) == 0)
    def _():
        st_ref[...] = jnp.zeros_like(st_ref)

    for b in range(gq_ref.shape[0]):
        _gla_chunk(gq_ref.at[b], gk_ref.at[b], gv_ref.at[b], glow_ref.at[b], gr_ref.at[b], w2_ref,
                   gb_ref, coef_ref, mask_ref, bdiag_ref, nw_ref, o_ref.at[b], st_ref.at[b])


def _gla_chunk(gq_ref, gk_ref, gv_ref, glow_ref, gr_ref, w2_ref, gb_ref, coef_ref, mask_ref,
               bdiag_ref, nw_ref, o_ref, st_ref):
    Q = CHUNK
    logits = _dot(glow_ref[...].astype(BF16), w2_ref[...]) + gb_ref[...]
    la = _log_sigmoid(logits) * (1.0 / GLA_GATE_NORM)
    hi, lo = _split2(la)
    expo = _dot(coef_ref[...], jnp.concatenate([hi, lo], axis=0))
    q = gq_ref[...]
    k = gk_ref[...]
    v = gv_ref[...]
    lane_head = lax.broadcasted_iota(jnp.int32, (Q, GLA_QK), 1) // GLA_DK
    row = lax.broadcasted_iota(jnp.int32, (Q, GLA_QK), 0)
    hmask = [lane_head == h for h in range(GLA_HEADS)]

    att = [None] * GLA_HEADS
    for lvl in range(-1, GLA_LEVELS):
        if lvl < 0:
            xq, xk = q, k.astype(BF16)
        else:
            m = 1 << lvl
            upper = (row & (2 * m - 1)) >= m
            xq = jnp.where(upper, q, k) * jnp.exp(expo[(2 + lvl) * Q:(3 + lvl) * Q, :])
            xk = xq.astype(BF16)
        msk = mask_ref[lvl + 1]
        lhs = jnp.concatenate([jnp.where(hmask[h], xq, 0.0) for h in range(GLA_HEADS)], axis=0)
        prod = _dot_nt(lhs.astype(BF16), xk)
        for h in range(GLA_HEADS):
            part = prod[h * Q:(h + 1) * Q] * msk
            att[h] = part if att[h] is None else att[h] + part

    st = st_ref[...]
    q_in = (q * jnp.exp(expo[0:Q, :])).astype(BF16)
    o = _dot_nt(q_in, st.astype(BF16))
    o_intra = [_dot(att[h].astype(BF16), v[:, h * GLA_DV:(h + 1) * GLA_DV].astype(BF16))
               for h in range(GLA_HEADS)]
    o = o + jnp.concatenate(o_intra, axis=1)

    k_dec = (k * jnp.exp(expo[Q:2 * Q, :])).astype(BF16)
    upd = _dot(v.T.astype(BF16), k_dec)
    g_last = expo[Q - 1:Q, :]
    st_ref[...] = st * jnp.exp(g_last) + upd * bdiag_ref[...]

    gr = gr_ref[...]
    o = jnp.concatenate([_rms(o[:, h * GLA_DV:(h + 1) * GLA_DV], nw_ref[...])
                         for h in range(GLA_HEADS)], axis=1)
    o_ref[...] = (o * _silu(gr)).astype(BF16)


def _gla(gq, gk, gv, glow, gr, gate_w2, gate_b, norm_w):
    B, L, _ = gq.shape
    Q = CHUNK
    coef2, masks, bdiag = _gla_tables()
    w2 = jnp.zeros((LANES, GLA_QK), F32).at[:GLA_RANK].set(gate_w2).astype(BF16)
    row = lambda width: pl.BlockSpec((B, Q, width), lambda c: (0, c, 0))
    const = lambda shape: pl.BlockSpec(shape, lambda c: (0,) * len(shape))
    return pl.pallas_call(
        _gla_kernel,
        grid=(L // Q,),
        in_specs=[row(GLA_QK), row(GLA_QK), row(GLA_V), row(LANES), row(GLA_V),
                  const((LANES, GLA_QK)), const((1, GLA_QK)), const(coef2.shape),
                  const(masks.shape), const(bdiag.shape), const((1, GLA_DV))],
        out_specs=row(GLA_V),
        out_shape=jax.ShapeDtypeStruct((B, L, GLA_V), BF16),
        scratch_shapes=[pltpu.VMEM((B, GLA_V, GLA_QK), F32)],
        compiler_params=_params(("arbitrary",)),
        name="gla_scan",
    )(gq, gk, gv, glow, gr, w2, gate_b.reshape(1, -1), coef2, masks, bdiag,
      norm_w.reshape(1, -1))


def _block(n, want):
    return want if n % want == 0 else n


def kernel(x, norm_mix, norm_mlp, norm_final, w_in_ab, fox_f_bias, ssd_conv_w, ssd_conv_b,
           ssd_dt_bias, ssd_a_log, ssd_d, ssd_norm, w_out_ab, w_in_cd, gla_gate_w2,
           gla_gate_b, gla_norm, w_out_cd, w_mlp_up, w_mlp_down):
    B, L, D = x.shape
    assert L % CHUNK == 0
    tm = _block(L, 512)
    tq = _block(L, 256)
    tk_fox = _block(L // 2, 512)

    assert tm == tk_fox
    head_order = jnp.argsort(fox_f_bias[0])
    head_cols = (head_order[:, None] * HEAD_DIM + jnp.arange(HEAD_DIM)[None, :]).reshape(-1)
    qd, kd, vd, qa, ka, z, xbc, dt, fend = _inproj_ab(x, norm_mix[0], w_in_ab[0], fox_f_bias[0],
                                                      ssd_dt_bias[0], head_order, head_cols, tm)
    w_out0 = jnp.concatenate([jnp.take(w_out_ab[0, :ATT_W], head_cols, axis=0),
                              w_out_ab[0, ATT_W:]], axis=0)
    y_fox = _fox_attention(qd, qa, kd, ka, vd, fend.reshape(B, L // tm, LANES), tk_fox)
    y_ssd = _ssd(xbc, z, dt, ssd_conv_w[0], ssd_conv_b[0], ssd_a_log[0], ssd_d[0], ssd_norm[0])
    w_up = w_mlp_up
    w_down = w_mlp_down
    h = _mix_mlp(x, y_fox, y_ssd, w_out0, norm_mlp[0], w_up, w_down, 0, norm_final, tm,
                 final=False)

    sq, sk, sv, gq, gk, gv, gr, glow = _inproj_cd(h, norm_mix[1], w_in_cd[0], tm)
    y_sb = _sb_attention(sq, sk, sv, tq)
    y_gla = _gla(gq, gk, gv, glow, gr, gla_gate_w2[0], gla_gate_b[0], gla_norm[0])
    return _mix_mlp(h, y_sb, y_gla, w_out_cd[0], norm_mlp[1], w_up, w_down, 1, norm_final, tm,
                    final=True)
```

```python
import functools
import math

import numpy as np
import jax
import jax.numpy as jnp
from jax import lax
from jax.experimental import pallas as pl
from jax.experimental.pallas import tpu as pltpu

F32 = jnp.float32
BF16 = jnp.bfloat16

LANES = 128
SUBLANES = 8
VMEM_LIMIT_BYTES = 56 * 1024 * 1024

HEAD_DIM = 64
N_ATT_HEADS = 8
ATT_W = N_ATT_HEADS * HEAD_DIM
SSD_HEADS = 8
SSD_W = 512
SSD_GROUPS = 2
SSD_STATE = 64
SSD_CONV = 4
SSD_CONV_DIM = SSD_W + 2 * SSD_GROUPS * SSD_STATE
GLA_HEADS = 4
GLA_DK = 64
GLA_DV = 128
GLA_RANK = 16
GLA_GATE_NORM = 16.0
EPS = 1e-5
CHUNK = 128
GLA_LEVELS = 7
LOG2E = math.log2(math.e)
DT_LANE0 = 8


def _params(sem):
    return pltpu.CompilerParams(dimension_semantics=sem, vmem_limit_bytes=VMEM_LIMIT_BYTES)


def _split3(x):
    hi = x.astype(BF16)
    r = x - hi.astype(F32)
    mid = r.astype(BF16)
    lo = (r - mid.astype(F32)).astype(BF16)
    return hi, mid, lo


def _split2(x):
    hi = x.astype(BF16)
    lo = (x - hi.astype(F32)).astype(BF16)
    return hi, lo


def _softplus(x):
    return jnp.maximum(x, 0.0) + jnp.log1p(jnp.exp(-jnp.abs(x)))


def _log_sigmoid(x):
    return jnp.minimum(x, 0.0) - jnp.log1p(jnp.exp(-jnp.abs(x)))


def _silu(x):
    h = 0.5 * x
    return h + h * jnp.tanh(h)


def _rms(x, w):
    ms = jnp.mean(x * x, axis=-1, keepdims=True)
    return x * lax.rsqrt(ms + EPS) * w


def _dot(a, b):
    return jnp.dot(a, b, preferred_element_type=F32)


def _dot_nt(a, b):
    return lax.dot_general(a, b, (((1,), (1,)), ((), ())), preferred_element_type=F32)


AB_COLS = 3 * ATT_W + SSD_W + SSD_CONV_DIM + LANES


def _inproj_ab_kernel(x_ref, nw_ref, w_ref, sb_ref, tri3_ref, sel_ref, cst_ref,
                      qd_ref, kd_ref, vd_ref, qa_ref, ka_ref, z_ref, xbc_ref, dt_ref, fend_ref,
                      carry_ref):
    @pl.when(pl.program_id(1) == 0)
    def _():
        carry_ref[...] = jnp.zeros_like(carry_ref)

    u = _rms(x_ref[0], nw_ref[...]).astype(BF16)

    def mm(a, b):
        return _dot(u, w_ref[:, a:b])

    o = 0
    qd_ref[0] = (mm(o, o + ATT_W) * (LOG2E * HEAD_DIM ** -0.5)).astype(BF16); o += ATT_W
    kd_ref[0] = mm(o, o + ATT_W).astype(BF16); o += ATT_W
    vd_ref[0] = mm(o, o + ATT_W).astype(BF16); o += ATT_W
    z_ref[0] = mm(o, o + SSD_W); o += SSD_W
    xbc_ref[0] = mm(o, o + SSD_CONV_DIM); o += SSD_CONV_DIM
    small = mm(o, o + LANES) + sb_ref[...]

    lane = lax.broadcasted_iota(jnp.int32, small.shape, 1)
    is_f = lane < N_ATT_HEADS
    is_dt = (lane >= DT_LANE0) & (lane < DT_LANE0 + SSD_HEADS)
    log_f = jnp.where(is_f, _log_sigmoid(small), 0.0)
    dt_ref[0] = jnp.where(is_dt, _softplus(small), 0.0)

    hi, mid, lo = _split3(log_f)
    tm = log_f.shape[0]
    carry = carry_ref[...]
    cums = []
    for r0 in range(0, tm, CHUNK):
        rs = slice(r0, r0 + CHUNK)
        piece = _dot(tri3_ref[...], jnp.concatenate([hi[rs], mid[rs], lo[rs]], axis=0)) + carry
        carry = piece[CHUNK - 1:CHUNK, :]
        cums.append(piece)
    cum = jnp.concatenate(cums, axis=0)
    carry_ref[...] = carry

    cum2 = cum * LOG2E
    fend_ref[0, 0] = cum2[tm - 1:tm, :]
    fh, fm, fl = _split3(cum2)
    fcat = (fh.astype(F32) + pltpu.roll(fm.astype(F32), N_ATT_HEADS, 1)
            + pltpu.roll(fl.astype(F32), 2 * N_ATT_HEADS, 1)).astype(BF16)
    aug = _dot(fcat, sel_ref[...]) + cst_ref[...]
    qa_ref[0] = aug[:, :ATT_W].astype(BF16)
    ka_ref[0] = aug[:, ATT_W:].astype(BF16)


def _aug_tables():
    sel = np.zeros((LANES, 2 * ATT_W), np.float32)
    cst = np.zeros((1, 2 * ATT_W), np.float32)
    for h in range(N_ATT_HEADS):
        p, e = divmod(h, 2)
        for part in range(3):
            src = part * N_ATT_HEADS + h
            sel[src, p * LANES + 6 * e + part] = 1.0
            sel[src, ATT_W + p * LANES + 6 * e + 3 + part] = -1.0
            cst[0, p * LANES + 6 * e + 3 + part] = 1.0
            cst[0, ATT_W + p * LANES + 6 * e + part] = 1.0
    return jnp.asarray(sel, BF16), jnp.asarray(cst, F32)


def _tri_incl(n):
    r = np.arange(n)
    return (r[None, :] <= r[:, None]).astype(np.float32)


def _inproj_ab(h, nw, w_in, f_bias, dt_bias, head_order, head_cols, tm):
    B, L, D = h.shape
    fq, fk, fv, fl, wz, wxbc, wdt = jnp.split(w_in, np.cumsum(
        [ATT_W, ATT_W, ATT_W, N_ATT_HEADS, SSD_W, SSD_CONV_DIM])[:].tolist(), axis=1)
    fq, fk, fv = (jnp.take(w, head_cols, axis=1) for w in (fq, fk, fv))
    fl = jnp.take(fl, head_order, axis=1)
    f_bias = f_bias[head_order]
    pad = jnp.zeros((D, LANES - N_ATT_HEADS - SSD_HEADS), w_in.dtype)
    w = jnp.concatenate([fq, fk, fv, wz, wxbc, fl, wdt, pad], axis=1).astype(BF16)
    sb = jnp.concatenate([f_bias, dt_bias, jnp.zeros((LANES - 16,), F32)]).reshape(1, LANES)
    assert tm % CHUNK == 0
    tri = _tri_incl(CHUNK)
    tri3 = jnp.asarray(np.concatenate([tri, tri, tri], axis=1), BF16)
    sel, cst = _aug_tables()

    row = lambda width: pl.BlockSpec((1, tm, width), lambda b, i: (b, i, 0))
    const = lambda shape: pl.BlockSpec(shape, lambda b, i: (0,) * len(shape))
    outs = [
        jax.ShapeDtypeStruct((B, L, ATT_W), BF16),
        jax.ShapeDtypeStruct((B, L, ATT_W), BF16),
        jax.ShapeDtypeStruct((B, L, ATT_W), BF16),
        jax.ShapeDtypeStruct((B, L, ATT_W), BF16),
        jax.ShapeDtypeStruct((B, L, ATT_W), BF16),
        jax.ShapeDtypeStruct((B, L, SSD_W), F32),
        jax.ShapeDtypeStruct((B, L, SSD_CONV_DIM), F32),
        jax.ShapeDtypeStruct((B, L, LANES), F32),
        jax.ShapeDtypeStruct((B, L // tm, 1, LANES), F32),
    ]
    return pl.pallas_call(
        _inproj_ab_kernel,
        grid=(B, L // tm),
        in_specs=[row(D), const((1, D)), const((D, AB_COLS)), const((1, LANES)),
                  const((CHUNK, 3 * CHUNK)), const((LANES, 2 * ATT_W)), const((1, 2 * ATT_W))],
        out_specs=[row(ATT_W)] * 5 + [row(SSD_W), row(SSD_CONV_DIM), row(LANES),
                   pl.BlockSpec((1, 1, 1, LANES), lambda b, i: (b, i, 0, 0))],
        out_shape=outs,
        scratch_shapes=[pltpu.VMEM((1, LANES), F32)],
        compiler_params=_params(("arbitrary", "arbitrary")),
        name="inproj_ab",
    )(h, nw.reshape(1, D), w, sb, tri3, sel, cst)


def _head_lane_mask(e, shape):
    lane = lax.broadcasted_iota(jnp.int32, shape, 1)
    return (lane < HEAD_DIM) if e == 0 else (lane >= HEAD_DIM)


EXP2_DEAD = -151.0
FOX_STRIP = 64


def _fox_kernel(qd_ref, qa_ref, kd_ref, ka_ref, v_ref, fend_ref, o_ref,
                q_ref, s0_ref, s1_ref, p0_ref, p1_ref, m_ref, al_ref, acc_ref, kn_ref, *, tk):
    it = pl.program_id(2)
    ka = 2 * it
    R = 4 * tk
    lower = slice(2 * tk, R)
    everything = slice(0, R)
    qd = qd_ref[0].astype(F32)
    qa = qa_ref[0].astype(F32)
    lane = lax.broadcasted_iota(jnp.int32, (tk, LANES), 1)
    first = lane < HEAD_DIM
    for half in range(2):
        hs = slice(half * tk, (half + 1) * tk)
        for e in range(2):
            amask = (lane >= 6 * e) & (lane < 6 * e + 6)
            g = 2 * half + e
            q_ref[g * tk:(g + 1) * tk, :] = jnp.concatenate(
                [jnp.where(first if e == 0 else ~first, qd[hs], 0.0),
                 jnp.where(amask, qa[hs], 0.0)], axis=1).astype(BF16)
    m_ref[...] = jnp.full(m_ref.shape, -jnp.inf, F32)
    acc_ref[...] = jnp.zeros(acc_ref.shape, F32)
    ones = jnp.ones((tk, LANES), BF16)

    def head_sq_norms(x):
        lane_x = lax.broadcasted_iota(jnp.int32, x.shape, 1)
        sq = x * x
        return (jnp.sum(jnp.where(lane_x < HEAD_DIM, sq, 0.0), axis=1, keepdims=True),
                jnp.sum(jnp.where(lane_x < HEAD_DIM, 0.0, sq), axis=1, keepdims=True))

    @pl.when(it == 0)
    def _():
        def kbody(c, carry):
            ks = pl.multiple_of(c * tk, tk)
            n0, n1 = head_sq_norms(kd_ref[0, pl.ds(ks, tk), :].astype(F32))
            return jnp.maximum(carry[0], n0), jnp.maximum(carry[1], n1)
        zero = jnp.zeros((tk, 1), F32)
        n0, n1 = lax.fori_loop(0, kd_ref.shape[1] // tk, kbody, (zero, zero))
        for e, n in enumerate((n0, n1)):
            kn_ref[e] = jnp.broadcast_to(jnp.sqrt(jnp.max(n, axis=0, keepdims=True)), kn_ref.shape[1:])

    def scores(j, s_ref, rows=everything):
        ks = pl.multiple_of(j * tk, tk)
        k = jnp.concatenate([kd_ref[0, pl.ds(ks, tk), :], ka_ref[0, pl.ds(ks, tk), :]], axis=1)
        s_ref[rows, :] = _dot_nt(q_ref[rows, :], k)

    def update(j, s_ref, p_ref, rows=everything, diag_rows=0):
        for r0 in range(rows.start, rows.stop, FOX_STRIP):
            rs = slice(r0, r0 + FOX_STRIP)
            s = s_ref[rs, :]
            if r0 < diag_rows:
                rws = lax.broadcasted_iota(jnp.int32, s.shape, 0) + (r0 % tk)
                cols = lax.broadcasted_iota(jnp.int32, s.shape, 1)
                s = jnp.where(rws >= cols, s, -jnp.inf)
            m_old = m_ref[rs, :]
            m_new = jnp.maximum(m_old, jnp.max(s, axis=1, keepdims=True))
            al_ref[rs, :] = jnp.exp2(m_old - m_new)
            m_ref[rs, :] = m_new
            p_ref[rs, :] = jnp.exp2(s - m_new).astype(BF16)
        ks = pl.multiple_of(j * tk, tk)
        v1 = jnp.concatenate([v_ref[0, pl.ds(ks, tk), :], ones], axis=1)
        acc_ref[rows, :] = al_ref[rows, :] * acc_ref[rows, :] + _dot(p_ref[rows, :], v1)

    scores(ka + 1, s0_ref, lower)
    scores(ka, s1_ref)
    update(ka + 1, s0_ref, p0_ref, rows=lower, diag_rows=R)
    scores(jnp.maximum(ka - 1, 0), s0_ref)
    update(ka, s1_ref, p1_ref, diag_rows=2 * tk)

    qn = head_sq_norms(qd)
    fend = fend_ref[0]
    hl = lax.broadcasted_iota(jnp.int32, fend.shape, 1)
    jcol = lax.broadcasted_iota(jnp.int32, (fend.shape[0], 1), 0)
    jmin = ka
    for e in range(2):
        head = 2 * pl.program_id(1) + e
        fe = jnp.sum(jnp.where(hl == head, fend, 0.0), axis=1, keepdims=True)
        f_top = jnp.sum(jnp.where(jcol == ka - 1, fe, 0.0), axis=0, keepdims=True)
        qk = jnp.sqrt(jnp.max(qn[e], axis=0, keepdims=True)) * kn_ref[e, 0:1, 0:1]
        m_min = jnp.minimum(jnp.min(m_ref[e * tk:(e + 1) * tk, :], axis=0, keepdims=True),
                            jnp.min(m_ref[(2 + e) * tk:(3 + e) * tk, :], axis=0, keepdims=True))
        live = (jcol < ka) & (qk + f_top - fe + 1.0 - m_min > EXP2_DEAD)
        jmin = jnp.minimum(jmin, jnp.min(jnp.where(live, jcol, ka)))
    nb = ka - jmin

    def pair(u, carry):
        j = ka - 1 - 2 * u
        scores(j - 1, s1_ref)
        update(j, s0_ref, p0_ref)
        scores(jnp.maximum(j - 2, 0), s0_ref)
        update(j - 1, s1_ref, p1_ref)
        return carry

    lax.fori_loop(0, nb // 2, pair, 0)

    @pl.when(nb % 2 == 1)
    def _():
        update(jmin, s0_ref, p0_ref)

    acc = acc_ref[...]
    out = acc[:, :LANES] / acc[:, LANES:]
    for half in range(2):
        g = 2 * half
        o_ref[0, half * tk:(half + 1) * tk, :] = jnp.where(
            first, out[g * tk:(g + 1) * tk], out[(g + 1) * tk:(g + 2) * tk]).astype(BF16)


def _fox_attention(qd, qa, kd, ka, v, fend, tk):
    B, L, _ = qd.shape
    assert fend.shape == (B, L // tk, LANES)
    n_pairs = N_ATT_HEADS // 2
    tq = 2 * tk
    R = 2 * tq
    qspec = pl.BlockSpec((1, tq, LANES), lambda b, p, i: (b, i, p))
    kspec = pl.BlockSpec((1, L, LANES), lambda b, p, i: (b, 0, p))
    return pl.pallas_call(
        functools.partial(_fox_kernel, tk=tk),
        grid=(B, n_pairs, L // tq),
        in_specs=[qspec, qspec, kspec, kspec, kspec,
                  pl.BlockSpec((1, L // tk, LANES), lambda b, p, i: (b, 0, 0))],
        out_specs=qspec,
        out_shape=jax.ShapeDtypeStruct((B, L, ATT_W), BF16),
        scratch_shapes=[pltpu.VMEM((R, 2 * LANES), BF16),
                        pltpu.VMEM((R, tk), F32), pltpu.VMEM((R, tk), F32),
                        pltpu.VMEM((R, tk), BF16), pltpu.VMEM((R, tk), BF16),
                        pltpu.VMEM((R, 1), F32), pltpu.VMEM((R, 1), F32),
                        pltpu.VMEM((R, 2 * LANES), F32),
                        pltpu.VMEM((2, SUBLANES, LANES), F32)],
        compiler_params=_params(("arbitrary", "arbitrary", "arbitrary")),
        name="fox_attention",
    )(qd, qa, kd, ka, v, fend)


def _ssd_kernel(xbc_ref, z_ref, dt_ref, cw_ref, cb_ref, alog_ref, dsk_ref, nw_ref, tri3_ref,
                y_ref, xext_ref, st_ref):
    @pl.when(pl.program_id(0) == 0)
    def _():
        xext_ref[:, 0:SUBLANES, :] = jnp.zeros((xext_ref.shape[0], SUBLANES, SSD_CONV_DIM), F32)
        st_ref[...] = jnp.zeros_like(st_ref)

    for b in range(xbc_ref.shape[0]):
        _ssd_chunk(xbc_ref.at[b], z_ref.at[b], dt_ref.at[b], cw_ref, cb_ref, alog_ref, dsk_ref,
                   nw_ref, tri3_ref, y_ref.at[b], xext_ref.at[b], st_ref.at[b])


def _ssd_chunk(xbc_ref, z_ref, dt_ref, cw_ref, cb_ref, alog_ref, dsk_ref, nw_ref, tri3_ref,
               y_ref, xext_ref, st_ref):
    Q = CHUNK
    xext_ref[SUBLANES:SUBLANES + Q, :] = xbc_ref[...]
    conv = cb_ref[...]
    for k in range(SSD_CONV):
        conv = conv + cw_ref[k:k + 1, :] * xext_ref[pl.ds(SUBLANES - (SSD_CONV - 1) + k, Q), :]
    xext_ref[0:SUBLANES, :] = xext_ref[Q:Q + SUBLANES, :]
    xc = _silu(conv)
    xs = xc[:, :SSD_W]
    bm = xc[:, SSD_W:SSD_W + LANES]
    cm = xc[:, SSD_W + LANES:]

    dt = dt_ref[...]
    a = dt * (-jnp.exp(alog_ref[...]))
    hi, mid, lo = _split3(a)
    a_cum = _dot(tri3_ref[...], jnp.concatenate([hi, mid, lo], axis=0))
    a_row = a_cum.T
    a_last_col = a_cum[Q - 1:Q, :]
    bm_t = bm.T

    rows = lax.broadcasted_iota(jnp.int32, (Q, Q), 0)
    cols = lax.broadcasted_iota(jnp.int32, (Q, Q), 1)
    causal = rows >= cols
    lane = lax.broadcasted_iota(jnp.int32, (Q, LANES), 1)
    first_half = lane < HEAD_DIM

    scores = []
    cmask = []
    for g in range(SSD_GROUPS):
        cg = jnp.where(_head_lane_mask(g, (Q, LANES)), cm, 0.0).astype(BF16)
        cmask.append(cg)
        scores.append(_dot_nt(cg, bm.astype(BF16)))

    y_pairs = []
    heads_per_group = SSD_HEADS // SSD_GROUPS
    for p in range(SSD_HEADS // 2):
        xs_pair = xs[:, p * LANES:(p + 1) * LANES]
        dt_pair = jnp.where(first_half, dt[:, DT_LANE0 + 2 * p:DT_LANE0 + 2 * p + 1],
                            dt[:, DT_LANE0 + 2 * p + 1:DT_LANE0 + 2 * p + 2])
        xdt = (xs_pair * dt_pair).astype(BF16)
        y_head = []
        for e in range(2):
            h = 2 * p + e
            g = h // heads_per_group
            hl = DT_LANE0 + h
            acol = a_cum[:, hl:hl + 1]
            arow = a_row[hl:hl + 1, :]
            alast = a_last_col[:, hl:hl + 1]
            lmat = jnp.exp(jnp.where(causal, acol - arow, -jnp.inf))
            y_diag = _dot((scores[g] * lmat).astype(BF16), xdt)
            prev = st_ref[h]
            y_off = _dot(cmask[g], prev.astype(BF16)) * jnp.exp(acol)
            y_head.append(y_diag + y_off)
            decay_row = jnp.exp(alast - arow)
            local = _dot((bm_t * decay_row).astype(BF16), xdt)
            st_ref[h] = prev * jnp.exp(alast) + local
        y_pairs.append(jnp.where(first_half, y_head[0], y_head[1]))
    y = jnp.concatenate(y_pairs, axis=1) + dsk_ref[...] * xs
    y = y * _silu(z_ref[...])
    gw = SSD_W // SSD_GROUPS
    y = jnp.concatenate([_rms(y[:, g * gw:(g + 1) * gw], nw_ref[:, g * gw:(g + 1) * gw])
                         for g in range(SSD_GROUPS)], axis=1)
    y_ref[...] = y.astype(BF16)


def _ssd(xbc, z, dt, conv_w, conv_b, a_log, d_skip, norm_w):
    B, L, _ = xbc.shape
    Q = CHUNK
    alog = jnp.zeros((1, LANES), F32).at[0, DT_LANE0:DT_LANE0 + SSD_HEADS].set(a_log)
    dsk = jnp.repeat(d_skip, SSD_W // SSD_HEADS).reshape(1, SSD_W)
    tri = _tri_incl(Q)
    tri3 = jnp.asarray(np.concatenate([tri, tri, tri], axis=1), BF16)
    row = lambda width: pl.BlockSpec((B, Q, width), lambda c: (0, c, 0))
    const = lambda shape: pl.BlockSpec(shape, lambda c: (0,) * len(shape))
    return pl.pallas_call(
        _ssd_kernel,
        grid=(L // Q,),
        in_specs=[row(SSD_CONV_DIM), row(SSD_W), row(LANES), const((SSD_CONV, SSD_CONV_DIM)),
                  const((1, SSD_CONV_DIM)), const((1, LANES)), const((1, SSD_W)),
                  const((1, SSD_W)), const((Q, 3 * Q))],
        out_specs=row(SSD_W),
        out_shape=jax.ShapeDtypeStruct((B, L, SSD_W), BF16),
        scratch_shapes=[pltpu.VMEM((B, Q + SUBLANES, SSD_CONV_DIM), F32),
                        pltpu.VMEM((B, SSD_HEADS, LANES, LANES), F32)],
        compiler_params=_params(("arbitrary",)),
        name="ssd_scan",
    )(xbc, z, dt, conv_w, conv_b.reshape(1, -1), alog, dsk, norm_w.reshape(1, -1), tri3)


def _mix_mlp_kernel(h_ref, y1_ref, y2_ref, wo_ref, nw_ref, wup_ref, wdn_ref, nf_ref, o_ref,
                    *, ff_chunk, final):
    half = y1_ref.shape[-1]
    mix = _dot(y1_ref[...], wo_ref[0:half, :]) + _dot(y2_ref[...], wo_ref[half:2 * half, :])
    h1 = h_ref[...] + mix
    u = _rms(h1, nw_ref[...]).astype(BF16)
    d_ff = wup_ref.shape[1]
    acc = jnp.zeros_like(h1)
    for c in range(d_ff // ff_chunk):
        act = jnp.maximum(_dot(u, wup_ref[:, c * ff_chunk:(c + 1) * ff_chunk]), 0.0)
        acc = acc + _dot((act * act).astype(BF16), wdn_ref[c * ff_chunk:(c + 1) * ff_chunk, :])
    h2 = h1 + acc
    if final:
        h2 = _rms(h2, nf_ref[...])
    o_ref[...] = h2


def _mix_mlp(h, y1, y2, w_out, nw, w_up, w_down, layer, nf, tm, final):
    B, L, D = h.shape
    T = B * L
    d_ff = w_up.shape[2]
    half = y1.shape[-1]
    row = lambda width: pl.BlockSpec((tm, width), lambda i: (i, 0))
    const = lambda shape: pl.BlockSpec(shape, lambda i: (0,) * len(shape))
    stacked = lambda shape: pl.BlockSpec((None,) + shape, lambda i: (layer, 0, 0))
    out = pl.pallas_call(
        functools.partial(_mix_mlp_kernel, ff_chunk=min(1024, d_ff), final=final),
        grid=(T // tm,),
        in_specs=[row(D), row(half), row(half), const((2 * half, D)), const((1, D)),
                  stacked((D, d_ff)), stacked((d_ff, D)), const((1, D))],
        out_specs=row(D),
        out_shape=jax.ShapeDtypeStruct((T, D), F32),
        compiler_params=_params(("arbitrary",)),
        name="mix_mlp",
    )(h.reshape(T, D), y1.reshape(T, half), y2.reshape(T, half), w_out.astype(BF16),
      nw.reshape(1, D), w_up, w_down, nf.reshape(1, D))
    return out.reshape(B, L, D)


GLA_QK = GLA_HEADS * GLA_DK
GLA_V = GLA_HEADS * GLA_DV
CD_COLS = 3 * ATT_W + 2 * GLA_QK + 2 * GLA_V + LANES


def _inproj_cd_kernel(x_ref, nw_ref, w_ref, sq_ref, sk_ref, sv_ref, gq_ref, gk_ref, gv_ref,
                      gr_ref, glow_ref):
    u = _rms(x_ref[...], nw_ref[...]).astype(BF16)

    def mm(a, b):
        return _dot(u, w_ref[:, a:b])

    o = 0
    sq_ref[...] = (mm(o, o + ATT_W) * (LOG2E * HEAD_DIM ** -0.5)).astype(BF16); o += ATT_W
    sk_ref[...] = mm(o, o + ATT_W).astype(BF16); o += ATT_W
    sv_ref[...] = mm(o, o + ATT_W).astype(BF16); o += ATT_W
    gq_ref[...] = mm(o, o + GLA_QK) * (GLA_DK ** -0.5); o += GLA_QK
    gk_ref[...] = mm(o, o + GLA_QK); o += GLA_QK
    gv_ref[...] = mm(o, o + GLA_V); o += GLA_V
    gr_ref[...] = mm(o, o + GLA_V); o += GLA_V
    glow_ref[...] = mm(o, o + LANES)


def _inproj_cd(h, nw, w_in, tm):
    B, L, D = h.shape
    T = B * L
    sq, sk, sv, gq, gk, gv, glow, gr = jnp.split(w_in, np.cumsum(
        [ATT_W, ATT_W, ATT_W, GLA_QK, GLA_QK, GLA_V, GLA_RANK]).tolist(), axis=1)
    pad = jnp.zeros((D, LANES - GLA_RANK), w_in.dtype)
    w = jnp.concatenate([sq, sk, sv, gq, gk, gv, gr, glow, pad], axis=1).astype(BF16)
    row = lambda width: pl.BlockSpec((tm, width), lambda i: (i, 0))
    const = lambda shape: pl.BlockSpec(shape, lambda i: (0,) * len(shape))
    widths = [ATT_W, ATT_W, ATT_W, GLA_QK, GLA_QK, GLA_V, GLA_V, LANES]
    dtypes = [BF16, BF16, BF16, F32, F32, F32, F32, F32]
    outs = pl.pallas_call(
        _inproj_cd_kernel,
        grid=(T // tm,),
        in_specs=[row(D), const((1, D)), const((D, CD_COLS))],
        out_specs=[row(wd) for wd in widths],
        out_shape=[jax.ShapeDtypeStruct((T, wd), dt) for wd, dt in zip(widths, dtypes)],
        compiler_params=_params(("arbitrary",)),
        name="inproj_cd",
    )(h.reshape(T, D), nw.reshape(1, D), w)
    return [o.reshape(B, L, -1) for o in outs]


SB_DEAD = EXP2_DEAD - 1.0
SB_STRIP = 64
SB_CHAINS = 2


def _sb_kernel(q_ref, k_ref, v_ref, u2_ref, o_ref,
               qs_ref, z_ref, sp_ref, in_ref, a_ref, c_ref, acc_ref, *, tq):
    i = pl.program_id(2)
    tk = tq
    R = 2 * tq
    chains = range(SB_CHAINS)
    for ch in chains:
        qf = q_ref[0, :, ch * LANES:(ch + 1) * LANES].astype(F32)
        for e in range(2):
            qs_ref[ch, e * tq:(e + 1) * tq, :] = jnp.where(_head_lane_mask(e, (tq, LANES)), qf,
                                                           0.0).astype(BF16)
    c_ref[...] = jnp.zeros(c_ref.shape, F32)
    acc_ref[...] = jnp.zeros(acc_ref.shape, F32)

    def strict_mask(r0, shape):
        rows = lax.broadcasted_iota(jnp.int32, shape, 0) + (r0 % tq)
        return lax.broadcasted_iota(jnp.int32, shape, 1) < rows

    def step(j, masked):
        ks = pl.multiple_of(j * tk, tk)
        for ch in chains:
            z_ref[ch] = _dot_nt(qs_ref[ch], k_ref[0, pl.ds(ks, tk), ch * LANES:(ch + 1) * LANES])
        for ch in chains:
            for r0 in range(0, R, SB_STRIP):
                rs = slice(r0, r0 + SB_STRIP)
                z = z_ref[ch, rs, :]
                w = jnp.log2(1.0 + jnp.exp2(-jnp.abs(z)))
                sp = jnp.maximum(z, 0.0) + w
                if masked:
                    sp = jnp.where(strict_mask(r0, sp.shape), sp, 0.0)
                sp_ref[ch, rs, :] = sp.astype(BF16)
                z_ref[ch, rs, :] = jnp.minimum(z, 0.0) - w
        for ch in chains:
            in_ref[ch] = _dot(sp_ref[ch], u2_ref[...])
        for ch in chains:
            for r0 in range(0, R, SB_STRIP):
                rs = slice(r0, r0 + SB_STRIP)
                inc = in_ref[ch, rs, :]
                a = jnp.exp2(z_ref[ch, rs, :] - c_ref[ch, rs, :] - inc)
                if masked:
                    a = jnp.where(strict_mask(r0, a.shape), a, 0.0)
                a_ref[ch, rs, :] = a.astype(BF16)
                c_ref[ch, rs, :] += inc[:, 0:1] + sp_ref[ch, rs, 0:1].astype(F32)
        for ch in chains:
            acc_ref[ch] += _dot(a_ref[ch], v_ref[0, pl.ds(ks, tk), ch * LANES:(ch + 1) * LANES])
        return jnp.min(c_ref[...])

    cmin = step(i, True)

    def cond(carry):
        j, cmin = carry
        return (j >= 0) & (cmin < -SB_DEAD)

    def body(carry):
        j, _ = carry
        return j - 1, step(j, False)

    lax.while_loop(cond, body, (i - 1, cmin))
    for ch in chains:
        acc = acc_ref[ch]
        o_ref[0, :, ch * LANES:(ch + 1) * LANES] = jnp.where(
            _head_lane_mask(0, (tq, LANES)), acc[:tq], acc[tq:]).astype(BF16)


def _sb_attention(q, k, v, tq):
    B, L, _ = q.shape
    n_groups = N_ATT_HEADS // (2 * SB_CHAINS)
    R = 2 * tq
    W = SB_CHAINS * LANES
    r = np.arange(tq)
    u = (r[:, None] > r[None, :]).astype(np.float32)
    u2 = jnp.asarray(u, BF16)
    qspec = pl.BlockSpec((1, tq, W), lambda b, p, i: (b, i, p))
    kspec = pl.BlockSpec((1, L, W), lambda b, p, i: (b, 0, p))
    return pl.pallas_call(
        functools.partial(_sb_kernel, tq=tq),
        grid=(B, n_groups, L // tq),
        in_specs=[qspec, kspec, kspec, pl.BlockSpec((tq, tq), lambda b, p, i: (0, 0))],
        out_specs=qspec,
        out_shape=jax.ShapeDtypeStruct((B, L, ATT_W), BF16),
        scratch_shapes=[pltpu.VMEM((SB_CHAINS, R, LANES), BF16), pltpu.VMEM((SB_CHAINS, R, tq), F32),
                        pltpu.VMEM((SB_CHAINS, R, tq), BF16),
                        pltpu.VMEM((SB_CHAINS, R, tq), F32),
                        pltpu.VMEM((SB_CHAINS, R, tq), BF16), pltpu.VMEM((SB_CHAINS, R, 1), F32),
                        pltpu.VMEM((SB_CHAINS, R, LANES), F32)],
        compiler_params=_params(("arbitrary", "arbitrary", "arbitrary")),
        name="sb_attention",
    )(q, k, v, u2)


def _gla_tables():
    Q = CHUNK
    r = np.arange(Q)
    j = np.arange(Q)
    coef = [(j[None, :] <= r[:, None]), (j[None, :] > r[:, None])]
    masks = [np.eye(Q, dtype=bool)]
    for lvl in range(GLA_LEVELS):
        m = 1 << lvl
        c0 = (r // (2 * m)) * (2 * m)
        mid = c0 + m - 1
        upper = (r - c0) >= m
        up = (j[None, :] > mid[:, None]) & (j[None, :] <= r[:, None])
        lowr = (j[None, :] > r[:, None]) & (j[None, :] <= mid[:, None])
        coef.append(np.where(upper[:, None], up, lowr))
        masks.append((c0[:, None] == c0[None, :]) & upper[:, None] & (~upper)[None, :])
    coef = np.concatenate(coef, axis=0).astype(np.float32)
    coef2 = np.concatenate([coef, coef], axis=1)
    masks = np.stack(masks).astype(np.float32)
    hv = np.arange(GLA_V) // GLA_DV
    hk = np.arange(GLA_QK) // GLA_DK
    bdiag = (hv[:, None] == hk[None, :]).astype(np.float32)
    return jnp.asarray(coef2, BF16), jnp.asarray(masks, F32), jnp.asarray(bdiag, F32)


def _gla_kernel(gq_ref, gk_ref, gv_ref, glow_ref, gr_ref, w2_ref, gb_ref, coef_ref, mask_ref,
                bdiag_ref, nw_ref, o_ref, st_ref):
    @pl.when(pl.program_id(0) == 0)
    def _():
        st_ref[...] = jnp.zeros_like(st_ref)

    for b in range(gq_ref.shape[0]):
        _gla_chunk(gq_ref.at[b], gk_ref.at[b], gv_ref.at[b], glow_ref.at[b], gr_ref.at[b], w2_ref,
                   gb_ref, coef_ref, mask_ref, bdiag_ref, nw_ref, o_ref.at[b], st_ref.at[b])


def _gla_chunk(gq_ref, gk_ref, gv_ref, glow_ref, gr_ref, w2_ref, gb_ref, coef_ref, mask_ref,
               bdiag_ref, nw_ref, o_ref, st_ref):
    Q = CHUNK
    logits = _dot(glow_ref[...].astype(BF16), w2_ref[...]) + gb_ref[...]
    la = _log_sigmoid(logits) * (1.0 / GLA_GATE_NORM)
    hi, lo = _split2(la)
    expo = _dot(coef_ref[...], jnp.concatenate([hi, lo], axis=0))
    q = gq_ref[...]
    k = gk_ref[...]
    v = gv_ref[...]
    lane_head = lax.broadcasted_iota(jnp.int32, (Q, GLA_QK), 1) // GLA_DK
    row = lax.broadcasted_iota(jnp.int32, (Q, GLA_QK), 0)
    hmask = [lane_head == h for h in range(GLA_HEADS)]

    att = [None] * GLA_HEADS
    for lvl in range(-1, GLA_LEVELS):
        if lvl < 0:
            xq, xk = q, k.astype(BF16)
        else:
            m = 1 << lvl
            upper = (row & (2 * m - 1)) >= m
            xq = jnp.where(upper, q, k) * jnp.exp(expo[(2 + lvl) * Q:(3 + lvl) * Q, :])
            xk = xq.astype(BF16)
        msk = mask_ref[lvl + 1]
        lhs = jnp.concatenate([jnp.where(hmask[h], xq, 0.0) for h in range(GLA_HEADS)], axis=0)
        prod = _dot_nt(lhs.astype(BF16), xk)
        for h in range(GLA_HEADS):
            part = prod[h * Q:(h + 1) * Q] * msk
            att[h] = part if att[h] is None else att[h] + part

    st = st_ref[...]
    q_in = (q * jnp.exp(expo[0:Q, :])).astype(BF16)
    o = _dot_nt(q_in, st.astype(BF16))
    o_intra = [_dot(att[h].astype(BF16), v[:, h * GLA_DV:(h + 1) * GLA_DV].astype(BF16))
               for h in range(GLA_HEADS)]
    o = o + jnp.concatenate(o_intra, axis=1)

    k_dec = (k * jnp.exp(expo[Q:2 * Q, :])).astype(BF16)
    upd = _dot(v.T.astype(BF16), k_dec)
    g_last = expo[Q - 1:Q, :]
    st_ref[...] = st * jnp.exp(g_last) + upd * bdiag_ref[...]

    gr = gr_ref[...]
    o = jnp.concatenate([_rms(o[:, h * GLA_DV:(h + 1) * GLA_DV], nw_ref[...])
                         for h in range(GLA_HEADS)], axis=1)
    o_ref[...] = (o * _silu(gr)).astype(BF16)


def _gla(gq, gk, gv, glow, gr, gate_w2, gate_b, norm_w):
    B, L, _ = gq.shape
    Q = CHUNK
    coef2, masks, bdiag = _gla_tables()
    w2 = jnp.zeros((LANES, GLA_QK), F32).at[:GLA_RANK].set(gate_w2).astype(BF16)
    row = lambda width: pl.BlockSpec((B, Q, width), lambda c: (0, c, 0))
    const = lambda shape: pl.BlockSpec(shape, lambda c: (0,) * len(shape))
    return pl.pallas_call(
        _gla_kernel,
        grid=(L // Q,),
        in_specs=[row(GLA_QK), row(GLA_QK), row(GLA_V), row(LANES), row(GLA_V),
                  const((LANES, GLA_QK)), const((1, GLA_QK)), const(coef2.shape),
                  const(masks.shape), const(bdiag.shape), const((1, GLA_DV))],
        out_specs=row(GLA_V),
        out_shape=jax.ShapeDtypeStruct((B, L, GLA_V), BF16),
        scratch_shapes=[pltpu.VMEM((B, GLA_V, GLA_QK), F32)],
        compiler_params=_params(("arbitrary",)),
        name="gla_scan",
    )(gq, gk, gv, glow, gr, w2, gate_b.reshape(1, -1), coef2, masks, bdiag,
      norm_w.reshape(1, -1))


def _block(n, want):
    return want if n % want == 0 else n


def kernel(x, norm_mix, norm_mlp, norm_final, w_in_ab, fox_f_bias, ssd_conv_w, ssd_conv_b,
           ssd_dt_bias, ssd_a_log, ssd_d, ssd_norm, w_out_ab, w_in_cd, gla_gate_w2,
           gla_gate_b, gla_norm, w_out_cd, w_mlp_up, w_mlp_down):
    B, L, D = x.shape
    assert L % CHUNK == 0
    tm = _block(L, 512)
    tq = _block(L, 256)
    tk_fox = _block(L // 2, 512)

    assert tm == tk_fox
    head_order = jnp.argsort(fox_f_bias[0])
    head_cols = (head_order[:, None] * HEAD_DIM + jnp.arange(HEAD_DIM)[None, :]).reshape(-1)
    qd, kd, vd, qa, ka, z, xbc, dt, fend = _inproj_ab(x, norm_mix[0], w_in_ab[0], fox_f_bias[0],
                                                      ssd_dt_bias[0], head_order, head_cols, tm)
    w_out0 = jnp.concatenate([jnp.take(w_out_ab[0, :ATT_W], head_cols, axis=0),
                              w_out_ab[0, ATT_W:]], axis=0)
    y_fox = _fox_attention(qd, qa, kd, ka, vd, fend.reshape(B, L // tm, LANES), tk_fox)
    y_ssd = _ssd(xbc, z, dt, ssd_conv_w[0], ssd_conv_b[0], ssd_a_log[0], ssd_d[0], ssd_norm[0])
    w_up = w_mlp_up.astype(BF16)
    w_down = w_mlp_down.astype(BF16)
    h = _mix_mlp(x, y_fox, y_ssd, w_out0, norm_mlp[0], w_up, w_down, 0, norm_final, tm,
                 final=False)

    sq, sk, sv, gq, gk, gv, gr, glow = _inproj_cd(h, norm_mix[1], w_in_cd[0], tm)
    y_sb = _sb_attention(sq, sk, sv, tq)
    y_gla = _gla(gq, gk, gv, glow, gr, gla_gate_w2[0], gla_gate_b[0], gla_norm[0])
    return _mix_mlp(h, y_sb, y_gla, w_out_cd[0], norm_mlp[1], w_up, w_down, 1, norm_final, tm,
                    final=True)
```

```python
import functools
import math

import numpy as np
import jax
import jax.numpy as jnp
from jax import lax
from jax.experimental import pallas as pl
from jax.experimental.pallas import tpu as pltpu

F32 = jnp.float32
BF16 = jnp.bfloat16

LANES = 128
SUBLANES = 8
VMEM_LIMIT_BYTES = 56 * 1024 * 1024

HEAD_DIM = 64
N_ATT_HEADS = 8
ATT_W = N_ATT_HEADS * HEAD_DIM
SSD_HEADS = 8
SSD_W = 512
SSD_GROUPS = 2
SSD_STATE = 64
SSD_CONV = 4
SSD_CONV_DIM = SSD_W + 2 * SSD_GROUPS * SSD_STATE
GLA_HEADS = 4
GLA_DK = 64
GLA_DV = 128
GLA_RANK = 16
GLA_GATE_NORM = 16.0
EPS = 1e-5
CHUNK = 128
GLA_LEVELS = 7
LOG2E = math.log2(math.e)
DT_LANE0 = 8


def _params(sem):
    return pltpu.CompilerParams(dimension_semantics=sem, vmem_limit_bytes=VMEM_LIMIT_BYTES)


def _split3(x):
    hi = x.astype(BF16)
    r = x - hi.astype(F32)
    mid = r.astype(BF16)
    lo = (r - mid.astype(F32)).astype(BF16)
    return hi, mid, lo


def _split2(x):
    hi = x.astype(BF16)
    lo = (x - hi.astype(F32)).astype(BF16)
    return hi, lo


def _softplus(x):
    return jnp.maximum(x, 0.0) + jnp.log1p(jnp.exp(-jnp.abs(x)))


def _log_sigmoid(x):
    return jnp.minimum(x, 0.0) - jnp.log1p(jnp.exp(-jnp.abs(x)))


def _silu(x):
    h = 0.5 * x
    return h + h * jnp.tanh(h)


def _rms(x, w):
    ms = jnp.mean(x * x, axis=-1, keepdims=True)
    return x * lax.rsqrt(ms + EPS) * w


def _dot(a, b):
    return jnp.dot(a, b, preferred_element_type=F32)


def _dot_nt(a, b):
    return lax.dot_general(a, b, (((1,), (1,)), ((), ())), preferred_element_type=F32)


AB_COLS = 3 * ATT_W + SSD_W + SSD_CONV_DIM + LANES


def _inproj_ab_kernel(x_ref, nw_ref, w_ref, sb_ref, tri3_ref, sel_ref, cst_ref,
                      qd_ref, kd_ref, vd_ref, qa_ref, ka_ref, z_ref, xbc_ref, dt_ref, fend_ref,
                      carry_ref):
    @pl.when(pl.program_id(1) == 0)
    def _():
        carry_ref[...] = jnp.zeros_like(carry_ref)

    u = _rms(x_ref[0], nw_ref[...]).astype(BF16)

    def mm(a, b):
        return _dot(u, w_ref[:, a:b])

    o = 0
    qd_ref[0] = (mm(o, o + ATT_W) * (LOG2E * HEAD_DIM ** -0.5)).astype(BF16); o += ATT_W
    kd_ref[0] = mm(o, o + ATT_W).astype(BF16); o += ATT_W
    vd_ref[0] = mm(o, o + ATT_W).astype(BF16); o += ATT_W
    z_ref[0] = mm(o, o + SSD_W); o += SSD_W
    xbc_ref[0] = mm(o, o + SSD_CONV_DIM); o += SSD_CONV_DIM
    small = mm(o, o + LANES) + sb_ref[...]

    lane = lax.broadcasted_iota(jnp.int32, small.shape, 1)
    is_f = lane < N_ATT_HEADS
    is_dt = (lane >= DT_LANE0) & (lane < DT_LANE0 + SSD_HEADS)
    log_f = jnp.where(is_f, _log_sigmoid(small), 0.0)
    dt_ref[0] = jnp.where(is_dt, _softplus(small), 0.0)

    hi, mid, lo = _split3(log_f)
    tm = log_f.shape[0]
    carry = carry_ref[...]
    cums = []
    for r0 in range(0, tm, CHUNK):
        rs = slice(r0, r0 + CHUNK)
        piece = _dot(tri3_ref[...], jnp.concatenate([hi[rs], mid[rs], lo[rs]], axis=0)) + carry
        carry = piece[CHUNK - 1:CHUNK, :]
        cums.append(piece)
    cum = jnp.concatenate(cums, axis=0)
    carry_ref[...] = carry

    cum2 = cum * LOG2E
    fend_ref[0, 0] = cum2[tm - 1:tm, :]
    fh, fm, fl = _split3(cum2)
    fcat = (fh.astype(F32) + pltpu.roll(fm.astype(F32), N_ATT_HEADS, 1)
            + pltpu.roll(fl.astype(F32), 2 * N_ATT_HEADS, 1)).astype(BF16)
    aug = _dot(fcat, sel_ref[...]) + cst_ref[...]
    qa_ref[0] = aug[:, :ATT_W].astype(BF16)
    ka_ref[0] = aug[:, ATT_W:].astype(BF16)


def _aug_tables():
    sel = np.zeros((LANES, 2 * ATT_W), np.float32)
    cst = np.zeros((1, 2 * ATT_W), np.float32)
    for h in range(N_ATT_HEADS):
        p, e = divmod(h, 2)
        for part in range(3):
            src = part * N_ATT_HEADS + h
            sel[src, p * LANES + 6 * e + part] = 1.0
            sel[src, ATT_W + p * LANES + 6 * e + 3 + part] = -1.0
            cst[0, p * LANES + 6 * e + 3 + part] = 1.0
            cst[0, ATT_W + p * LANES + 6 * e + part] = 1.0
    return jnp.asarray(sel, BF16), jnp.asarray(cst, F32)


def _tri_incl(n):
    r = np.arange(n)
    return (r[None, :] <= r[:, None]).astype(np.float32)


def _inproj_ab(h, nw, w_in, f_bias, dt_bias, head_order, head_cols, tm):
    B, L, D = h.shape
    fq, fk, fv, fl, wz, wxbc, wdt = jnp.split(w_in, np.cumsum(
        [ATT_W, ATT_W, ATT_W, N_ATT_HEADS, SSD_W, SSD_CONV_DIM])[:].tolist(), axis=1)
    fq, fk, fv = (jnp.take(w, head_cols, axis=1) for w in (fq, fk, fv))
    fl = jnp.take(fl, head_order, axis=1)
    f_bias = f_bias[head_order]
    pad = jnp.zeros((D, LANES - N_ATT_HEADS - SSD_HEADS), w_in.dtype)
    w = jnp.concatenate([fq, fk, fv, wz, wxbc, fl, wdt, pad], axis=1).astype(BF16)
    sb = jnp.concatenate([f_bias, dt_bias, jnp.zeros((LANES - 16,), F32)]).reshape(1, LANES)
    assert tm % CHUNK == 0
    tri = _tri_incl(CHUNK)
    tri3 = jnp.asarray(np.concatenate([tri, tri, tri], axis=1), BF16)
    sel, cst = _aug_tables()

    row = lambda width: pl.BlockSpec((1, tm, width), lambda b, i: (b, i, 0))
    const = lambda shape: pl.BlockSpec(shape, lambda b, i: (0,) * len(shape))
    outs = [
        jax.ShapeDtypeStruct((B, L, ATT_W), BF16),
        jax.ShapeDtypeStruct((B, L, ATT_W), BF16),
        jax.ShapeDtypeStruct((B, L, ATT_W), BF16),
        jax.ShapeDtypeStruct((B, L, ATT_W), BF16),
        jax.ShapeDtypeStruct((B, L, ATT_W), BF16),
        jax.ShapeDtypeStruct((B, L, SSD_W), F32),
        jax.ShapeDtypeStruct((B, L, SSD_CONV_DIM), F32),
        jax.ShapeDtypeStruct((B, L, LANES), F32),
        jax.ShapeDtypeStruct((B, L // tm, 1, LANES), F32),
    ]
    return pl.pallas_call(
        _inproj_ab_kernel,
        grid=(B, L // tm),
        in_specs=[row(D), const((1, D)), const((D, AB_COLS)), const((1, LANES)),
                  const((CHUNK, 3 * CHUNK)), const((LANES, 2 * ATT_W)), const((1, 2 * ATT_W))],
        out_specs=[row(ATT_W)] * 5 + [row(SSD_W), row(SSD_CONV_DIM), row(LANES),
                   pl.BlockSpec((1, 1, 1, LANES), lambda b, i: (b, i, 0, 0))],
        out_shape=outs,
        scratch_shapes=[pltpu.VMEM((1, LANES), F32)],
        compiler_params=_params(("arbitrary", "arbitrary")),
        name="inproj_ab",
    )(h, nw.reshape(1, D), w, sb, tri3, sel, cst)


def _head_lane_mask(e, shape):
    lane = lax.broadcasted_iota(jnp.int32, shape, 1)
    return (lane < HEAD_DIM) if e == 0 else (lane >= HEAD_DIM)


EXP2_DEAD = -151.0
FOX_STRIP = 64


def _fox_kernel(qd_ref, qa_ref, kd_ref, ka_ref, v_ref, fend_ref, o_ref,
                q_ref, s0_ref, s1_ref, p0_ref, p1_ref, m_ref, al_ref, acc_ref, kn_ref, *, tk):
    it = pl.program_id(2)
    ka = 2 * it
    R = 4 * tk
    lower = slice(2 * tk, R)
    everything = slice(0, R)
    qd = qd_ref[0].astype(F32)
    qa = qa_ref[0].astype(F32)
    lane = lax.broadcasted_iota(jnp.int32, (tk, LANES), 1)
    first = lane < HEAD_DIM
    for half in range(2):
        hs = slice(half * tk, (half + 1) * tk)
        for e in range(2):
            amask = (lane >= 6 * e) & (lane < 6 * e + 6)
            g = 2 * half + e
            q_ref[g * tk:(g + 1) * tk, :] = jnp.concatenate(
                [jnp.where(first if e == 0 else ~first, qd[hs], 0.0),
                 jnp.where(amask, qa[hs], 0.0)], axis=1).astype(BF16)
    m_ref[...] = jnp.full(m_ref.shape, -jnp.inf, F32)
    acc_ref[...] = jnp.zeros(acc_ref.shape, F32)
    ones = jnp.ones((tk, LANES), BF16)

    def head_sq_norms(x):
        lane_x = lax.broadcasted_iota(jnp.int32, x.shape, 1)
        sq = x * x
        return (jnp.sum(jnp.where(lane_x < HEAD_DIM, sq, 0.0), axis=1, keepdims=True),
                jnp.sum(jnp.where(lane_x < HEAD_DIM, 0.0, sq), axis=1, keepdims=True))

    @pl.when(it == 0)
    def _():
        def kbody(c, carry):
            ks = pl.multiple_of(c * tk, tk)
            n0, n1 = head_sq_norms(kd_ref[0, pl.ds(ks, tk), :].astype(F32))
            return jnp.maximum(carry[0], n0), jnp.maximum(carry[1], n1)
        zero = jnp.zeros((tk, 1), F32)
        n0, n1 = lax.fori_loop(0, kd_ref.shape[1] // tk, kbody, (zero, zero))
        for e, n in enumerate((n0, n1)):
            kn_ref[e] = jnp.broadcast_to(jnp.sqrt(jnp.max(n, axis=0, keepdims=True)), kn_ref.shape[1:])

    def scores(j, s_ref, rows=everything):
        ks = pl.multiple_of(j * tk, tk)
        k = jnp.concatenate([kd_ref[0, pl.ds(ks, tk), :], ka_ref[0, pl.ds(ks, tk), :]], axis=1)
        s_ref[rows, :] = _dot_nt(q_ref[rows, :], k)

    def update(j, s_ref, p_ref, rows=everything, diag_rows=0):
        for r0 in range(rows.start, rows.stop, FOX_STRIP):
            rs = slice(r0, r0 + FOX_STRIP)
            s = s_ref[rs, :]
            if r0 < diag_rows:
                rws = lax.broadcasted_iota(jnp.int32, s.shape, 0) + (r0 % tk)
                cols = lax.broadcasted_iota(jnp.int32, s.shape, 1)
                s = jnp.where(rws >= cols, s, -jnp.inf)
            m_old = m_ref[rs, :]
            m_new = jnp.maximum(m_old, jnp.max(s, axis=1, keepdims=True))
            al_ref[rs, :] = jnp.exp2(m_old - m_new)
            m_ref[rs, :] = m_new
            p_ref[rs, :] = jnp.exp2(s - m_new).astype(BF16)
        ks = pl.multiple_of(j * tk, tk)
        v1 = jnp.concatenate([v_ref[0, pl.ds(ks, tk), :], ones], axis=1)
        acc_ref[rows, :] = al_ref[rows, :] * acc_ref[rows, :] + _dot(p_ref[rows, :], v1)

    scores(ka + 1, s0_ref, lower)
    scores(ka, s1_ref)
    update(ka + 1, s0_ref, p0_ref, rows=lower, diag_rows=R)
    scores(jnp.maximum(ka - 1, 0), s0_ref)
    update(ka, s1_ref, p1_ref, diag_rows=2 * tk)

    qn = head_sq_norms(qd)
    fend = fend_ref[0]
    hl = lax.broadcasted_iota(jnp.int32, fend.shape, 1)
    jcol = lax.broadcasted_iota(jnp.int32, (fend.shape[0], 1), 0)
    jmin = ka
    for e in range(2):
        head = 2 * pl.program_id(1) + e
        fe = jnp.sum(jnp.where(hl == head, fend, 0.0), axis=1, keepdims=True)
        f_top = jnp.sum(jnp.where(jcol == ka - 1, fe, 0.0), axis=0, keepdims=True)
        qk = jnp.sqrt(jnp.max(qn[e], axis=0, keepdims=True)) * kn_ref[e, 0:1, 0:1]
        m_min = jnp.minimum(jnp.min(m_ref[e * tk:(e + 1) * tk, :], axis=0, keepdims=True),
                            jnp.min(m_ref[(2 + e) * tk:(3 + e) * tk, :], axis=0, keepdims=True))
        live = (jcol < ka) & (qk + f_top - fe + 1.0 - m_min > EXP2_DEAD)
        jmin = jnp.minimum(jmin, jnp.min(jnp.where(live, jcol, ka)))
    nb = ka - jmin

    def pair(u, carry):
        j = ka - 1 - 2 * u
        scores(j - 1, s1_ref)
        update(j, s0_ref, p0_ref)
        scores(jnp.maximum(j - 2, 0), s0_ref)
        update(j - 1, s1_ref, p1_ref)
        return carry

    lax.fori_loop(0, nb // 2, pair, 0)

    @pl.when(nb % 2 == 1)
    def _():
        update(jmin, s0_ref, p0_ref)

    acc = acc_ref[...]
    out = acc[:, :LANES] / acc[:, LANES:]
    for half in range(2):
        g = 2 * half
        o_ref[0, half * tk:(half + 1) * tk, :] = jnp.where(
            first, out[g * tk:(g + 1) * tk], out[(g + 1) * tk:(g + 2) * tk]).astype(BF16)


def _fox_attention(qd, qa, kd, ka, v, fend, tk):
    B, L, _ = qd.shape
    assert fend.shape == (B, L // tk, LANES)
    n_pairs = N_ATT_HEADS // 2
    tq = 2 * tk
    R = 2 * tq
    qspec = pl.BlockSpec((1, tq, LANES), lambda b, p, i: (b, i, p))
    kspec = pl.BlockSpec((1, L, LANES), lambda b, p, i: (b, 0, p))
    return pl.pallas_call(
        functools.partial(_fox_kernel, tk=tk),
        grid=(B, n_pairs, L // tq),
        in_specs=[qspec, qspec, kspec, kspec, kspec,
                  pl.BlockSpec((1, L // tk, LANES), lambda b, p, i: (b, 0, 0))],
        out_specs=qspec,
        out_shape=jax.ShapeDtypeStruct((B, L, ATT_W), BF16),
        scratch_shapes=[pltpu.VMEM((R, 2 * LANES), BF16),
                        pltpu.VMEM((R, tk), F32), pltpu.VMEM((R, tk), F32),
                        pltpu.VMEM((R, tk), BF16), pltpu.VMEM((R, tk), BF16),
                        pltpu.VMEM((R, 1), F32), pltpu.VMEM((R, 1), F32),
                        pltpu.VMEM((R, 2 * LANES), F32),
                        pltpu.VMEM((2, SUBLANES, LANES), F32)],
        compiler_params=_params(("arbitrary", "arbitrary", "arbitrary")),
        name="fox_attention",
    )(qd, qa, kd, ka, v, fend)


def _ssd_kernel(xbc_ref, z_ref, dt_ref, cw_ref, cb_ref, alog_ref, dsk_ref, nw_ref, tri3_ref,
                y_ref, xext_ref, st_ref):
    @pl.when(pl.program_id(0) == 0)
    def _():
        xext_ref[:, 0:SUBLANES, :] = jnp.zeros((xext_ref.shape[0], SUBLANES, SSD_CONV_DIM), F32)
        st_ref[...] = jnp.zeros_like(st_ref)

    for b in range(xbc_ref.shape[0]):
        _ssd_chunk(xbc_ref.at[b], z_ref.at[b], dt_ref.at[b], cw_ref, cb_ref, alog_ref, dsk_ref,
                   nw_ref, tri3_ref, y_ref.at[b], xext_ref.at[b], st_ref.at[b])


def _ssd_chunk(xbc_ref, z_ref, dt_ref, cw_ref, cb_ref, alog_ref, dsk_ref, nw_ref, tri3_ref,
               y_ref, xext_ref, st_ref):
    Q = CHUNK
    xext_ref[SUBLANES:SUBLANES + Q, :] = xbc_ref[...]
    conv = cb_ref[...]
    for k in range(SSD_CONV):
        conv = conv + cw_ref[k:k + 1, :] * xext_ref[pl.ds(SUBLANES - (SSD_CONV - 1) + k, Q), :]
    xext_ref[0:SUBLANES, :] = xext_ref[Q:Q + SUBLANES, :]
    xc = _silu(conv)
    xs = xc[:, :SSD_W]
    bm = xc[:, SSD_W:SSD_W + LANES]
    cm = xc[:, SSD_W + LANES:]

    dt = dt_ref[...]
    a = dt * (-jnp.exp(alog_ref[...]))
    hi, mid, lo = _split3(a)
    a_cum = _dot(tri3_ref[...], jnp.concatenate([hi, mid, lo], axis=0))
    a_row = a_cum.T
    a_last_col = a_cum[Q - 1:Q, :]
    bm_t = bm.T

    rows = lax.broadcasted_iota(jnp.int32, (Q, Q), 0)
    cols = lax.broadcasted_iota(jnp.int32, (Q, Q), 1)
    causal = rows >= cols
    lane = lax.broadcasted_iota(jnp.int32, (Q, LANES), 1)
    first_half = lane < HEAD_DIM

    scores = []
    cmask = []
    for g in range(SSD_GROUPS):
        cg = jnp.where(_head_lane_mask(g, (Q, LANES)), cm, 0.0).astype(BF16)
        cmask.append(cg)
        scores.append(_dot_nt(cg, bm.astype(BF16)))

    y_pairs = []
    heads_per_group = SSD_HEADS // SSD_GROUPS
    for p in range(SSD_HEADS // 2):
        xs_pair = xs[:, p * LANES:(p + 1) * LANES]
        dt_pair = jnp.where(first_half, dt[:, DT_LANE0 + 2 * p:DT_LANE0 + 2 * p + 1],
                            dt[:, DT_LANE0 + 2 * p + 1:DT_LANE0 + 2 * p + 2])
        xdt = (xs_pair * dt_pair).astype(BF16)
        y_head = []
        for e in range(2):
            h = 2 * p + e
            g = h // heads_per_group
            hl = DT_LANE0 + h
            acol = a_cum[:, hl:hl + 1]
            arow = a_row[hl:hl + 1, :]
            alast = a_last_col[:, hl:hl + 1]
            lmat = jnp.exp(jnp.where(causal, acol - arow, -jnp.inf))
            y_diag = _dot((scores[g] * lmat).astype(BF16), xdt)
            prev = st_ref[h]
            y_off = _dot(cmask[g], prev.astype(BF16)) * jnp.exp(acol)
            y_head.append(y_diag + y_off)
            decay_row = jnp.exp(alast - arow)
            local = _dot((bm_t * decay_row).astype(BF16), xdt)
            st_ref[h] = prev * jnp.exp(alast) + local
        y_pairs.append(jnp.where(first_half, y_head[0], y_head[1]))
    y = jnp.concatenate(y_pairs, axis=1) + dsk_ref[...] * xs
    y = y * _silu(z_ref[...])
    gw = SSD_W // SSD_GROUPS
    y = jnp.concatenate([_rms(y[:, g * gw:(g + 1) * gw], nw_ref[:, g * gw:(g + 1) * gw])
                         for g in range(SSD_GROUPS)], axis=1)
    y_ref[...] = y.astype(BF16)


def _ssd(xbc, z, dt, conv_w, conv_b, a_log, d_skip, norm_w):
    B, L, _ = xbc.shape
    Q = CHUNK
    alog = jnp.zeros((1, LANES), F32).at[0, DT_LANE0:DT_LANE0 + SSD_HEADS].set(a_log)
    dsk = jnp.repeat(d_skip, SSD_W // SSD_HEADS).reshape(1, SSD_W)
    tri = _tri_incl(Q)
    tri3 = jnp.asarray(np.concatenate([tri, tri, tri], axis=1), BF16)
    row = lambda width: pl.BlockSpec((B, Q, width), lambda c: (0, c, 0))
    const = lambda shape: pl.BlockSpec(shape, lambda c: (0,) * len(shape))
    return pl.pallas_call(
        _ssd_kernel,
        grid=(L // Q,),
        in_specs=[row(SSD_CONV_DIM), row(SSD_W), row(LANES), const((SSD_CONV, SSD_CONV_DIM)),
                  const((1, SSD_CONV_DIM)), const((1, LANES)), const((1, SSD_W)),
                  const((1, SSD_W)), const((Q, 3 * Q))],
        out_specs=row(SSD_W),
        out_shape=jax.ShapeDtypeStruct((B, L, SSD_W), BF16),
        scratch_shapes=[pltpu.VMEM((B, Q + SUBLANES, SSD_CONV_DIM), F32),
                        pltpu.VMEM((B, SSD_HEADS, LANES, LANES), F32)],
        compiler_params=_params(("arbitrary",)),
        name="ssd_scan",
    )(xbc, z, dt, conv_w, conv_b.reshape(1, -1), alog, dsk, norm_w.reshape(1, -1), tri3)


def _mix_mlp_kernel(h_ref, y1_ref, y2_ref, wo_ref, nw_ref, wup_ref, wdn_ref, nf_ref, o_ref,
                    *, ff_chunk, final):
    half = y1_ref.shape[-1]
    mix = _dot(y1_ref[...], wo_ref[0:half, :]) + _dot(y2_ref[...], wo_ref[half:2 * half, :])
    h1 = h_ref[...] + mix
    u = _rms(h1, nw_ref[...]).astype(BF16)
    d_ff = wup_ref.shape[1]
    acc = jnp.zeros_like(h1)
    for c in range(d_ff // ff_chunk):
        act = jnp.maximum(_dot(u, wup_ref[:, c * ff_chunk:(c + 1) * ff_chunk]), 0.0)
        acc = acc + _dot((act * act).astype(BF16), wdn_ref[c * ff_chunk:(c + 1) * ff_chunk, :])
    h2 = h1 + acc
    if final:
        h2 = _rms(h2, nf_ref[...])
    o_ref[...] = h2


def _mix_mlp(h, y1, y2, w_out, nw, w_up, w_down, layer, nf, tm, final):
    B, L, D = h.shape
    T = B * L
    d_ff = w_up.shape[2]
    half = y1.shape[-1]
    row = lambda width: pl.BlockSpec((tm, width), lambda i: (i, 0))
    const = lambda shape: pl.BlockSpec(shape, lambda i: (0,) * len(shape))
    stacked = lambda shape: pl.BlockSpec((None,) + shape, lambda i: (layer, 0, 0))
    out = pl.pallas_call(
        functools.partial(_mix_mlp_kernel, ff_chunk=min(1024, d_ff), final=final),
        grid=(T // tm,),
        in_specs=[row(D), row(half), row(half), const((2 * half, D)), const((1, D)),
                  stacked((D, d_ff)), stacked((d_ff, D)), const((1, D))],
        out_specs=row(D),
        out_shape=jax.ShapeDtypeStruct((T, D), F32),
        compiler_params=_params(("arbitrary",)),
        name="mix_mlp",
    )(h.reshape(T, D), y1.reshape(T, half), y2.reshape(T, half), w_out.astype(BF16),
      nw.reshape(1, D), w_up, w_down, nf.reshape(1, D))
    return out.reshape(B, L, D)


GLA_QK = GLA_HEADS * GLA_DK
GLA_V = GLA_HEADS * GLA_DV
CD_COLS = 3 * ATT_W + 2 * GLA_QK + 2 * GLA_V + LANES


def _inproj_cd_kernel(x_ref, nw_ref, w_ref, sq_ref, sk_ref, sv_ref, gq_ref, gk_ref, gv_ref,
                      gr_ref, glow_ref):
    u = _rms(x_ref[...], nw_ref[...]).astype(BF16)

    def mm(a, b):
        return _dot(u, w_ref[:, a:b])

    o = 0
    sq_ref[...] = (mm(o, o + ATT_W) * (LOG2E * HEAD_DIM ** -0.5)).astype(BF16); o += ATT_W
    sk_ref[...] = mm(o, o + ATT_W).astype(BF16); o += ATT_W
    sv_ref[...] = mm(o, o + ATT_W).astype(BF16); o += ATT_W
    gq_ref[...] = mm(o, o + GLA_QK) * (GLA_DK ** -0.5); o += GLA_QK
    gk_ref[...] = mm(o, o + GLA_QK); o += GLA_QK
    gv_ref[...] = mm(o, o + GLA_V); o += GLA_V
    gr_ref[...] = mm(o, o + GLA_V); o += GLA_V
    glow_ref[...] = mm(o, o + LANES)


def _inproj_cd(h, nw, w_in, tm):
    B, L, D = h.shape
    T = B * L
    sq, sk, sv, gq, gk, gv, glow, gr = jnp.split(w_in, np.cumsum(
        [ATT_W, ATT_W, ATT_W, GLA_QK, GLA_QK, GLA_V, GLA_RANK]).tolist(), axis=1)
    pad = jnp.zeros((D, LANES - GLA_RANK), w_in.dtype)
    w = jnp.concatenate([sq, sk, sv, gq, gk, gv, gr, glow, pad], axis=1).astype(BF16)
    row = lambda width: pl.BlockSpec((tm, width), lambda i: (i, 0))
    const = lambda shape: pl.BlockSpec(shape, lambda i: (0,) * len(shape))
    widths = [ATT_W, ATT_W, ATT_W, GLA_QK, GLA_QK, GLA_V, GLA_V, LANES]
    dtypes = [BF16, BF16, BF16, F32, F32, F32, F32, F32]
    outs = pl.pallas_call(
        _inproj_cd_kernel,
        grid=(T // tm,),
        in_specs=[row(D), const((1, D)), const((D, CD_COLS))],
        out_specs=[row(wd) for wd in widths],
        out_shape=[jax.ShapeDtypeStruct((T, wd), dt) for wd, dt in zip(widths, dtypes)],
        compiler_params=_params(("arbitrary",)),
        name="inproj_cd",
    )(h.reshape(T, D), nw.reshape(1, D), w)
    return [o.reshape(B, L, -1) for o in outs]


SB_DEAD = EXP2_DEAD - 1.0
SB_STRIP = 64
SB_CHAINS = 4


def _sb_kernel(q_ref, k_ref, v_ref, u2_ref, o_ref,
               qs_ref, z_ref, sp_ref, in_ref, a_ref, c_ref, acc_ref, *, tq):
    i = pl.program_id(2)
    tk = tq
    R = 2 * tq
    chains = range(SB_CHAINS)
    for ch in chains:
        qf = q_ref[0, :, ch * LANES:(ch + 1) * LANES].astype(F32)
        for e in range(2):
            qs_ref[ch, e * tq:(e + 1) * tq, :] = jnp.where(_head_lane_mask(e, (tq, LANES)), qf,
                                                           0.0).astype(BF16)
    c_ref[...] = jnp.zeros(c_ref.shape, F32)
    acc_ref[...] = jnp.zeros(acc_ref.shape, F32)

    def strict_mask(r0, shape):
        rows = lax.broadcasted_iota(jnp.int32, shape, 0) + (r0 % tq)
        return lax.broadcasted_iota(jnp.int32, shape, 1) < rows

    def step(j, masked):
        ks = pl.multiple_of(j * tk, tk)
        for ch in chains:
            z_ref[ch] = _dot_nt(qs_ref[ch], k_ref[0, pl.ds(ks, tk), ch * LANES:(ch + 1) * LANES])
        for ch in chains:
            for r0 in range(0, R, SB_STRIP):
                rs = slice(r0, r0 + SB_STRIP)
                z = z_ref[ch, rs, :]
                w = jnp.log2(1.0 + jnp.exp2(-jnp.abs(z)))
                sp = jnp.maximum(z, 0.0) + w
                if masked:
                    sp = jnp.where(strict_mask(r0, sp.shape), sp, 0.0)
                sp_ref[ch, rs, :] = sp.astype(BF16)
                z_ref[ch, rs, :] = jnp.minimum(z, 0.0) - w
        for ch in chains:
            in_ref[ch] = _dot(sp_ref[ch], u2_ref[...])
        for ch in chains:
            for r0 in range(0, R, SB_STRIP):
                rs = slice(r0, r0 + SB_STRIP)
                inc = in_ref[ch, rs, :]
                a = jnp.exp2(z_ref[ch, rs, :] - c_ref[ch, rs, :] - inc)
                if masked:
                    a = jnp.where(strict_mask(r0, a.shape), a, 0.0)
                a_ref[ch, rs, :] = a.astype(BF16)
                c_ref[ch, rs, :] += inc[:, 0:1] + sp_ref[ch, rs, 0:1].astype(F32)
        for ch in chains:
            acc_ref[ch] += _dot(a_ref[ch], v_ref[0, pl.ds(ks, tk), ch * LANES:(ch + 1) * LANES])
        return jnp.min(c_ref[...])

    cmin = step(i, True)

    def cond(carry):
        j, cmin = carry
        return (j >= 0) & (cmin < -SB_DEAD)

    def body(carry):
        j, _ = carry
        return j - 1, step(j, False)

    lax.while_loop(cond, body, (i - 1, cmin))
    for ch in chains:
        acc = acc_ref[ch]
        o_ref[0, :, ch * LANES:(ch + 1) * LANES] = jnp.where(
            _head_lane_mask(0, (tq, LANES)), acc[:tq], acc[tq:]).astype(BF16)


def _sb_attention(q, k, v, tq):
    B, L, _ = q.shape
    n_groups = N_ATT_HEADS // (2 * SB_CHAINS)
    R = 2 * tq
    W = SB_CHAINS * LANES
    r = np.arange(tq)
    u = (r[:, None] > r[None, :]).astype(np.float32)
    u2 = jnp.asarray(u, BF16)
    qspec = pl.BlockSpec((1, tq, W), lambda b, p, i: (b, i, p))
    kspec = pl.BlockSpec((1, L, W), lambda b, p, i: (b, 0, p))
    return pl.pallas_call(
        functools.partial(_sb_kernel, tq=tq),
        grid=(B, n_groups, L // tq),
        in_specs=[qspec, kspec, kspec, pl.BlockSpec((tq, tq), lambda b, p, i: (0, 0))],
        out_specs=qspec,
        out_shape=jax.ShapeDtypeStruct((B, L, ATT_W), BF16),
        scratch_shapes=[pltpu.VMEM((SB_CHAINS, R, LANES), BF16), pltpu.VMEM((SB_CHAINS, R, tq), F32),
                        pltpu.VMEM((SB_CHAINS, R, tq), BF16),
                        pltpu.VMEM((SB_CHAINS, R, tq), F32),
                        pltpu.VMEM((SB_CHAINS, R, tq), BF16), pltpu.VMEM((SB_CHAINS, R, 1), F32),
                        pltpu.VMEM((SB_CHAINS, R, LANES), F32)],
        compiler_params=_params(("arbitrary", "arbitrary", "arbitrary")),
        name="sb_attention",
    )(q, k, v, u2)


def _gla_tables():
    Q = CHUNK
    r = np.arange(Q)
    j = np.arange(Q)
    coef = [(j[None, :] <= r[:, None]), (j[None, :] > r[:, None])]
    masks = [np.eye(Q, dtype=bool)]
    for lvl in range(GLA_LEVELS):
        m = 1 << lvl
        c0 = (r // (2 * m)) * (2 * m)
        mid = c0 + m - 1
        upper = (r - c0) >= m
        up = (j[None, :] > mid[:, None]) & (j[None, :] <= r[:, None])
        lowr = (j[None, :] > r[:, None]) & (j[None, :] <= mid[:, None])
        coef.append(np.where(upper[:, None], up, lowr))
        masks.append((c0[:, None] == c0[None, :]) & upper[:, None] & (~upper)[None, :])
    coef = np.concatenate(coef, axis=0).astype(np.float32)
    coef2 = np.concatenate([coef, coef], axis=1)
    masks = np.stack(masks).astype(np.float32)
    hv = np.arange(GLA_V) // GLA_DV
    hk = np.arange(GLA_QK) // GLA_DK
    bdiag = (hv[:, None] == hk[None, :]).astype(np.float32)
    return jnp.asarray(coef2, BF16), jnp.asarray(masks, F32), jnp.asarray(bdiag, F32)


def _gla_kernel(gq_ref, gk_ref, gv_ref, glow_ref, gr_ref, w2_ref, gb_ref, coef_ref, mask_ref,
                bdiag_ref, nw_ref, o_ref, st_ref):
    @pl.when(pl.program_id(0) == 0)
    def _():
        st_ref[...] = jnp.zeros_like(st_ref)

    for b in range(gq_ref.shape[0]):
        _gla_chunk(gq_ref.at[b], gk_ref.at[b], gv_ref.at[b], glow_ref.at[b], gr_ref.at[b], w2_ref,
                   gb_ref, coef_ref, mask_ref, bdiag_ref, nw_ref, o_ref.at[b], st_ref.at[b])


def _gla_chunk(gq_ref, gk_ref, gv_ref, glow_ref, gr_ref, w2_ref, gb_ref, coef_ref, mask_ref,
               bdiag_ref, nw_ref, o_ref, st_ref):
    Q = CHUNK
    logits = _dot(glow_ref[...].astype(BF16), w2_ref[...]) + gb_ref[...]
    la = _log_sigmoid(logits) * (1.0 / GLA_GATE_NORM)
    hi, lo = _split2(la)
    expo = _dot(coef_ref[...], jnp.concatenate([hi, lo], axis=0))
    q = gq_ref[...]
    k = gk_ref[...]
    v = gv_ref[...]
    lane_head = lax.broadcasted_iota(jnp.int32, (Q, GLA_QK), 1) // GLA_DK
    row = lax.broadcasted_iota(jnp.int32, (Q, GLA_QK), 0)
    hmask = [lane_head == h for h in range(GLA_HEADS)]

    att = [None] * GLA_HEADS
    for lvl in range(-1, GLA_LEVELS):
        if lvl < 0:
            xq, xk = q, k.astype(BF16)
        else:
            m = 1 << lvl
            upper = (row & (2 * m - 1)) >= m
            xq = jnp.where(upper, q, k) * jnp.exp(expo[(2 + lvl) * Q:(3 + lvl) * Q, :])
            xk = xq.astype(BF16)
        msk = mask_ref[lvl + 1]
        lhs = jnp.concatenate([jnp.where(hmask[h], xq, 0.0) for h in range(GLA_HEADS)], axis=0)
        prod = _dot_nt(lhs.astype(BF16), xk)
        for h in range(GLA_HEADS):
            part = prod[h * Q:(h + 1) * Q] * msk
            att[h] = part if att[h] is None else att[h] + part

    st = st_ref[...]
    q_in = (q * jnp.exp(expo[0:Q, :])).astype(BF16)
    o = _dot_nt(q_in, st.astype(BF16))
    o_intra = [_dot(att[h].astype(BF16), v[:, h * GLA_DV:(h + 1) * GLA_DV].astype(BF16))
               for h in range(GLA_HEADS)]
    o = o + jnp.concatenate(o_intra, axis=1)

    k_dec = (k * jnp.exp(expo[Q:2 * Q, :])).astype(BF16)
    upd = _dot(v.T.astype(BF16), k_dec)
    g_last = expo[Q - 1:Q, :]
    st_ref[...] = st * jnp.exp(g_last) + upd * bdiag_ref[...]

    gr = gr_ref[...]
    o = jnp.concatenate([_rms(o[:, h * GLA_DV:(h + 1) * GLA_DV], nw_ref[...])
                         for h in range(GLA_HEADS)], axis=1)
    o_ref[...] = (o * _silu(gr)).astype(BF16)


def _gla(gq, gk, gv, glow, gr, gate_w2, gate_b, norm_w):
    B, L, _ = gq.shape
    Q = CHUNK
    coef2, masks, bdiag = _gla_tables()
    w2 = jnp.zeros((LANES, GLA_QK), F32).at[:GLA_RANK].set(gate_w2).astype(BF16)
    row = lambda width: pl.BlockSpec((B, Q, width), lambda c: (0, c, 0))
    const = lambda shape: pl.BlockSpec(shape, lambda c: (0,) * len(shape))
    return pl.pallas_call(
        _gla_kernel,
        grid=(L // Q,),
        in_specs=[row(GLA_QK), row(GLA_QK), row(GLA_V), row(LANES), row(GLA_V),
                  const((LANES, GLA_QK)), const((1, GLA_QK)), const(coef2.shape),
                  const(masks.shape), const(bdiag.shape), const((1, GLA_DV))],
        out_specs=row(GLA_V),
        out_shape=jax.ShapeDtypeStruct((B, L, GLA_V), BF16),
        scratch_shapes=[pltpu.VMEM((B, GLA_V, GLA_QK), F32)],
        compiler_params=_params(("arbitrary",)),
        name="gla_scan",
    )(gq, gk, gv, glow, gr, w2, gate_b.reshape(1, -1), coef2, masks, bdiag,
      norm_w.reshape(1, -1))


def _block(n, want):
    return want if n % want == 0 else n


def kernel(x, norm_mix, norm_mlp, norm_final, w_in_ab, fox_f_bias, ssd_conv_w, ssd_conv_b,
           ssd_dt_bias, ssd_a_log, ssd_d, ssd_norm, w_out_ab, w_in_cd, gla_gate_w2,
           gla_gate_b, gla_norm, w_out_cd, w_mlp_up, w_mlp_down):
    B, L, D = x.shape
    assert L % CHUNK == 0
    tm = _block(L, 512)
    tq = _block(L, 256)
    tk_fox = _block(L // 2, 512)

    assert tm == tk_fox
    head_order = jnp.argsort(fox_f_bias[0])
    head_cols = (head_order[:, None] * HEAD_DIM + jnp.arange(HEAD_DIM)[None, :]).reshape(-1)
    qd, kd, vd, qa, ka, z, xbc, dt, fend = _inproj_ab(x, norm_mix[0], w_in_ab[0], fox_f_bias[0],
                                                      ssd_dt_bias[0], head_order, head_cols, tm)
    w_out0 = jnp.concatenate([jnp.take(w_out_ab[0, :ATT_W], head_cols, axis=0),
                              w_out_ab[0, ATT_W:]], axis=0)
    y_fox = _fox_attention(qd, qa, kd, ka, vd, fend.reshape(B, L // tm, LANES), tk_fox)
    y_ssd = _ssd(xbc, z, dt, ssd_conv_w[0], ssd_conv_b[0], ssd_a_log[0], ssd_d[0], ssd_norm[0])
    w_up = w_mlp_up.astype(BF16)
    w_down = w_mlp_down.astype(BF16)
    h = _mix_mlp(x, y_fox, y_ssd, w_out0, norm_mlp[0], w_up, w_down, 0, norm_final, tm,
                 final=False)

    sq, sk, sv, gq, gk, gv, gr, glow = _inproj_cd(h, norm_mix[1], w_in_cd[0], tm)
    y_sb = _sb_attention(sq, sk, sv, tq)
    y_gla = _gla(gq, gk, gv, glow, gr, gla_gate_w2[0], gla_gate_b[0], gla_norm[0])
    return _mix_mlp(h, y_sb, y_gla, w_out_cd[0], norm_mlp[1], w_up, w_down, 1, norm_final, tm,
                    final=True)
```
